```python
import math
import jax, jax.numpy as jnp
from jax import lax

D_MODEL = 2048
BATCH = 32
SEQ = 256
DEPTH = 2
DEC_BATCH = 8
DEC_SEQ = 1024
PAST_LEN = 256

GRID_W = 64
MIX_WIDTH = D_MODEL
GROUP_WIDTH = MIX_WIDTH // 4
EPS = 1e-6
HYENA_CH = GROUP_WIDTH
HYENA_SHORT = 3
HYENA_BANDS = 16
HYENA_POS_DIM = 1 + 2 * HYENA_BANDS
HYENA_FFN = 64
HYENA_FAST_DECAY = 0.3
HYENA_SLOW_DECAY = 1.5
HYENA_TARGET = 0.01
SGU_CH = GROUP_WIDTH
SGU_HEADS = 4
SGU_HEAD_CH = SGU_CH // SGU_HEADS
CHUNK = 128
ATTN_HEADS = 4
ATTN_KV_HEADS = 2
ATTN_GROUP = ATTN_HEADS // ATTN_KV_HEADS
ATTN_HEAD_DIM = GROUP_WIDTH // ATTN_HEADS
ROPE_THETA = 10000.0
Q_BLOCK = 128
DIFF_HEADS = 4
DIFF_V_DIM = GROUP_WIDTH // DIFF_HEADS
DIFF_QK_DIM = DIFF_V_DIM // 2
HY_COLS = 3 * HYENA_CH
SGU_COLS = 2 * SGU_CH
ATTN_Q_COLS = ATTN_HEADS * ATTN_HEAD_DIM
ATTN_KV_COLS = ATTN_KV_HEADS * ATTN_HEAD_DIM
ATTN_COLS = ATTN_Q_COLS + 2 * ATTN_KV_COLS
DIFF_QK_COLS = DIFF_HEADS * 2 * DIFF_QK_DIM
DIFF_COLS = 2 * DIFF_QK_COLS + DIFF_HEADS * DIFF_V_DIM
IN_COLS = HY_COLS + SGU_COLS + ATTN_COLS + DIFF_COLS
N_EXPERTS = 32
TOP_K = 4
D_FF = D_MODEL
SWIGLU_LIMIT = 7.0
SWIGLU_ALPHA = 1.702
MOE_BLOCK = 256

kernel_name = 'hybrid_diffusion_prefix_hyena_sgu_gqa_diffattn_moe'


def _rms(x, g):
    xf = x.astype(jnp.float32)
    y = xf * lax.rsqrt(jnp.mean(xf * xf, axis=-1, keepdims=True) + EPS)
    return y.astype(x.dtype) * g


def _layer_norm(x, g, b):
    xf = x.astype(jnp.float32)
    mu = jnp.mean(xf, axis=-1, keepdims=True)
    xc = xf - mu
    y = xc * lax.rsqrt(jnp.mean(xc * xc, axis=-1, keepdims=True) + EPS)
    return y.astype(x.dtype) * g + b


def _axial_rope(x):
    L, dh = x.shape[1], x.shape[-1]
    rows = L // GRID_W
    row = jnp.repeat(jnp.arange(rows), GRID_W).astype(jnp.float32)
    col = jnp.tile(jnp.arange(GRID_W), rows).astype(jnp.float32)
    half = dh // 2
    quarter = half // 2
    inv_freq = ROPE_THETA ** (-jnp.arange(quarter, dtype=jnp.float32) / quarter)

    def rotate(xh, pos):
        ang = pos[:, None] * inv_freq[None, :]
        cos = jnp.cos(ang)[None, :, None, :]
        sin = jnp.sin(ang)[None, :, None, :]
        xf = xh.astype(jnp.float32)
        x1, x2 = xf[..., :quarter], xf[..., quarter:]
        return jnp.concatenate([x1 * cos - x2 * sin, x2 * cos + x1 * sin], axis=-1)

    out = jnp.concatenate([rotate(x[..., :half], row), rotate(x[..., half:], col)], axis=-1)
    return out.astype(x.dtype)


def _short_conv(x, w, b):
    L = x.shape[1]
    pad = HYENA_SHORT // 2
    xp = jnp.pad(x, ((0, 0), (pad, pad), (0, 0)))
    y = b
    for j in range(HYENA_SHORT):
        y = y + xp[:, j:j + L] * w[j]
    return y


def _hyena_filters(L, w1, b1, w2, b2, w3, sin_freq):
    t = jnp.linspace(0.0, 1.0, L, dtype=jnp.float32)[:, None]
    bands = jnp.linspace(1e-4, HYENA_BANDS - 1, HYENA_BANDS, dtype=jnp.float32)[None, :]
    z = jnp.concatenate([t, jnp.cos(2 * math.pi * bands * t), jnp.sin(2 * math.pi * bands * t)], axis=-1)
    h = jnp.sin(sin_freq[0] * (z @ w1 + b1))
    h = jnp.sin(sin_freq[1] * (h @ w2 + b2))
    h = (h @ w3).astype(jnp.float32).reshape(L, 2, HYENA_CH)
    max_decay = math.log(HYENA_TARGET) / HYENA_FAST_DECAY
    min_decay = math.log(HYENA_TARGET) / HYENA_SLOW_DECAY
    deltas = jnp.abs(jnp.linspace(min_decay, max_decay, HYENA_CH, dtype=jnp.float32))
    window = jnp.exp(-t * deltas[None, :])
    h = h * window[:, None, :]
    return h[:, 0], h[:, 1]


def _long_conv(u, hf, hb, skip):
    B, L, C = u.shape
    k = jnp.concatenate([hf, jnp.zeros((1, C), jnp.float32), hb[1:][::-1]], axis=0)
    uf = jnp.fft.rfft(u.astype(jnp.float32), n=2 * L, axis=1)
    kf = jnp.fft.rfft(k, n=2 * L, axis=0)
    y = jnp.fft.irfft(uf * kf[None], n=2 * L, axis=1)[:, :L]
    return (y + u.astype(jnp.float32) * skip.astype(jnp.float32)).astype(u.dtype)


def _chunk_sgu(u, v, g, b, w_s, b_s):
    B, L, C = v.shape
    vn = _layer_norm(v, g, b).reshape(B, L // CHUNK, CHUNK, SGU_HEADS, SGU_HEAD_CH)
    mixed = jnp.einsum('hpq,bnqhc->bnphc', w_s, vn) + b_s.T[None, None, :, :, None]
    return u * mixed.reshape(B, L, C)


def _blocked_gqa(q, k, v, scale):
    B, Lq = q.shape[:2]
    qb = jnp.moveaxis(q.reshape(B, Lq // Q_BLOCK, Q_BLOCK, *q.shape[2:]), 1, 0)

    def block(qi):
        s = jnp.einsum('bqhgd,bkhd->bhgqk', qi, k).astype(jnp.float32) * scale
        pr = jax.nn.softmax(s, axis=-1).astype(v.dtype)
        return jnp.einsum('bhgqk,bkhd->bqhgd', pr, v)

    o = lax.map(block, qb)
    return jnp.moveaxis(o, 0, 1).reshape(B, Lq, -1)


def _blocked_diff(q, k, v, lam, scale):
    B, Lq = q.shape[:2]
    qb = jnp.moveaxis(q.reshape(B, Lq // Q_BLOCK, Q_BLOCK, *q.shape[2:]), 1, 0)

    def block(qi):
        s = jnp.einsum('bqhid,bkhid->bhiqk', qi, k).astype(jnp.float32) * scale
        pr = jax.nn.softmax(s, axis=-1)
        a = (pr[:, :, 0] - lam * pr[:, :, 1]).astype(v.dtype)
        return jnp.einsum('bhqk,bkhd->bqhd', a, v)

    o = lax.map(block, qb)
    return jnp.moveaxis(o, 0, 1)


def _mixer(h, p, l, ctx):
    B, L, _ = h.shape
    proj = h @ p['w_in'][l]
    a_in, s_in, c_in, d_in = jnp.split(proj, [HY_COLS, HY_COLS + SGU_COLS, HY_COLS + SGU_COLS + ATTN_COLS], axis=-1)

    a = _short_conv(a_in, p['hy_conv_w'][l], p['hy_conv_b'][l])
    x0, x1, hv = jnp.split(a, 3, axis=-1)
    hf, hb = _hyena_filters(L, p['hy_ffn_w1'][l], p['hy_ffn_b1'][l], p['hy_ffn_w2'][l],
                            p['hy_ffn_b2'][l], p['hy_ffn_w3'][l], p['hy_sin_freq'][l])
    y_a = x0 * _long_conv(hv * x1, hf, hb, p['hy_skip'][l])

    u, v_s = jnp.split(s_in, 2, axis=-1)
    y_b = _chunk_sgu(u, v_s, p['sgu_norm_g'][l], p['sgu_norm_b'][l], p['sgu_w'][l], p['sgu_b'][l])

    q = _rms(c_in[..., :ATTN_Q_COLS].reshape(B, L, ATTN_HEADS, ATTN_HEAD_DIM), p['attn_q_g'][l])
    k = _rms(c_in[..., ATTN_Q_COLS:ATTN_Q_COLS + ATTN_KV_COLS].reshape(B, L, ATTN_KV_HEADS, ATTN_HEAD_DIM), p['attn_k_g'][l])
    v = c_in[..., ATTN_Q_COLS + ATTN_KV_COLS:].reshape(B, L, ATTN_KV_HEADS, ATTN_HEAD_DIM)

    dq = d_in[..., :DIFF_QK_COLS].reshape(B, L, DIFF_HEADS, 2, DIFF_QK_DIM)
    dk = d_in[..., DIFF_QK_COLS:2 * DIFF_QK_COLS].reshape(B, L, DIFF_HEADS, 2, DIFF_QK_DIM)
    dv = d_in[..., 2 * DIFF_QK_COLS:].reshape(B, L, DIFF_HEADS, DIFF_V_DIM)

    if ctx is None:
        new_ctx = (k, v, dk, dv)
        ak, av, dkk, dvv = k, v, dk, dv
    else:
        q = _axial_rope(q)
        k = _axial_rope(k)
        dq = _axial_rope(dq.reshape(B, L, 2 * DIFF_HEADS, DIFF_QK_DIM)).reshape(dq.shape)
        dk = _axial_rope(dk.reshape(B, L, 2 * DIFF_HEADS, DIFF_QK_DIM)).reshape(dk.shape)
        ck, cv, cdk, cdv = ctx
        ak = jnp.concatenate([ck.astype(k.dtype), k], axis=1)
        av = jnp.concatenate([cv.astype(v.dtype), v], axis=1)
        dkk = jnp.concatenate([cdk.astype(dk.dtype), dk], axis=1)
        dvv = jnp.concatenate([cdv.astype(dv.dtype), dv], axis=1)
        new_ctx = None

    y_c = _blocked_gqa(q.reshape(B, L, ATTN_KV_HEADS, ATTN_GROUP, ATTN_HEAD_DIM), ak, av, ATTN_HEAD_DIM ** -0.5)

    lambda_init = 0.8 - 0.6 * math.exp(-0.3 * l)
    dl = p['diff_lambda'][l].astype(jnp.float32)
    lam = jnp.exp(jnp.sum(dl[0] * dl[1])) - jnp.exp(jnp.sum(dl[2] * dl[3])) + lambda_init
    y_d = _blocked_diff(dq, dkk, dvv, lam, DIFF_QK_DIM ** -0.5)
    y_d = (_rms(y_d, p['diff_norm_g'][l]) * (1.0 - lambda_init)).reshape(B, L, -1)

    out = jnp.concatenate([y_a, y_b, y_c, y_d], axis=-1) @ p['w_out'][l]
    return out, new_ctx


def _moe(h, p, l):
    T = h.shape[0]
    logits = (h @ p['w_router'][l]).astype(jnp.float32) + p['b_router'][l].astype(jnp.float32)
    top_v, top_i = lax.top_k(logits, TOP_K)
    gates = jax.nn.softmax(top_v, axis=-1).reshape(-1)
    A = T * TOP_K
    flat_e = top_i.reshape(-1).astype(jnp.int32)
    order = jnp.argsort(flat_e).astype(jnp.int32)
    sorted_e = flat_e[order]
    counts = jnp.zeros((N_EXPERTS,), jnp.int32).at[flat_e].add(1)
    padded = (counts + MOE_BLOCK - 1) // MOE_BLOCK * MOE_BLOCK
    pad_end = jnp.cumsum(padded)
    pad_start = pad_end - padded
    seg_start = jnp.cumsum(counts) - counts
    dest = pad_start[sorted_e] + jnp.arange(A, dtype=jnp.int32) - seg_start[sorted_e]
    n_blocks = -(-A // MOE_BLOCK) + N_EXPERTS
    slot = jnp.full((n_blocks * MOE_BLOCK,), A, jnp.int32).at[dest].set(order)
    filled = slot < A
    slot_c = jnp.minimum(slot, A - 1)
    slot_tok = jnp.where(filled, slot_c // TOP_K, 0)
    slot_gate = jnp.where(filled, gates[slot_c], 0.0).astype(h.dtype)
    block_e = jnp.minimum(jnp.searchsorted(pad_end, jnp.arange(n_blocks, dtype=jnp.int32) * MOE_BLOCK, side='right'),
                          N_EXPERTS - 1)
    w_gu, b_gu, w_dn, b_dn = p['w_gate_up'][l], p['b_gate_up'][l], p['w_down'][l], p['b_down'][l]

    def expert_block(args):
        toks, e = args
        xb = h[toks]
        gu = xb @ w_gu[e] + b_gu[e]
        gate = jnp.minimum(gu[:, :D_FF], SWIGLU_LIMIT)
        up = jnp.clip(gu[:, D_FF:], -SWIGLU_LIMIT, SWIGLU_LIMIT)
        glu = gate * jax.nn.sigmoid(gate * SWIGLU_ALPHA)
        return ((up + 1.0) * glu) @ w_dn[e] + b_dn[e]

    y_slots = lax.map(expert_block, (slot_tok.reshape(n_blocks, MOE_BLOCK), block_e))
    return jnp.zeros_like(h).at[slot_tok].add(y_slots.reshape(-1, D_MODEL) * slot_gate[:, None])


def _layer(x, cond, p, l, ctx):
    mod = jax.nn.silu(cond) @ p['w_mod'][l] + p['b_mod'][l]
    sh1, sc1, g1, sh2, sc2, g2 = jnp.split(mod[:, None, :], 6, axis=-1)
    ng = p['norm_g'][l]
    h = _rms(x, ng[0]) * (1.0 + sc1) + sh1
    m, new_ctx = _mixer(h, p, l, ctx)
    x = x + g1 * _rms(m, ng[1])
    h = _rms(x, ng[2]) * (1.0 + sc2) + sh2
    f = _moe(h.reshape(-1, D_MODEL), p, l).reshape(x.shape)
    x = x + g2 * _rms(f, ng[3])
    return x, new_ctx


def setup_inputs(seed: int = 0) -> dict:
    key = jax.random.key(seed)
    k = jax.random.split(key, 36)

    def nrm(kk, shape, scale=1.0):
        return jax.random.normal(kk, shape, jnp.float32) * scale

    def near_one(kk, shape):
        return 1.0 + 0.05 * jax.random.normal(kk, shape, jnp.float32)

    return {
        'x_prompt': nrm(k[0], (BATCH, SEQ, D_MODEL)),
        'x_sample': nrm(k[1], (DEC_BATCH, DEC_SEQ, D_MODEL)),
        'cache_attn_k': nrm(k[2], (DEC_BATCH, DEPTH, PAST_LEN, ATTN_KV_HEADS, ATTN_HEAD_DIM)),
        'cache_attn_v': nrm(k[3], (DEC_BATCH, DEPTH, PAST_LEN, ATTN_KV_HEADS, ATTN_HEAD_DIM)),
        'cache_diff_k': nrm(k[4], (DEC_BATCH, DEPTH, PAST_LEN, DIFF_HEADS, 2, DIFF_QK_DIM)),
        'cache_diff_v': nrm(k[5], (DEC_BATCH, DEPTH, PAST_LEN, DIFF_HEADS, DIFF_V_DIM)),
        'c': nrm(k[6], (DEC_BATCH, D_MODEL)),
        'c_ctx': nrm(k[7], (D_MODEL,)),
        'w_mod': nrm(k[8], (DEPTH, D_MODEL, 6 * D_MODEL), 0.5 * D_MODEL ** -0.5),
        'b_mod': nrm(k[9], (DEPTH, 6 * D_MODEL), 0.02),
        'norm_g': near_one(k[10], (DEPTH, 4, D_MODEL)),
        'w_in': nrm(k[11], (DEPTH, D_MODEL, IN_COLS), D_MODEL ** -0.5),
        'w_out': nrm(k[12], (DEPTH, MIX_WIDTH, D_MODEL), MIX_WIDTH ** -0.5),
        'hy_conv_w': nrm(k[13], (DEPTH, HYENA_SHORT, HY_COLS), HYENA_SHORT ** -0.5),
        'hy_conv_b': nrm(k[14], (DEPTH, HY_COLS), 0.02),
        'hy_ffn_w1': nrm(k[15], (DEPTH, HYENA_POS_DIM, HYENA_FFN), HYENA_POS_DIM ** -0.5),
        'hy_ffn_b1': nrm(k[16], (DEPTH, HYENA_FFN), 0.1),
        'hy_ffn_w2': nrm(k[17], (DEPTH, HYENA_FFN, HYENA_FFN), HYENA_FFN ** -0.5),
        'hy_ffn_b2': nrm(k[18], (DEPTH, HYENA_FFN), 0.1),
        'hy_ffn_w3': nrm(k[19], (DEPTH, HYENA_FFN, 2 * HYENA_CH), 0.1 * HYENA_FFN ** -0.5),
        'hy_sin_freq': near_one(k[20], (DEPTH, 2, HYENA_FFN)),
        'hy_skip': nrm(k[21], (DEPTH, HYENA_CH), 0.5),
        'sgu_norm_g': near_one(k[22], (DEPTH, SGU_CH)),
        'sgu_norm_b': nrm(k[23], (DEPTH, SGU_CH), 0.02),
        'sgu_w': nrm(k[24], (DEPTH, SGU_HEADS, CHUNK, CHUNK), CHUNK ** -0.5),
        'sgu_b': near_one(k[25], (DEPTH, SGU_HEADS, CHUNK)),
        'attn_q_g': near_one(k[26], (DEPTH, ATTN_HEAD_DIM)),
        'attn_k_g': near_one(k[27], (DEPTH, ATTN_HEAD_DIM)),
        'diff_lambda': nrm(k[28], (DEPTH, 4, DIFF_QK_DIM), 0.1),
        'diff_norm_g': near_one(k[29], (DEPTH, DIFF_V_DIM)),
        'w_router': nrm(k[30], (DEPTH, D_MODEL, N_EXPERTS), D_MODEL ** -0.5),
        'b_router': nrm(k[31], (DEPTH, N_EXPERTS), 0.01),
        'w_gate_up': nrm(k[32], (DEPTH, N_EXPERTS, D_MODEL, 2 * D_FF), D_MODEL ** -0.5),
        'b_gate_up': nrm(k[33], (DEPTH, N_EXPERTS, 2 * D_FF), 0.02),
        'w_down': nrm(k[34], (DEPTH, N_EXPERTS, D_FF, D_MODEL), D_FF ** -0.5),
        'b_down': nrm(k[35], (DEPTH, N_EXPERTS, D_MODEL), 0.02),
    }


def reference(x_prompt, x_sample, cache_attn_k, cache_attn_v, cache_diff_k, cache_diff_v, c, c_ctx,
              w_mod, b_mod, norm_g, w_in, w_out, hy_conv_w, hy_conv_b, hy_ffn_w1, hy_ffn_b1,
              hy_ffn_w2, hy_ffn_b2, hy_ffn_w3, hy_sin_freq, hy_skip, sgu_norm_g, sgu_norm_b, sgu_w,
              sgu_b, attn_q_g, attn_k_g, diff_lambda, diff_norm_g, w_router, b_router, w_gate_up,
              b_gate_up, w_down, b_down):
    p = dict(w_mod=w_mod, b_mod=b_mod, norm_g=norm_g, w_in=w_in, w_out=w_out,
             hy_conv_w=hy_conv_w, hy_conv_b=hy_conv_b, hy_ffn_w1=hy_ffn_w1, hy_ffn_b1=hy_ffn_b1,
             hy_ffn_w2=hy_ffn_w2, hy_ffn_b2=hy_ffn_b2, hy_ffn_w3=hy_ffn_w3, hy_sin_freq=hy_sin_freq,
             hy_skip=hy_skip, sgu_norm_g=sgu_norm_g, sgu_norm_b=sgu_norm_b, sgu_w=sgu_w, sgu_b=sgu_b,
             attn_q_g=attn_q_g, attn_k_g=attn_k_g, diff_lambda=diff_lambda, diff_norm_g=diff_norm_g,
             w_router=w_router, b_router=b_router, w_gate_up=w_gate_up, b_gate_up=b_gate_up,
             w_down=w_down, b_down=b_down)

    xp = x_prompt
    ks, vs, dks, dvs = [], [], [], []
    for l in range(DEPTH):
        xp, (ck, cv, cdk, cdv) = _layer(xp, c_ctx[None, :], p, l, None)
        ks.append(ck)
        vs.append(cv)
        dks.append(cdk)
        dvs.append(cdv)

    xs = x_sample
    for l in range(DEPTH):
        ctx = (cache_attn_k[:, l], cache_attn_v[:, l], cache_diff_k[:, l], cache_diff_v[:, l])
        xs, _ = _layer(xs, c, p, l, ctx)

    new_attn_k = jnp.stack(ks, axis=1)
    new_attn_v = jnp.stack(vs, axis=1)
    new_diff_k = jnp.stack(dks, axis=1)
    new_diff_v = jnp.stack(dvs, axis=1)
    return (xp, xs, new_attn_k, new_attn_v, new_diff_k, new_diff_v)
```

```python
import functools
import math

import jax
import jax.numpy as jnp
from jax import lax
from jax.experimental import pallas as pl
from jax.experimental.pallas import tpu as pltpu

F32 = jnp.float32
BF16 = jnp.bfloat16
HIGHEST = lax.Precision.HIGHEST

D_MODEL = 2048
DEPTH = 2
GRID_W = 64
GROUP_WIDTH = 512
EPS = 1e-6
IN_COLS = 5120
HYENA_BANDS = 16
HYENA_POS_DIM = 1 + 2 * HYENA_BANDS
HYENA_FFN = 64
HYENA_FAST_DECAY = 0.3
HYENA_SLOW_DECAY = 1.5
HYENA_TARGET = 0.01
POS_PAD = 128
CHUNK = 128
SGU_HEADS = 4
ATTN_HEADS = 4
ATTN_KV_HEADS = 2
ATTN_HEAD_DIM = 128
ROPE_THETA = 10000.0
DIFF_HEADS = 4
DIFF_QK_DIM = 64
DIFF_V_DIM = 128
N_EXPERTS = 32
TOP_K = 4
D_FF = 2048
SWIGLU_LIMIT = 7.0
SWIGLU_ALPHA = 1.702
ROUTER_PAD = 128

IN_BM, IN_BN = 1024, 1024
OUT_BM = 256
MOE_BM = 512
GM1_TN = 512
GM2_TN = 1024
FIN_BM = 512
COND_ROWS = 16
MOD_TN = 1024
HY_CC = 256
Q_CHUNK = 512
MIB = 2 ** 20


def _cp(sem, vmem_mib):
    return pltpu.CompilerParams(dimension_semantics=sem, vmem_limit_bytes=vmem_mib * MIB)


def _mod_kernel(c_ref, w_ref, b_ref, o_ref):
    c = c_ref[...]
    s = c * jax.nn.sigmoid(c)
    o_ref[...] = jnp.dot(s.astype(BF16), w_ref[...].astype(BF16), preferred_element_type=F32) + b_ref[...]


def _modulation(cond, w_mod, b_mod):
    n = w_mod.shape[-1]
    return pl.pallas_call(
        _mod_kernel,
        grid=(DEPTH, n // MOD_TN),
        in_specs=[
            pl.BlockSpec((COND_ROWS, D_MODEL), lambda l, j: (0, 0)),
            pl.BlockSpec((None, D_MODEL, MOD_TN), lambda l, j: (l, 0, j)),
            pl.BlockSpec((None, 1, MOD_TN), lambda l, j: (l, 0, j)),
        ],
        out_specs=pl.BlockSpec((None, COND_ROWS, MOD_TN), lambda l, j: (l, 0, j)),
        out_shape=jax.ShapeDtypeStruct((DEPTH, COND_ROWS, n), F32),
        compiler_params=_cp(("arbitrary", "arbitrary"), 40),
        name="modulation",
    )(cond, w_mod, b_mod.reshape(DEPTH, 1, n))


def _mod_row(i, bm, tp, ls):
    nbp = tp // bm
    return jnp.where(i < nbp, 0, 1 + (i - nbp) // (ls // bm))


def _in_proj_kernel(x_ref, mod_ref, g_ref, w_ref, o_ref, h_ref):
    @pl.when(pl.program_id(1) == 0)
    def _():
        x = x_ref[...]
        r = lax.rsqrt(jnp.mean(x * x, axis=-1, keepdims=True) + EPS)
        h = (x * r) * g_ref[...] * (1.0 + mod_ref[0, 1:2, :]) + mod_ref[0, 0:1, :]
        h_ref[...] = h.astype(BF16)

    o_ref[...] = jnp.dot(h_ref[...], w_ref[...], preferred_element_type=F32)


def _in_proj(x, mod, g, w, tp, ls):
    t = x.shape[0]
    bm = min(IN_BM, ls, tp)
    return pl.pallas_call(
        _in_proj_kernel,
        grid=(t // bm, IN_COLS // IN_BN),
        in_specs=[
            pl.BlockSpec((bm, D_MODEL), lambda i, j: (i, 0)),
            pl.BlockSpec((1, 6, D_MODEL), lambda i, j: (_mod_row(i, bm, tp, ls), 0, 0)),
            pl.BlockSpec((1, D_MODEL), lambda i, j: (0, 0)),
            pl.BlockSpec((D_MODEL, IN_BN), lambda i, j: (0, j)),
        ],
        out_specs=pl.BlockSpec((bm, IN_BN), lambda i, j: (i, j)),
        out_shape=jax.ShapeDtypeStruct((t, IN_COLS), F32),
        scratch_shapes=[pltpu.VMEM((bm, D_MODEL), BF16)],
        compiler_params=_cp(("arbitrary", "arbitrary"), 56),
        name="in_proj",
    )(x, mod, g, w)


def _dft_tables(L):
    k = jnp.arange(L, dtype=jnp.int32)
    m = (k[:, None] * k[None, :]) % (2 * L)
    ang = m.astype(F32) * (math.pi / L)
    cm = jnp.cos(ang)
    sm = jnp.sin(ang)
    alt = jnp.where(k % 2 == 0, 1.0, -1.0).astype(F32)
    fwd = jnp.concatenate([cm, (-sm).at[0].set(alt)], axis=0)
    wk = jnp.where(k == 0, 1.0, 2.0).astype(F32) / (2 * L)
    ci = cm * wk[None, :]
    si = (-sm / L).at[:, 0].set(alt / (2 * L))
    inv = jnp.concatenate([ci, si], axis=1)
    alt8 = jnp.zeros((8, L), F32).at[0].set(alt)
    return cm, sm, alt8, fwd.astype(BF16), inv.astype(BF16)


def _filter_consts(L):
    t = jnp.linspace(0.0, 1.0, L, dtype=F32)[:, None]
    bands = jnp.linspace(1e-4, HYENA_BANDS - 1, HYENA_BANDS, dtype=F32)[None, :]
    z = jnp.concatenate([t, jnp.cos(2 * math.pi * bands * t), jnp.sin(2 * math.pi * bands * t)], axis=-1)
    z = jnp.pad(z, ((0, 0), (0, POS_PAD - HYENA_POS_DIM)))
    max_decay = math.log(HYENA_TARGET) / HYENA_FAST_DECAY
    min_decay = math.log(HYENA_TARGET) / HYENA_SLOW_DECAY
    deltas = jnp.abs(jnp.linspace(min_decay, max_decay, GROUP_WIDTH, dtype=F32))
    window = jnp.exp(-t * deltas[None, :])
    return z, window


def _filter_kernel(z_ref, w1_ref, b1_ref, w2_ref, b2_ref, w3_ref, fr_ref, win_ref, cm_ref, sm_ref, alt_ref,
                   kre_ref, kim_ref):
    c = GROUP_WIDTH
    h = jnp.dot(z_ref[...], w1_ref[...], precision=HIGHEST, preferred_element_type=F32) + b1_ref[...]
    h = jnp.sin(fr_ref[0:1, :] * h)
    h = jnp.dot(h, w2_ref[...], precision=HIGHEST, preferred_element_type=F32) + b2_ref[...]
    h = jnp.sin(fr_ref[1:2, :] * h)
    hfb = jnp.dot(h, w3_ref[...], precision=HIGHEST, preferred_element_type=F32)
    win = win_ref[...]
    hf = hfb[:, :c] * win
    hb = hfb[:, c:] * win
    row = lax.broadcasted_iota(jnp.int32, hf.shape, 0)
    g = hf + jnp.where(row == 0, 0.0, hb)
    kre = jnp.dot(cm_ref[...], g, precision=HIGHEST, preferred_element_type=F32)
    kim = jnp.dot(sm_ref[...], hb - hf, precision=HIGHEST, preferred_element_type=F32)
    kn = jnp.dot(alt_ref[...], g, precision=HIGHEST, preferred_element_type=F32)[0:1, :]
    kre_ref[...] = kre
    kim_ref[...] = jnp.where(row == 0, kn, kim)


def _hyena_filter(L, z, window, cm, sm, alt8, w1, b1, w2, b2, w3, freq):
    w1p = jnp.pad(w1, ((0, POS_PAD - HYENA_POS_DIM), (0, 0)))
    return pl.pallas_call(
        _filter_kernel,
        out_shape=(jax.ShapeDtypeStruct((L, GROUP_WIDTH), F32), jax.ShapeDtypeStruct((L, GROUP_WIDTH), F32)),
        compiler_params=pltpu.CompilerParams(vmem_limit_bytes=56 * MIB),
        name="hyena_filter",
    )(z, w1p, b1.reshape(1, -1), w2, b2.reshape(1, -1), w3, freq, window, cm, sm, alt8)


def _short_conv(a, w_ref, b_ref, row, L):
    prev = jnp.where(row == 0, 0.0, pltpu.roll(a, 1, 0))
    nxt = jnp.where(row == L - 1, 0.0, pltpu.roll(a, L - 1, 0))
    return b_ref[...] + prev * w_ref[0:1, :] + a * w_ref[1:2, :] + nxt * w_ref[2:3, :]


def _hyena_kernel(a0_ref, a1_ref, a2_ref, w0_ref, w1_ref, w2_ref, b0_ref, b1_ref, b2_ref, kre_ref, kim_ref,
                  skip_ref, fwd_ref, inv_ref, o_ref, *, L):
    row = lax.broadcasted_iota(jnp.int32, (L, HY_CC), 0)
    x1 = _short_conv(a1_ref[...], w1_ref, b1_ref, row, L)
    hv = _short_conv(a2_ref[...], w2_ref, b2_ref, row, L)
    u = hv * x1
    spec = jnp.dot(fwd_ref[...], u.astype(BF16), preferred_element_type=F32)
    ure, uim = spec[:L], spec[L:]
    kre, kim = kre_ref[...], kim_ref[...]
    first = row == 0
    yre = ure * kre - jnp.where(first, 0.0, uim * kim)
    yim = jnp.where(first, uim * kim, ure * kim + uim * kre)
    yspec = jnp.concatenate([yre, yim], axis=0).astype(BF16)
    y = jnp.dot(inv_ref[...], yspec, preferred_element_type=F32)
    x0 = _short_conv(a0_ref[...], w0_ref, b0_ref, row, L)
    o_ref[...] = (x0 * (y + u * skip_ref[...])).astype(BF16)


def _hyena(proj, row_off, B, L, conv_w, conv_b, kre, kim, skip, fwd, inv):
    ncc = GROUP_WIDTH // HY_CC

    def a_spec(part):
        return pl.BlockSpec((L, HY_CC), lambda b, cc: (row_off + b, part * ncc + cc))

    def w_spec(part, rows):
        return pl.BlockSpec((rows, HY_CC), lambda b, cc: (0, part * ncc + cc))

    chan = pl.BlockSpec((L, HY_CC), lambda b, cc: (0, cc))
    conv_b = conv_b.reshape(1, -1)
    return pl.pallas_call(
        functools.partial(_hyena_kernel, L=L),
        grid=(B, ncc),
        in_specs=[a_spec(0), a_spec(1), a_spec(2), w_spec(0, 3), w_spec(1, 3), w_spec(2, 3),
                  w_spec(0, 1), w_spec(1, 1), w_spec(2, 1), chan, chan,
                  pl.BlockSpec((1, HY_CC), lambda b, cc: (0, cc)),
                  pl.BlockSpec((2 * L, L), lambda b, cc: (0, 0)),
                  pl.BlockSpec((L, 2 * L), lambda b, cc: (0, 0))],
        out_specs=pl.BlockSpec((L, HY_CC), lambda b, cc: (b, cc)),
        out_shape=jax.ShapeDtypeStruct((B * L, GROUP_WIDTH), BF16),
        compiler_params=_cp(("arbitrary", "arbitrary"), 56),
        name="hyena_conv",
    )(proj, proj, proj, conv_w, conv_w, conv_w, conv_b, conv_b, conv_b, kre, kim, skip.reshape(1, -1), fwd, inv)


def _sgu_kernel(u_ref, v_ref, g_ref, b_ref, w_ref, bs_ref, o_ref, *, L):
    v = v_ref[...]
    mu = jnp.mean(v, axis=-1, keepdims=True)
    xc = v - mu
    vn = xc * lax.rsqrt(jnp.mean(xc * xc, axis=-1, keepdims=True) + EPS) * g_ref[...] + b_ref[...]
    vnb = vn.astype(BF16)
    hc = GROUP_WIDTH // SGU_HEADS
    for h in range(SGU_HEADS):
        wh = w_ref[h].astype(BF16)
        bias = bs_ref[h]
        cols = slice(h * hc, (h + 1) * hc)
        for n in range(L // CHUNK):
            rows = slice(n * CHUNK, (n + 1) * CHUNK)
            mixed = jnp.dot(wh, vnb[rows, cols], preferred_element_type=F32) + bias
            o_ref[rows, cols] = (u_ref[rows, cols] * mixed).astype(BF16)


def _sgu(proj, row_off, B, L, g, b, w_s, b_s):
    hc = GROUP_WIDTH // SGU_HEADS
    bias = jnp.broadcast_to(b_s[:, :, None], (SGU_HEADS, CHUNK, hc))
    return pl.pallas_call(
        functools.partial(_sgu_kernel, L=L),
        grid=(B,),
        in_specs=[pl.BlockSpec((L, GROUP_WIDTH), lambda i: (row_off + i, 3)),
                  pl.BlockSpec((L, GROUP_WIDTH), lambda i: (row_off + i, 4)),
                  pl.BlockSpec((1, GROUP_WIDTH), lambda i: (0, 0)),
                  pl.BlockSpec((1, GROUP_WIDTH), lambda i: (0, 0)),
                  pl.BlockSpec((SGU_HEADS, CHUNK, CHUNK), lambda i: (0, 0, 0)),
                  pl.BlockSpec((SGU_HEADS, CHUNK, hc), lambda i: (0, 0, 0))],
        out_specs=pl.BlockSpec((L, GROUP_WIDTH), lambda i: (i, 0)),
        out_shape=jax.ShapeDtypeStruct((B * L, GROUP_WIDTH), BF16),
        compiler_params=_cp(("arbitrary",), 40),
        name="sgu",
    )(proj, proj, g.reshape(1, -1), b.reshape(1, -1), w_s, bias)


def _rope_tables(L, dh, width):
    quarter = dh // 4
    pos = jnp.arange(L)
    rowp = (pos // GRID_W).astype(F32)
    colp = (pos % GRID_W).astype(F32)
    inv_freq = ROPE_THETA ** (-jnp.arange(quarter, dtype=F32) / quarter)
    ar = rowp[:, None] * inv_freq[None, :]
    ac = colp[:, None] * inv_freq[None, :]
    cos = jnp.concatenate([jnp.cos(ar), jnp.cos(ar), jnp.cos(ac), jnp.cos(ac)], axis=-1)
    sin = jnp.concatenate([-jnp.sin(ar), jnp.sin(ar), -jnp.sin(ac), jnp.sin(ac)], axis=-1)
    reps = width // dh
    return jnp.tile(cos, (1, reps)), jnp.tile(sin, (1, reps))


def _rope(x, cos, sin, quarter):
    w = x.shape[-1]
    lane = lax.broadcasted_iota(jnp.int32, x.shape, 1)
    partner = jnp.where((lane & quarter) != 0, pltpu.roll(x, quarter, 1), pltpu.roll(x, w - quarter, 1))
    return x * cos + partner * sin


def _head_rms(x, g, heads, dh):
    outs = []
    for h in range(heads):
        xh = x[:, h * dh:(h + 1) * dh]
        outs.append(xh * lax.rsqrt(jnp.mean(xh * xh, axis=-1, keepdims=True) + EPS) * g)
    return jnp.concatenate(outs, axis=-1)


def _softmax_pv(q, k, v):
    s = lax.dot_general(q, k, (((1,), (1,)), ((), ())), preferred_element_type=F32)
    e = jnp.exp(s - jnp.max(s, axis=-1, keepdims=True))
    inv_l = 1.0 / jnp.sum(e, axis=-1, keepdims=True)
    return jnp.dot(e.astype(BF16), v, preferred_element_type=F32), inv_l


def _gqa_body(q, k_ref, v_ref, o_ref, L):
    dh = ATTN_HEAD_DIM
    group = ATTN_HEADS // ATTN_KV_HEADS
    qs = (q * dh ** -0.5).astype(BF16)
    qc = min(Q_CHUNK, L)
    for h in range(ATTN_HEADS):
        kv = h // group
        kh = k_ref[:, kv * dh:(kv + 1) * dh]
        vh = v_ref[:, kv * dh:(kv + 1) * dh]
        for c in range(L // qc):
            rows = slice(c * qc, (c + 1) * qc)
            pv, inv_l = _softmax_pv(qs[rows, h * dh:(h + 1) * dh], kh, vh)
            o_ref[rows, h * dh:(h + 1) * dh] = (pv * inv_l).astype(BF16)


def _gqa_ctx_kernel(q_ref, k_ref, v_ref, gq_ref, gk_ref, o_ref, ko_ref, kb_ref, vb_ref, *, L):
    q = _head_rms(q_ref[...], gq_ref[...], ATTN_HEADS, ATTN_HEAD_DIM)
    k = _head_rms(k_ref[...], gk_ref[...], ATTN_KV_HEADS, ATTN_HEAD_DIM)
    ko_ref[...] = k
    kb_ref[...] = k.astype(BF16)
    vb_ref[...] = v_ref[...].astype(BF16)
    _gqa_body(q, kb_ref, vb_ref, o_ref, L)


def _gqa_lat_kernel(q_ref, k_ref, v_ref, gq_ref, gk_ref, cq_ref, sq_ref, ck_ref, sk_ref, pk_ref, pv_ref, o_ref,
                    kb_ref, vb_ref, *, L, P):
    quarter = ATTN_HEAD_DIM // 4
    q = _rope(_head_rms(q_ref[...], gq_ref[...], ATTN_HEADS, ATTN_HEAD_DIM), cq_ref[...], sq_ref[...], quarter)
    k = _rope(_head_rms(k_ref[...], gk_ref[...], ATTN_KV_HEADS, ATTN_HEAD_DIM), ck_ref[...], sk_ref[...], quarter)
    kb_ref[0:P, :] = pk_ref[...].astype(BF16)
    kb_ref[P:P + L, :] = k.astype(BF16)
    vb_ref[0:P, :] = pv_ref[...].astype(BF16)
    vb_ref[P:P + L, :] = v_ref[...].astype(BF16)
    _gqa_body(q, kb_ref, vb_ref, o_ref, L)


def _gqa(proj, row_off, B, L, gq, gk, l, past=None, tables=None):
    qw = ATTN_HEADS * ATTN_HEAD_DIM
    kw = ATTN_KV_HEADS * ATTN_HEAD_DIM
    qoff = 5
    koff = (2560 + qw) // kw
    in_specs = [pl.BlockSpec((L, qw), lambda i: (row_off + i, qoff)),
                pl.BlockSpec((L, kw), lambda i: (row_off + i, koff)),
                pl.BlockSpec((L, kw), lambda i: (row_off + i, koff + 1)),
                pl.BlockSpec((1, ATTN_HEAD_DIM), lambda i: (0, 0)),
                pl.BlockSpec((1, ATTN_HEAD_DIM), lambda i: (0, 0))]
    args = [proj, proj, proj, gq.reshape(1, -1), gk.reshape(1, -1)]
    y_spec = pl.BlockSpec((L, qw), lambda i: (i, 0))
    y_shape = jax.ShapeDtypeStruct((B * L, qw), BF16)
    if past is None:
        return pl.pallas_call(
            functools.partial(_gqa_ctx_kernel, L=L),
            grid=(B,),
            in_specs=in_specs,
            out_specs=(y_spec, pl.BlockSpec((L, kw), lambda i: (i, 0))),
            out_shape=(y_shape, jax.ShapeDtypeStruct((B * L, kw), F32)),
            scratch_shapes=[pltpu.VMEM((L, kw), BF16), pltpu.VMEM((L, kw), BF16)],
            compiler_params=_cp(("arbitrary",), 40),
            name="gqa_ctx",
        )(*args)
    pk, pv = past
    P = pk.shape[2]
    cq, sq, ck, sk = tables
    const = lambda w: pl.BlockSpec((L, w), lambda i: (0, 0))
    cache = pl.BlockSpec((None, None, P, kw), lambda i: (i, l, 0, 0))
    return pl.pallas_call(
        functools.partial(_gqa_lat_kernel, L=L, P=P),
        grid=(B,),
        in_specs=in_specs + [const(qw), const(qw), const(kw), const(kw), cache, cache],
        out_specs=y_spec,
        out_shape=y_shape,
        scratch_shapes=[pltpu.VMEM((P + L, kw), BF16), pltpu.VMEM((P + L, kw), BF16)],
        compiler_params=_cp(("arbitrary",), 56),
        name="gqa_lat",
    )(*args, cq, sq, ck, sk, pk, pv)


def _diff_body(dq, k_ref, v_ref, dl_ref, g_ref, o_ref, L, lambda_init):
    dv = DIFF_V_DIM
    dl = dl_ref[...]
    lam = (jnp.exp(jnp.sum(dl[0:1] * dl[1:2], axis=-1, keepdims=True))
           - jnp.exp(jnp.sum(dl[2:3] * dl[3:4], axis=-1, keepdims=True)) + lambda_init)
    qs = dq * DIFF_QK_DIM ** -0.5
    first_map = lax.broadcasted_iota(jnp.int32, (1, dv), 1) < DIFF_QK_DIM
    qc = min(Q_CHUNK, L)
    for h in range(DIFF_HEADS):
        cols = slice(h * dv, (h + 1) * dv)
        kh = k_ref[:, cols]
        vh = v_ref[:, cols]
        for c in range(L // qc):
            rows = slice(c * qc, (c + 1) * qc)
            qh = qs[rows, cols]
            pv0, inv0 = _softmax_pv(jnp.where(first_map, qh, 0.0).astype(BF16), kh, vh)
            pv1, inv1 = _softmax_pv(jnp.where(first_map, 0.0, qh).astype(BF16), kh, vh)
            o = pv0 * inv0 - pv1 * (lam * inv1)
            r = lax.rsqrt(jnp.mean(o * o, axis=-1, keepdims=True) + EPS)
            o_ref[rows, cols] = ((o * r) * g_ref[...] * (1.0 - lambda_init)).astype(BF16)


def _diff_ctx_kernel(q_ref, k_ref, v_ref, dl_ref, g_ref, o_ref, kb_ref, vb_ref, *, L, lambda_init):
    kb_ref[...] = k_ref[...].astype(BF16)
    vb_ref[...] = v_ref[...].astype(BF16)
    _diff_body(q_ref[...], kb_ref, vb_ref, dl_ref, g_ref, o_ref, L, lambda_init)


def _diff_lat_kernel(q_ref, k_ref, v_ref, dl_ref, g_ref, cos_ref, sin_ref, pk_ref, pv_ref, o_ref, kb_ref, vb_ref,
                     *, L, P, lambda_init):
    quarter = DIFF_QK_DIM // 4
    cos, sin = cos_ref[...], sin_ref[...]
    q = _rope(q_ref[...], cos, sin, quarter)
    k = _rope(k_ref[...], cos, sin, quarter)
    kb_ref[0:P, :] = pk_ref[...].astype(BF16)
    kb_ref[P:P + L, :] = k.astype(BF16)
    vb_ref[0:P, :] = pv_ref[...].astype(BF16)
    vb_ref[P:P + L, :] = v_ref[...].astype(BF16)
    _diff_body(q, kb_ref, vb_ref, dl_ref, g_ref, o_ref, L, lambda_init)


def _diff(proj, row_off, B, L, dl, g, l, past=None, tables=None):
    w = GROUP_WIDTH
    lambda_init = 0.8 - 0.6 * math.exp(-0.3 * l)
    in_specs = [pl.BlockSpec((L, w), lambda i: (row_off + i, 7)),
                pl.BlockSpec((L, w), lambda i: (row_off + i, 8)),
                pl.BlockSpec((L, w), lambda i: (row_off + i, 9)),
                pl.BlockSpec((4, DIFF_QK_DIM), lambda i: (0, 0)),
                pl.BlockSpec((1, DIFF_V_DIM), lambda i: (0, 0))]
    args = [proj, proj, proj, dl, g.reshape(1, -1)]
    y_spec = pl.BlockSpec((L, w), lambda i: (i, 0))
    y_shape = jax.ShapeDtypeStruct((B * L, w), BF16)
    if past is None:
        return pl.pallas_call(
            functools.partial(_diff_ctx_kernel, L=L, lambda_init=lambda_init),
            grid=(B,),
            in_specs=in_specs,
            out_specs=y_spec,
            out_shape=y_shape,
            scratch_shapes=[pltpu.VMEM((L, w), BF16), pltpu.VMEM((L, w), BF16)],
            compiler_params=_cp(("arbitrary",), 40),
            name="diff_ctx",
        )(*args)
    pk, pv = past
    P = pk.shape[2]
    cos, sin = tables
    const = pl.BlockSpec((L, w), lambda i: (0, 0))
    cache = pl.BlockSpec((None, None, P, w), lambda i: (i, l, 0, 0))
    return pl.pallas_call(
        functools.partial(_diff_lat_kernel, L=L, P=P, lambda_init=lambda_init),
        grid=(B,),
        in_specs=in_specs + [const, const, cache, cache],
        out_specs=y_spec,
        out_shape=y_shape,
        scratch_shapes=[pltpu.VMEM((P + L, w), BF16), pltpu.VMEM((P + L, w), BF16)],
        compiler_params=_cp(("arbitrary",), 56),
        name="diff_lat",
    )(*args, cos, sin, pk, pv)


def _out_proj_kernel(ya_ref, yb_ref, yc_ref, yd_ref, w_ref, x_ref, mod_ref, ng_ref, wr_ref, br_ref,
                     x1_ref, h_ref, lg_ref, cat_ref):
    gw = GROUP_WIDTH
    cat_ref[:, 0 * gw:1 * gw] = ya_ref[...]
    cat_ref[:, 1 * gw:2 * gw] = yb_ref[...]
    cat_ref[:, 2 * gw:3 * gw] = yc_ref[...]
    cat_ref[:, 3 * gw:4 * gw] = yd_ref[...]
    m = jnp.dot(cat_ref[...], w_ref[...], preferred_element_type=F32)
    mn = m * lax.rsqrt(jnp.mean(m * m, axis=-1, keepdims=True) + EPS) * ng_ref[1:2, :]
    x1 = x_ref[...] + mod_ref[0, 2:3, :] * mn
    x1_ref[...] = x1
    hn = x1 * lax.rsqrt(jnp.mean(x1 * x1, axis=-1, keepdims=True) + EPS) * ng_ref[2:3, :]
    h = hn * (1.0 + mod_ref[0, 4:5, :]) + mod_ref[0, 3:4, :]
    h_ref[...] = h.astype(BF16)
    lg_ref[...] = jnp.dot(h, wr_ref[...], precision=HIGHEST, preferred_element_type=F32) + br_ref[...]


def _out_proj(ys, w, x, mod, ng, wr, br, tp, ls):
    t = x.shape[0]
    bm = OUT_BM
    yspec = pl.BlockSpec((bm, GROUP_WIDTH), lambda i: (i, 0))
    row = pl.BlockSpec((bm, D_MODEL), lambda i: (i, 0))
    return pl.pallas_call(
        _out_proj_kernel,
        grid=(t // bm,),
        in_specs=[yspec, yspec, yspec, yspec,
                  pl.BlockSpec((D_MODEL, D_MODEL), lambda i: (0, 0)),
                  row,
                  pl.BlockSpec((1, 6, D_MODEL), lambda i: (_mod_row(i, bm, tp, ls), 0, 0)),
                  pl.BlockSpec((4, D_MODEL), lambda i: (0, 0)),
                  pl.BlockSpec((D_MODEL, ROUTER_PAD), lambda i: (0, 0)),
                  pl.BlockSpec((1, ROUTER_PAD), lambda i: (0, 0))],
        out_specs=(row, row, pl.BlockSpec((bm, ROUTER_PAD), lambda i: (i, 0))),
        out_shape=(jax.ShapeDtypeStruct((t, D_MODEL), F32), jax.ShapeDtypeStruct((t, D_MODEL), BF16),
                   jax.ShapeDtypeStruct((t, ROUTER_PAD), F32)),
        scratch_shapes=[pltpu.VMEM((bm, D_MODEL), BF16)],
        compiler_params=_cp(("arbitrary",), 56),
        name="out_proj",
    )(*ys, w, x, mod, ng, wr, br)


def _new_expert(be_ref, i):
    return jnp.logical_or(i == 0, be_ref[jnp.maximum(i - 1, 0)] != be_ref[i])


def _gm1_kernel(be_ref, nu_ref, x_ref, wg_ref, wu_ref, bg_ref, bu_ref, o_ref, wbf_ref):
    i = pl.program_id(1)
    used = i < nu_ref[0]
    tn = GM1_TN

    @pl.when(jnp.logical_and(used, _new_expert(be_ref, i)))
    def _():
        wbf_ref[:, :tn] = wg_ref[...].astype(BF16)
        wbf_ref[:, tn:] = wu_ref[...].astype(BF16)

    @pl.when(used)
    def _():
        gu = jnp.dot(x_ref[...], wbf_ref[...], preferred_element_type=F32)
        gate = jnp.minimum(gu[:, :tn] + bg_ref[...], SWIGLU_LIMIT)
        up = jnp.clip(gu[:, tn:] + bu_ref[...], -SWIGLU_LIMIT, SWIGLU_LIMIT)
        glu = gate * jax.nn.sigmoid(gate * SWIGLU_ALPHA)
        o_ref[...] = ((up + 1.0) * glu).astype(BF16)


def _gm1(block_e, n_used, xs, w_gu, b_gu, l):
    rows = xs.shape[0]
    nb = rows // MOE_BM
    nj = D_FF // GM1_TN
    b_gu = b_gu.reshape(DEPTH, N_EXPERTS, 1, 2 * D_FF)
    last = lambda i, nu: jnp.minimum(i, nu[0] - 1)
    grid_spec = pltpu.PrefetchScalarGridSpec(
        num_scalar_prefetch=2,
        grid=(nj, nb),
        in_specs=[
            pl.BlockSpec((MOE_BM, D_MODEL), lambda j, i, be, nu: (last(i, nu), 0)),
            pl.BlockSpec((None, None, D_MODEL, GM1_TN), lambda j, i, be, nu: (l, be[i], 0, j)),
            pl.BlockSpec((None, None, D_MODEL, GM1_TN), lambda j, i, be, nu: (l, be[i], 0, nj + j)),
            pl.BlockSpec((None, None, 1, GM1_TN), lambda j, i, be, nu: (l, be[i], 0, j)),
            pl.BlockSpec((None, None, 1, GM1_TN), lambda j, i, be, nu: (l, be[i], 0, nj + j)),
        ],
        out_specs=pl.BlockSpec((MOE_BM, GM1_TN), lambda j, i, be, nu: (last(i, nu), j)),
        scratch_shapes=[pltpu.VMEM((D_MODEL, 2 * GM1_TN), BF16)],
    )
    return pl.pallas_call(
        _gm1_kernel,
        grid_spec=grid_spec,
        out_shape=jax.ShapeDtypeStruct((rows, D_FF), BF16),
        compiler_params=_cp(("arbitrary", "arbitrary"), 48),
        name="moe_gate_up",
    )(block_e, n_used, xs, w_gu, w_gu, b_gu, b_gu)


def _gm2_kernel(be_ref, nu_ref, a_ref, w_ref, b_ref, gate_ref, o_ref, wbf_ref):
    i = pl.program_id(1)
    used = i < nu_ref[0]

    @pl.when(jnp.logical_and(used, _new_expert(be_ref, i)))
    def _():
        wbf_ref[...] = w_ref[...].astype(BF16)

    @pl.when(used)
    def _():
        y = jnp.dot(a_ref[...], wbf_ref[...], preferred_element_type=F32) + b_ref[...]
        o_ref[...] = (y * gate_ref[...]).astype(BF16)


def _gm2(block_e, n_used, act, w_dn, b_dn, gates, l):
    rows = act.shape[0]
    nb = rows // MOE_BM
    nj = D_MODEL // GM2_TN
    b_dn = b_dn.reshape(DEPTH, N_EXPERTS, 1, D_MODEL)
    last = lambda i, nu: jnp.minimum(i, nu[0] - 1)
    grid_spec = pltpu.PrefetchScalarGridSpec(
        num_scalar_prefetch=2,
        grid=(nj, nb),
        in_specs=[
            pl.BlockSpec((MOE_BM, D_FF), lambda j, i, be, nu: (last(i, nu), 0)),
            pl.BlockSpec((None, None, D_FF, GM2_TN), lambda j, i, be, nu: (l, be[i], 0, j)),
            pl.BlockSpec((None, None, 1, GM2_TN), lambda j, i, be, nu: (l, be[i], 0, j)),
            pl.BlockSpec((MOE_BM, 1), lambda j, i, be, nu: (last(i, nu), 0)),
        ],
        out_specs=pl.BlockSpec((MOE_BM, GM2_TN), lambda j, i, be, nu: (last(i, nu), j)),
        scratch_shapes=[pltpu.VMEM((D_FF, GM2_TN), BF16)],
    )
    return pl.pallas_call(
        _gm2_kernel,
        grid_spec=grid_spec,
        out_shape=jax.ShapeDtypeStruct((rows, D_MODEL), BF16),
        compiler_params=_cp(("arbitrary", "arbitrary"), 48),
        name="moe_down",
    )(block_e, n_used, act, w_dn, b_dn, gates)


def _route(logits):
    t = logits.shape[0]
    a = t * TOP_K
    top_v, top_i = lax.top_k(logits, TOP_K)
    gates = jax.nn.softmax(top_v, axis=-1).reshape(-1)
    flat_e = top_i.reshape(-1).astype(jnp.int32)
    order = jnp.argsort(flat_e).astype(jnp.int32)
    sorted_e = flat_e[order]
    counts = jnp.zeros((N_EXPERTS,), jnp.int32).at[flat_e].add(1)
    padded = (counts + MOE_BM - 1) // MOE_BM * MOE_BM
    pad_end = jnp.cumsum(padded)
    pad_start = pad_end - padded
    seg_start = jnp.cumsum(counts) - counts
    dest = pad_start[sorted_e] + jnp.arange(a, dtype=jnp.int32) - seg_start[sorted_e]
    n_blocks = -(-a // MOE_BM) + N_EXPERTS
    slot = jnp.full((n_blocks * MOE_BM,), a, jnp.int32).at[dest].set(order)
    filled = slot < a
    slot_c = jnp.minimum(slot, a - 1)
    slot_tok = jnp.where(filled, slot_c // TOP_K, 0)
    slot_gate = jnp.where(filled, gates[slot_c], 0.0).astype(F32)
    block_e = jnp.minimum(
        jnp.searchsorted(pad_end, jnp.arange(n_blocks, dtype=jnp.int32) * MOE_BM, side='right'),
        N_EXPERTS - 1).astype(jnp.int32)
    n_used = (pad_end[-1] // MOE_BM).astype(jnp.int32).reshape(1)
    dest_of = jnp.zeros((a,), jnp.int32).at[order].set(dest)
    return slot_tok, slot_gate, block_e, n_used, dest_of


def _finish_kernel(y_ref, x_ref, mod_ref, g_ref, o_ref):
    d = D_MODEL
    f = y_ref[:, 0:d].astype(F32)
    for k in range(1, TOP_K):
        f = f + y_ref[:, k * d:(k + 1) * d].astype(F32)
    fn = f * lax.rsqrt(jnp.mean(f * f, axis=-1, keepdims=True) + EPS) * g_ref[...]
    o_ref[...] = x_ref[...] + mod_ref[0, 5:6, :] * fn


def _finish(y4, x1, mod, g, tp, ls):
    t = x1.shape[0]
    bm = min(FIN_BM, ls, tp)
    row = pl.BlockSpec((bm, D_MODEL), lambda i: (i, 0))
    return pl.pallas_call(
        _finish_kernel,
        grid=(t // bm,),
        in_specs=[pl.BlockSpec((bm, TOP_K * D_MODEL), lambda i: (i, 0)),
                  row,
                  pl.BlockSpec((1, 6, D_MODEL), lambda i: (_mod_row(i, bm, tp, ls), 0, 0)),
                  pl.BlockSpec((1, D_MODEL), lambda i: (0, 0))],
        out_specs=row,
        out_shape=jax.ShapeDtypeStruct((t, D_MODEL), F32),
        compiler_params=_cp(("arbitrary",), 48),
        name="moe_finish",
    )(y4, x1, mod, g)


def _moe(h, logits, x1, mod, g, w_gu, b_gu, w_dn, b_dn, l, tp, ls):
    t = h.shape[0]
    slot_tok, slot_gate, block_e, n_used, dest_of = _route(logits[:, :N_EXPERTS])
    xs = h[slot_tok]
    act = _gm1(block_e, n_used, xs, w_gu, b_gu, l)
    y = _gm2(block_e, n_used, act, w_dn, b_dn, slot_gate[:, None], l)
    y4 = y[dest_of].reshape(t, TOP_K * D_MODEL)
    return _finish(y4, x1, mod, g, tp, ls)


def kernel(x_prompt, x_sample, cache_attn_k, cache_attn_v, cache_diff_k, cache_diff_v, c, c_ctx, w_mod, b_mod, norm_g, w_in, w_out, hy_conv_w, hy_conv_b, hy_ffn_w1, hy_ffn_b1, hy_ffn_w2, hy_ffn_b2, hy_ffn_w3, hy_sin_freq, hy_skip, sgu_norm_g, sgu_norm_b, sgu_w, sgu_b, attn_q_g, attn_k_g, diff_lambda, diff_norm_g, w_router, b_router, w_gate_up, b_gate_up, w_down, b_down):
    bp, lp, d = x_prompt.shape
    bs, ls, _ = x_sample.shape
    tp = bp * lp
    past = cache_attn_k.shape[2]
    x = jnp.concatenate([x_prompt.reshape(tp, d), x_sample.reshape(bs * ls, d)], axis=0)

    cond = jnp.concatenate([c_ctx[None, :], c, jnp.zeros((COND_ROWS - 1 - bs, d), F32)], axis=0)
    mod = _modulation(cond, w_mod, b_mod).reshape(DEPTH, COND_ROWS, 6, d)

    w_in_b = w_in.astype(BF16)
    w_out_b = w_out.astype(BF16)
    wr = jnp.pad(w_router, ((0, 0), (0, 0), (0, ROUTER_PAD - N_EXPERTS)))
    br = jnp.pad(b_router, ((0, 0), (0, ROUTER_PAD - N_EXPERTS))).reshape(DEPTH, 1, ROUTER_PAD)

    pk = cache_attn_k.reshape(bs, DEPTH, past, -1)
    pv = cache_attn_v.reshape(bs, DEPTH, past, -1)
    pdk = cache_diff_k.reshape(bs, DEPTH, past, -1)
    pdv = cache_diff_v.reshape(bs, DEPTH, past, -1)

    groups = []
    for L in (lp, ls):
        cm, sm, alt8, fwd, inv = _dft_tables(L)
        z, window = _filter_consts(L)
        groups.append((L, cm, sm, alt8, fwd, inv, z, window))
    qw = ATTN_HEADS * ATTN_HEAD_DIM
    kw = ATTN_KV_HEADS * ATTN_HEAD_DIM
    attn_tables = _rope_tables(ls, ATTN_HEAD_DIM, qw) + _rope_tables(ls, ATTN_HEAD_DIM, kw)
    diff_tables = _rope_tables(ls, DIFF_QK_DIM, GROUP_WIDTH)

    new_k, new_v, new_dk, new_dv = [], [], [], []
    for l in range(DEPTH):
        ng = norm_g[l]
        proj = _in_proj(x, mod[l], ng[0:1], w_in_b[l], tp, ls)

        ys = []
        for gi, (B, row_off) in enumerate(((bp, 0), (bs, tp // ls))):
            L, cm, sm, alt8, fwd, inv, z, window = groups[gi]
            kre, kim = _hyena_filter(L, z, window, cm, sm, alt8, hy_ffn_w1[l], hy_ffn_b1[l], hy_ffn_w2[l],
                                     hy_ffn_b2[l], hy_ffn_w3[l], hy_sin_freq[l])
            ya = _hyena(proj, row_off, B, L, hy_conv_w[l], hy_conv_b[l], kre, kim, hy_skip[l], fwd, inv)
            yb = _sgu(proj, row_off, B, L, sgu_norm_g[l], sgu_norm_b[l], sgu_w[l], sgu_b[l])
            if gi == 0:
                yc, k_ctx = _gqa(proj, row_off, B, L, attn_q_g[l], attn_k_g[l], l)
                yd = _diff(proj, row_off, B, L, diff_lambda[l], diff_norm_g[l], l)
                new_k.append(k_ctx.reshape(bp, lp, ATTN_KV_HEADS, ATTN_HEAD_DIM))
                new_v.append(proj[:tp, 3328:3584].reshape(bp, lp, ATTN_KV_HEADS, ATTN_HEAD_DIM))
                new_dk.append(proj[:tp, 4096:4608].reshape(bp, lp, DIFF_HEADS, 2, DIFF_QK_DIM))
                new_dv.append(proj[:tp, 4608:5120].reshape(bp, lp, DIFF_HEADS, DIFF_V_DIM))
            else:
                yc = _gqa(proj, row_off, B, L, attn_q_g[l], attn_k_g[l], l, past=(pk, pv), tables=attn_tables)
                yd = _diff(proj, row_off, B, L, diff_lambda[l], diff_norm_g[l], l, past=(pdk, pdv),
                           tables=diff_tables)
            ys.append((ya, yb, yc, yd))
        ycat = [jnp.concatenate([ys[0][m], ys[1][m]], axis=0) for m in range(4)]

        x1, h2, logits = _out_proj(ycat, w_out_b[l], x, mod[l], ng, wr[l], br[l], tp, ls)
        x = _moe(h2, logits, x1, mod[l], ng[3:4], w_gate_up, b_gate_up, w_down, b_down, l, tp, ls)

    y_prompt = x[:tp].reshape(bp, lp, d)
    y_sample = x[tp:].reshape(bs, ls, d)
    return (y_prompt, y_sample, jnp.stack(new_k, axis=1), jnp.stack(new_v, axis=1),
            jnp.stack(new_dk, axis=1), jnp.stack(new_dv, axis=1))
```

```python
import functools
import math

import numpy as np
import jax
import jax.numpy as jnp
from jax import lax
from jax.experimental import pallas as pl
from jax.experimental.pallas import tpu as pltpu

F32 = jnp.float32
BF16 = jnp.bfloat16
HIGHEST = lax.Precision.HIGHEST

D_MODEL = 2048
DEPTH = 2
GRID_W = 64
GROUP_WIDTH = 512
EPS = 1e-6
IN_COLS = 5120
HYENA_BANDS = 16
HYENA_POS_DIM = 1 + 2 * HYENA_BANDS
HYENA_FFN = 64
HYENA_FAST_DECAY = 0.3
HYENA_SLOW_DECAY = 1.5
HYENA_TARGET = 0.01
POS_PAD = 128
CHUNK = 128
SGU_HEADS = 4
ATTN_HEADS = 4
ATTN_KV_HEADS = 2
ATTN_HEAD_DIM = 128
ROPE_THETA = 10000.0
DIFF_HEADS = 4
DIFF_QK_DIM = 64
DIFF_V_DIM = 128
N_EXPERTS = 32
TOP_K = 4
D_FF = 2048
SWIGLU_LIMIT = 7.0
SWIGLU_ALPHA = 1.702
ROUTER_PAD = 128

IN_BM, IN_BN = 1024, 1024
OUT_BM = 256
MOE_BM = 512
GM1_TN = 512
GM1_SUB = 256
GM2_TN = 1024
DISPATCH_BT = 512
FIN_BM = 256
COND_ROWS = 16
MOD_TN = 1024
HY_CC = 256
Q_CHUNK = 512
MIB = 2 ** 20


def _cp(sem, vmem_mib):
    return pltpu.CompilerParams(dimension_semantics=sem, vmem_limit_bytes=vmem_mib * MIB)


def _mod_kernel(c_ref, w_ref, b_ref, o_ref):
    c = c_ref[...]
    s = c * jax.nn.sigmoid(c)
    o_ref[...] = jnp.dot(s.astype(BF16), w_ref[...].astype(BF16), preferred_element_type=F32) + b_ref[...]


def _modulation(cond, w_mod, b_mod):
    n = w_mod.shape[-1]
    return pl.pallas_call(
        _mod_kernel,
        grid=(DEPTH, n // MOD_TN),
        in_specs=[
            pl.BlockSpec((COND_ROWS, D_MODEL), lambda l, j: (0, 0)),
            pl.BlockSpec((None, D_MODEL, MOD_TN), lambda l, j: (l, 0, j)),
            pl.BlockSpec((None, 1, MOD_TN), lambda l, j: (l, 0, j)),
        ],
        out_specs=pl.BlockSpec((None, COND_ROWS, MOD_TN), lambda l, j: (l, 0, j)),
        out_shape=jax.ShapeDtypeStruct((DEPTH, COND_ROWS, n), F32),
        compiler_params=_cp(("arbitrary", "arbitrary"), 40),
        name="modulation",
    )(cond, w_mod, b_mod.reshape(DEPTH, 1, n))


def _mod_row(i, bm, tp, ls):
    nbp = tp // bm
    return jnp.where(i < nbp, 0, 1 + (i - nbp) // (ls // bm))


def _in_proj_kernel(x_ref, mod_ref, g_ref, w_ref, o_ref, h_ref):
    @pl.when(pl.program_id(1) == 0)
    def _():
        x = x_ref[...]
        r = lax.rsqrt(jnp.mean(x * x, axis=-1, keepdims=True) + EPS)
        h = (x * r) * g_ref[...] * (1.0 + mod_ref[0, 1:2, :]) + mod_ref[0, 0:1, :]
        h_ref[...] = h.astype(BF16)

    o_ref[...] = jnp.dot(h_ref[...], w_ref[...], preferred_element_type=F32)


def _in_proj(x, mod, g, w, tp, ls):
    t = x.shape[0]
    bm = min(IN_BM, ls, tp)
    return pl.pallas_call(
        _in_proj_kernel,
        grid=(t // bm, IN_COLS // IN_BN),
        in_specs=[
            pl.BlockSpec((bm, D_MODEL), lambda i, j: (i, 0)),
            pl.BlockSpec((1, 6, D_MODEL), lambda i, j: (_mod_row(i, bm, tp, ls), 0, 0)),
            pl.BlockSpec((1, D_MODEL), lambda i, j: (0, 0)),
            pl.BlockSpec((D_MODEL, IN_BN), lambda i, j: (0, j)),
        ],
        out_specs=pl.BlockSpec((bm, IN_BN), lambda i, j: (i, j)),
        out_shape=jax.ShapeDtypeStruct((t, IN_COLS), F32),
        scratch_shapes=[pltpu.VMEM((bm, D_MODEL), BF16)],
        compiler_params=_cp(("arbitrary", "arbitrary"), 56),
        name="in_proj",
    )(x, mod, g, w)


def _dft_tables(L):
    k = np.arange(L, dtype=np.int64)
    ang = ((k[:, None] * k[None, :]) % (2 * L)).astype(np.float64) * (math.pi / L)
    cm = np.cos(ang)
    sm = np.sin(ang)
    alt = np.where(k % 2 == 0, 1.0, -1.0)
    sf = -sm
    sf[0] = alt
    fwd = np.concatenate([cm, sf], axis=0)
    wk = np.where(k == 0, 1.0, 2.0) / (2 * L)
    si = -sm / L
    si[:, 0] = alt / (2 * L)
    inv = np.concatenate([cm * wk[None, :], si], axis=1)
    alt8 = np.zeros((8, L))
    alt8[0] = alt
    return (jnp.asarray(cm, F32), jnp.asarray(sm, F32), jnp.asarray(alt8, F32),
            jnp.asarray(fwd.astype(BF16)), jnp.asarray(inv.astype(BF16)))


def _filter_consts(L):
    t = np.linspace(0.0, 1.0, L, dtype=np.float32)[:, None]
    bands = np.linspace(1e-4, HYENA_BANDS - 1, HYENA_BANDS, dtype=np.float32)[None, :]
    z = np.concatenate([t, np.cos(2 * math.pi * bands * t), np.sin(2 * math.pi * bands * t)], axis=-1)
    z = np.pad(z, ((0, 0), (0, POS_PAD - HYENA_POS_DIM)))
    max_decay = math.log(HYENA_TARGET) / HYENA_FAST_DECAY
    min_decay = math.log(HYENA_TARGET) / HYENA_SLOW_DECAY
    deltas = np.abs(np.linspace(min_decay, max_decay, GROUP_WIDTH, dtype=np.float32))
    window = np.exp(-t * deltas[None, :])
    return jnp.asarray(z, F32), jnp.asarray(window, F32)


def _filter_kernel(z_ref, w1_ref, b1_ref, w2_ref, b2_ref, w3_ref, fr_ref, win_ref, cm_ref, sm_ref, alt_ref,
                   kre_ref, kim_ref):
    c = GROUP_WIDTH
    h = jnp.dot(z_ref[...], w1_ref[...], precision=HIGHEST, preferred_element_type=F32) + b1_ref[...]
    h = jnp.sin(fr_ref[0:1, :] * h)
    h = jnp.dot(h, w2_ref[...], precision=HIGHEST, preferred_element_type=F32) + b2_ref[...]
    h = jnp.sin(fr_ref[1:2, :] * h)
    hfb = jnp.dot(h, w3_ref[...], precision=HIGHEST, preferred_element_type=F32)
    win = win_ref[...]
    hf = hfb[:, :c] * win
    hb = hfb[:, c:] * win
    row = lax.broadcasted_iota(jnp.int32, hf.shape, 0)
    g = hf + jnp.where(row == 0, 0.0, hb)
    kre = jnp.dot(cm_ref[...], g, precision=HIGHEST, preferred_element_type=F32)
    kim = jnp.dot(sm_ref[...], hb - hf, precision=HIGHEST, preferred_element_type=F32)
    kn = jnp.dot(alt_ref[...], g, precision=HIGHEST, preferred_element_type=F32)[0:1, :]
    kre_ref[...] = kre
    kim_ref[...] = jnp.where(row == 0, kn, kim)


def _hyena_filter(L, z, window, cm, sm, alt8, w1, b1, w2, b2, w3, freq):
    w1p = jnp.pad(w1, ((0, POS_PAD - HYENA_POS_DIM), (0, 0)))
    return pl.pallas_call(
        _filter_kernel,
        out_shape=(jax.ShapeDtypeStruct((L, GROUP_WIDTH), F32), jax.ShapeDtypeStruct((L, GROUP_WIDTH), F32)),
        compiler_params=pltpu.CompilerParams(vmem_limit_bytes=56 * MIB),
        name="hyena_filter",
    )(z, w1p, b1.reshape(1, -1), w2, b2.reshape(1, -1), w3, freq, window, cm, sm, alt8)


def _short_conv(a, w_ref, b_ref, row, L):
    prev = jnp.where(row == 0, 0.0, pltpu.roll(a, 1, 0))
    nxt = jnp.where(row == L - 1, 0.0, pltpu.roll(a, L - 1, 0))
    return b_ref[...] + prev * w_ref[0:1, :] + a * w_ref[1:2, :] + nxt * w_ref[2:3, :]


def _hyena_kernel(a0_ref, a1_ref, a2_ref, w0_ref, w1_ref, w2_ref, b0_ref, b1_ref, b2_ref, kre_ref, kim_ref,
                  skip_ref, fwd_ref, inv_ref, o_ref, *, L):
    row = lax.broadcasted_iota(jnp.int32, (L, HY_CC), 0)
    x1 = _short_conv(a1_ref[...], w1_ref, b1_ref, row, L)
    hv = _short_conv(a2_ref[...], w2_ref, b2_ref, row, L)
    u = hv * x1
    spec = jnp.dot(fwd_ref[...], u.astype(BF16), preferred_element_type=F32)
    ure, uim = spec[:L], spec[L:]
    kre, kim = kre_ref[...], kim_ref[...]
    first = row == 0
    yre = ure * kre - jnp.where(first, 0.0, uim * kim)
    yim = jnp.where(first, uim * kim, ure * kim + uim * kre)
    yspec = jnp.concatenate([yre, yim], axis=0).astype(BF16)
    y = jnp.dot(inv_ref[...], yspec, preferred_element_type=F32)
    x0 = _short_conv(a0_ref[...], w0_ref, b0_ref, row, L)
    o_ref[...] = (x0 * (y + u * skip_ref[...])).astype(BF16)


def _hyena(proj, row_off, B, L, conv_w, conv_b, kre, kim, skip, fwd, inv):
    ncc = GROUP_WIDTH // HY_CC

    def a_spec(part):
        return pl.BlockSpec((L, HY_CC), lambda b, cc: (row_off + b, part * ncc + cc))

    def w_spec(part, rows):
        return pl.BlockSpec((rows, HY_CC), lambda b, cc: (0, part * ncc + cc))

    chan = pl.BlockSpec((L, HY_CC), lambda b, cc: (0, cc))
    conv_b = conv_b.reshape(1, -1)
    return pl.pallas_call(
        functools.partial(_hyena_kernel, L=L),
        grid=(B, ncc),
        in_specs=[a_spec(0), a_spec(1), a_spec(2), w_spec(0, 3), w_spec(1, 3), w_spec(2, 3),
                  w_spec(0, 1), w_spec(1, 1), w_spec(2, 1), chan, chan,
                  pl.BlockSpec((1, HY_CC), lambda b, cc: (0, cc)),
                  pl.BlockSpec((2 * L, L), lambda b, cc: (0, 0)),
                  pl.BlockSpec((L, 2 * L), lambda b, cc: (0, 0))],
        out_specs=pl.BlockSpec((L, HY_CC), lambda b, cc: (b, cc)),
        out_shape=jax.ShapeDtypeStruct((B * L, GROUP_WIDTH), BF16),
        compiler_params=_cp(("arbitrary", "arbitrary"), 56),
        name="hyena_conv",
    )(proj, proj, proj, conv_w, conv_w, conv_w, conv_b, conv_b, conv_b, kre, kim, skip.reshape(1, -1), fwd, inv)


def _sgu_kernel(u_ref, v_ref, g_ref, b_ref, w_ref, bs_ref, o_ref, *, L):
    v = v_ref[...]
    mu = jnp.mean(v, axis=-1, keepdims=True)
    xc = v - mu
    vn = xc * lax.rsqrt(jnp.mean(xc * xc, axis=-1, keepdims=True) + EPS) * g_ref[...] + b_ref[...]
    vnb = vn.astype(BF16)
    hc = GROUP_WIDTH // SGU_HEADS
    for h in range(SGU_HEADS):
        wh = w_ref[h].astype(BF16)
        bias = bs_ref[h]
        cols = slice(h * hc, (h + 1) * hc)
        for n in range(L // CHUNK):
            rows = slice(n * CHUNK, (n + 1) * CHUNK)
            mixed = jnp.dot(wh, vnb[rows, cols], preferred_element_type=F32) + bias
            o_ref[rows, cols] = (u_ref[rows, cols] * mixed).astype(BF16)


def _sgu(proj, row_off, B, L, g, b, w_s, b_s):
    hc = GROUP_WIDTH // SGU_HEADS
    bias = jnp.broadcast_to(b_s[:, :, None], (SGU_HEADS, CHUNK, hc))
    return pl.pallas_call(
        functools.partial(_sgu_kernel, L=L),
        grid=(B,),
        in_specs=[pl.BlockSpec((L, GROUP_WIDTH), lambda i: (row_off + i, 3)),
                  pl.BlockSpec((L, GROUP_WIDTH), lambda i: (row_off + i, 4)),
                  pl.BlockSpec((1, GROUP_WIDTH), lambda i: (0, 0)),
                  pl.BlockSpec((1, GROUP_WIDTH), lambda i: (0, 0)),
                  pl.BlockSpec((SGU_HEADS, CHUNK, CHUNK), lambda i: (0, 0, 0)),
                  pl.BlockSpec((SGU_HEADS, CHUNK, hc), lambda i: (0, 0, 0))],
        out_specs=pl.BlockSpec((L, GROUP_WIDTH), lambda i: (i, 0)),
        out_shape=jax.ShapeDtypeStruct((B * L, GROUP_WIDTH), BF16),
        compiler_params=_cp(("arbitrary",), 40),
        name="sgu",
    )(proj, proj, g.reshape(1, -1), b.reshape(1, -1), w_s, bias)


def _rope_tables(L, dh, width):
    quarter = dh // 4
    pos = np.arange(L)
    rowp = (pos // GRID_W).astype(np.float32)
    colp = (pos % GRID_W).astype(np.float32)
    inv_freq = (ROPE_THETA ** (-np.arange(quarter, dtype=np.float32) / quarter)).astype(np.float32)
    ar = (rowp[:, None] * inv_freq[None, :]).astype(np.float64)
    ac = (colp[:, None] * inv_freq[None, :]).astype(np.float64)
    cos = np.concatenate([np.cos(ar), np.cos(ar), np.cos(ac), np.cos(ac)], axis=-1)
    sin = np.concatenate([-np.sin(ar), np.sin(ar), -np.sin(ac), np.sin(ac)], axis=-1)
    reps = width // dh
    return jnp.asarray(np.tile(cos, (1, reps)), F32), jnp.asarray(np.tile(sin, (1, reps)), F32)


def _rope(x, cos, sin, quarter):
    w = x.shape[-1]
    lane = lax.broadcasted_iota(jnp.int32, x.shape, 1)
    partner = jnp.where((lane & quarter) != 0, pltpu.roll(x, quarter, 1), pltpu.roll(x, w - quarter, 1))
    return x * cos + partner * sin


def _head_rms(x, g, heads, dh):
    outs = []
    for h in range(heads):
        xh = x[:, h * dh:(h + 1) * dh]
        outs.append(xh * lax.rsqrt(jnp.mean(xh * xh, axis=-1, keepdims=True) + EPS) * g)
    return jnp.concatenate(outs, axis=-1)


def _softmax_pv(q, k, v):
    s = lax.dot_general(q, k, (((1,), (1,)), ((), ())), preferred_element_type=F32)
    e = jnp.exp(s - jnp.max(s, axis=-1, keepdims=True))
    inv_l = 1.0 / jnp.sum(e, axis=-1, keepdims=True)
    return jnp.dot(e.astype(BF16), v, preferred_element_type=F32), inv_l


def _gqa_body(q, k_ref, v_ref, o_ref, L):
    dh = ATTN_HEAD_DIM
    group = ATTN_HEADS // ATTN_KV_HEADS
    qs = (q * dh ** -0.5).astype(BF16)
    qc = min(Q_CHUNK, L)
    for h in range(ATTN_HEADS):
        kv = h // group
        kh = k_ref[:, kv * dh:(kv + 1) * dh]
        vh = v_ref[:, kv * dh:(kv + 1) * dh]
        for c in range(L // qc):
            rows = slice(c * qc, (c + 1) * qc)
            pv, inv_l = _softmax_pv(qs[rows, h * dh:(h + 1) * dh], kh, vh)
            o_ref[rows, h * dh:(h + 1) * dh] = (pv * inv_l).astype(BF16)


def _gqa_ctx_kernel(q_ref, k_ref, v_ref, gq_ref, gk_ref, o_ref, ko_ref, kb_ref, vb_ref, *, L):
    q = _head_rms(q_ref[...], gq_ref[...], ATTN_HEADS, ATTN_HEAD_DIM)
    k = _head_rms(k_ref[...], gk_ref[...], ATTN_KV_HEADS, ATTN_HEAD_DIM)
    ko_ref[...] = k
    kb_ref[...] = k.astype(BF16)
    vb_ref[...] = v_ref[...].astype(BF16)
    _gqa_body(q, kb_ref, vb_ref, o_ref, L)


def _gqa_lat_kernel(q_ref, k_ref, v_ref, gq_ref, gk_ref, cq_ref, sq_ref, ck_ref, sk_ref, pk_ref, pv_ref, o_ref,
                    kb_ref, vb_ref, *, L, P):
    quarter = ATTN_HEAD_DIM // 4
    q = _rope(_head_rms(q_ref[...], gq_ref[...], ATTN_HEADS, ATTN_HEAD_DIM), cq_ref[...], sq_ref[...], quarter)
    k = _rope(_head_rms(k_ref[...], gk_ref[...], ATTN_KV_HEADS, ATTN_HEAD_DIM), ck_ref[...], sk_ref[...], quarter)
    kb_ref[0:P, :] = pk_ref[...].astype(BF16)
    kb_ref[P:P + L, :] = k.astype(BF16)
    vb_ref[0:P, :] = pv_ref[...].astype(BF16)
    vb_ref[P:P + L, :] = v_ref[...].astype(BF16)
    _gqa_body(q, kb_ref, vb_ref, o_ref, L)


def _gqa(proj, row_off, B, L, gq, gk, l, past=None, tables=None):
    qw = ATTN_HEADS * ATTN_HEAD_DIM
    kw = ATTN_KV_HEADS * ATTN_HEAD_DIM
    qoff = 5
    koff = (2560 + qw) // kw
    in_specs = [pl.BlockSpec((L, qw), lambda i: (row_off + i, qoff)),
                pl.BlockSpec((L, kw), lambda i: (row_off + i, koff)),
                pl.BlockSpec((L, kw), lambda i: (row_off + i, koff + 1)),
                pl.BlockSpec((1, ATTN_HEAD_DIM), lambda i: (0, 0)),
                pl.BlockSpec((1, ATTN_HEAD_DIM), lambda i: (0, 0))]
    args = [proj, proj, proj, gq.reshape(1, -1), gk.reshape(1, -1)]
    y_spec = pl.BlockSpec((L, qw), lambda i: (i, 0))
    y_shape = jax.ShapeDtypeStruct((B * L, qw), BF16)
    if past is None:
        return pl.pallas_call(
            functools.partial(_gqa_ctx_kernel, L=L),
            grid=(B,),
            in_specs=in_specs,
            out_specs=(y_spec, pl.BlockSpec((L, kw), lambda i: (i, 0))),
            out_shape=(y_shape, jax.ShapeDtypeStruct((B * L, kw), F32)),
            scratch_shapes=[pltpu.VMEM((L, kw), BF16), pltpu.VMEM((L, kw), BF16)],
            compiler_params=_cp(("arbitrary",), 40),
            name="gqa_ctx",
        )(*args)
    pk, pv = past
    P = pk.shape[2]
    cq, sq, ck, sk = tables
    const = lambda w: pl.BlockSpec((L, w), lambda i: (0, 0))
    cache = pl.BlockSpec((None, None, P, kw), lambda i: (i, l, 0, 0))
    return pl.pallas_call(
        functools.partial(_gqa_lat_kernel, L=L, P=P),
        grid=(B,),
        in_specs=in_specs + [const(qw), const(qw), const(kw), const(kw), cache, cache],
        out_specs=y_spec,
        out_shape=y_shape,
        scratch_shapes=[pltpu.VMEM((P + L, kw), BF16), pltpu.VMEM((P + L, kw), BF16)],
        compiler_params=_cp(("arbitrary",), 56),
        name="gqa_lat",
    )(*args, cq, sq, ck, sk, pk, pv)


def _diff_body(dq, k_ref, v_ref, dl_ref, g_ref, o_ref, L, lambda_init):
    dv = DIFF_V_DIM
    dl = dl_ref[...]
    lam = (jnp.exp(jnp.sum(dl[0:1] * dl[1:2], axis=-1, keepdims=True))
           - jnp.exp(jnp.sum(dl[2:3] * dl[3:4], axis=-1, keepdims=True)) + lambda_init)
    qs = dq * DIFF_QK_DIM ** -0.5
    first_map = lax.broadcasted_iota(jnp.int32, (1, dv), 1) < DIFF_QK_DIM
    qc = min(Q_CHUNK, L)
    for h in range(DIFF_HEADS):
        cols = slice(h * dv, (h + 1) * dv)
        kh = k_ref[:, cols]
        vh = v_ref[:, cols]
        for c in range(L // qc):
            rows = slice(c * qc, (c + 1) * qc)
            qh = qs[rows, cols]
            pv0, inv0 = _softmax_pv(jnp.where(first_map, qh, 0.0).astype(BF16), kh, vh)
            pv1, inv1 = _softmax_pv(jnp.where(first_map, 0.0, qh).astype(BF16), kh, vh)
            o = pv0 * inv0 - pv1 * (lam * inv1)
            r = lax.rsqrt(jnp.mean(o * o, axis=-1, keepdims=True) + EPS)
            o_ref[rows, cols] = ((o * r) * g_ref[...] * (1.0 - lambda_init)).astype(BF16)


def _diff_ctx_kernel(q_ref, k_ref, v_ref, dl_ref, g_ref, o_ref, kb_ref, vb_ref, *, L, lambda_init):
    kb_ref[...] = k_ref[...].astype(BF16)
    vb_ref[...] = v_ref[...].astype(BF16)
    _diff_body(q_ref[...], kb_ref, vb_ref, dl_ref, g_ref, o_ref, L, lambda_init)


def _diff_lat_kernel(q_ref, k_ref, v_ref, dl_ref, g_ref, cos_ref, sin_ref, pk_ref, pv_ref, o_ref, kb_ref, vb_ref,
                     *, L, P, lambda_init):
    quarter = DIFF_QK_DIM // 4
    cos, sin = cos_ref[...], sin_ref[...]
    q = _rope(q_ref[...], cos, sin, quarter)
    k = _rope(k_ref[...], cos, sin, quarter)
    kb_ref[0:P, :] = pk_ref[...].astype(BF16)
    kb_ref[P:P + L, :] = k.astype(BF16)
    vb_ref[0:P, :] = pv_ref[...].astype(BF16)
    vb_ref[P:P + L, :] = v_ref[...].astype(BF16)
    _diff_body(q, kb_ref, vb_ref, dl_ref, g_ref, o_ref, L, lambda_init)


def _diff(proj, row_off, B, L, dl, g, l, past=None, tables=None):
    w = GROUP_WIDTH
    lambda_init = 0.8 - 0.6 * math.exp(-0.3 * l)
    in_specs = [pl.BlockSpec((L, w), lambda i: (row_off + i, 7)),
                pl.BlockSpec((L, w), lambda i: (row_off + i, 8)),
                pl.BlockSpec((L, w), lambda i: (row_off + i, 9)),
                pl.BlockSpec((4, DIFF_QK_DIM), lambda i: (0, 0)),
                pl.BlockSpec((1, DIFF_V_DIM), lambda i: (0, 0))]
    args = [proj, proj, proj, dl, g.reshape(1, -1)]
    y_spec = pl.BlockSpec((L, w), lambda i: (i, 0))
    y_shape = jax.ShapeDtypeStruct((B * L, w), BF16)
    if past is None:
        return pl.pallas_call(
            functools.partial(_diff_ctx_kernel, L=L, lambda_init=lambda_init),
            grid=(B,),
            in_specs=in_specs,
            out_specs=y_spec,
            out_shape=y_shape,
            scratch_shapes=[pltpu.VMEM((L, w), BF16), pltpu.VMEM((L, w), BF16)],
            compiler_params=_cp(("arbitrary",), 40),
            name="diff_ctx",
        )(*args)
    pk, pv = past
    P = pk.shape[2]
    cos, sin = tables
    const = pl.BlockSpec((L, w), lambda i: (0, 0))
    cache = pl.BlockSpec((None, None, P, w), lambda i: (i, l, 0, 0))
    return pl.pallas_call(
        functools.partial(_diff_lat_kernel, L=L, P=P, lambda_init=lambda_init),
        grid=(B,),
        in_specs=in_specs + [const, const, cache, cache],
        out_specs=y_spec,
        out_shape=y_shape,
        scratch_shapes=[pltpu.VMEM((P + L, w), BF16), pltpu.VMEM((P + L, w), BF16)],
        compiler_params=_cp(("arbitrary",), 56),
        name="diff_lat",
    )(*args, cos, sin, pk, pv)


def _pack_halves(lo, hi):
    lo_bits = pltpu.bitcast(lo.astype(BF16).astype(F32), jnp.uint32) >> 16
    hi_bits = pltpu.bitcast(hi.astype(BF16).astype(F32), jnp.uint32) & jnp.uint32(0xFFFF0000)
    return lo_bits | hi_bits


def _unpack_halves(words):
    lo = pltpu.bitcast(words << 16, F32)
    hi = pltpu.bitcast(words & jnp.uint32(0xFFFF0000), F32)
    return lo, hi


def _route_rows(logits, prefix_base, tri_ref):
    lane = lax.broadcasted_iota(jnp.int32, logits.shape, 1)
    neg = jnp.float32(-jnp.inf)
    work = jnp.where(lane < N_EXPERTS, logits, neg)
    sels, vals, idxs = [], [], []
    for _ in range(TOP_K):
        mk = jnp.max(work, axis=-1, keepdims=True)
        ik = jnp.min(jnp.where(work == mk, lane, ROUTER_PAD), axis=-1, keepdims=True)
        sel = lane == ik
        work = jnp.where(sel, neg, work)
        sels.append(sel)
        vals.append(mk)
        idxs.append(ik)
    es = [jnp.exp(v - vals[0]) for v in vals]
    inv = 1.0 / (es[0] + es[1] + es[2] + es[3])
    member = jnp.zeros(logits.shape, F32)
    for sel in sels:
        member = member + jnp.where(sel, 1.0, 0.0)
    prefix = jnp.dot(tri_ref[...], member.astype(BF16), preferred_element_type=F32) + prefix_base
    ri = jnp.zeros(logits.shape, jnp.int32)
    rg = jnp.zeros(logits.shape, F32)
    for k in range(TOP_K):
        rank = jnp.sum(jnp.where(sels[k], prefix, 0.0), axis=-1, keepdims=True).astype(jnp.int32)
        ri = jnp.where(lane == k, idxs[k], ri)
        ri = jnp.where(lane == TOP_K + k, rank, ri)
        rg = jnp.where(lane == k, es[k] * inv, rg)
    return ri, rg, jnp.sum(member, axis=0, keepdims=True)


def _out_proj_kernel(ya_ref, yb_ref, yc_ref, yd_ref, w_ref, x_ref, mod_ref, ng_ref, wr_ref, br_ref, tri_ref,
                     x1_ref, hu_ref, ri_ref, rg_ref, cnt_ref, cat_ref, carry_ref):
    @pl.when(pl.program_id(0) == 0)
    def _():
        carry_ref[...] = jnp.zeros_like(carry_ref)

    gw = GROUP_WIDTH
    cat_ref[:, 0 * gw:1 * gw] = ya_ref[...]
    cat_ref[:, 1 * gw:2 * gw] = yb_ref[...]
    cat_ref[:, 2 * gw:3 * gw] = yc_ref[...]
    cat_ref[:, 3 * gw:4 * gw] = yd_ref[...]
    m = jnp.dot(cat_ref[...], w_ref[...], preferred_element_type=F32)
    mn = m * lax.rsqrt(jnp.mean(m * m, axis=-1, keepdims=True) + EPS) * ng_ref[1:2, :]
    x1 = x_ref[...] + mod_ref[0, 2:3, :] * mn
    x1_ref[...] = x1
    hn = x1 * lax.rsqrt(jnp.mean(x1 * x1, axis=-1, keepdims=True) + EPS) * ng_ref[2:3, :]
    h = hn * (1.0 + mod_ref[0, 4:5, :]) + mod_ref[0, 3:4, :]
    half = D_MODEL // 2
    hu_ref[...] = _pack_halves(h[:, :half], h[:, half:])
    hh = h.astype(BF16)
    hl = (h - hh.astype(F32)).astype(BF16)
    both = jnp.dot(hh, wr_ref[...], preferred_element_type=F32)
    low = jnp.dot(hl, wr_ref[:, :ROUTER_PAD], preferred_element_type=F32)
    logits = both[:, :ROUTER_PAD] + both[:, ROUTER_PAD:] + low + br_ref[...]
    ri, rg, counts = _route_rows(logits, carry_ref[0:1, :], tri_ref)
    ri_ref[...] = ri
    rg_ref[...] = rg
    carry_ref[...] = carry_ref[...] + counts
    cnt_ref[...] = carry_ref[...]


def _out_proj(ys, w, x, mod, ng, wr, br, tp, ls):
    t = x.shape[0]
    bm = OUT_BM
    yspec = pl.BlockSpec((bm, GROUP_WIDTH), lambda i: (i, 0))
    row = pl.BlockSpec((bm, D_MODEL), lambda i: (i, 0))
    info = pl.BlockSpec((bm, ROUTER_PAD), lambda i: (i, 0))
    tri = jnp.asarray(np.tril(np.ones((bm, bm), np.float32), -1).astype(BF16))
    return pl.pallas_call(
        _out_proj_kernel,
        grid=(t // bm,),
        in_specs=[yspec, yspec, yspec, yspec,
                  pl.BlockSpec((D_MODEL, D_MODEL), lambda i: (0, 0)),
                  row,
                  pl.BlockSpec((1, 6, D_MODEL), lambda i: (_mod_row(i, bm, tp, ls), 0, 0)),
                  pl.BlockSpec((4, D_MODEL), lambda i: (0, 0)),
                  pl.BlockSpec((D_MODEL, 2 * ROUTER_PAD), lambda i: (0, 0)),
                  pl.BlockSpec((1, ROUTER_PAD), lambda i: (0, 0)),
                  pl.BlockSpec((bm, bm), lambda i: (0, 0))],
        out_specs=(row, pl.BlockSpec((bm, D_MODEL // 2), lambda i: (i, 0)), info, info,
                   pl.BlockSpec((8, ROUTER_PAD), lambda i: (0, 0))),
        out_shape=(jax.ShapeDtypeStruct((t, D_MODEL), F32), jax.ShapeDtypeStruct((t, D_MODEL // 2), jnp.uint32),
                   jax.ShapeDtypeStruct((t, ROUTER_PAD), jnp.int32), jax.ShapeDtypeStruct((t, ROUTER_PAD), F32),
                   jax.ShapeDtypeStruct((8, ROUTER_PAD), F32)),
        scratch_shapes=[pltpu.VMEM((bm, D_MODEL), BF16), pltpu.VMEM((8, ROUTER_PAD), F32)],
        compiler_params=_cp(("arbitrary",), 56),
        name="out_proj",
    )(*ys, w, x, mod, ng, wr, br, tri)


def _new_expert(be_ref, i):
    return jnp.logical_or(i == 0, be_ref[jnp.maximum(i - 1, 0)] != be_ref[i])


def _gm1_kernel(be_ref, nu_ref, x_ref, wg_ref, wu_ref, bg_ref, bu_ref, o_ref, wbf_ref, xb_ref):
    i = pl.program_id(1)
    used = i < nu_ref[0]
    tn = GM1_TN
    half = D_MODEL // 2

    @pl.when(jnp.logical_and(used, _new_expert(be_ref, i)))
    def _():
        wbf_ref[:, :tn] = wg_ref[...].astype(BF16)
        wbf_ref[:, tn:] = wu_ref[...].astype(BF16)

    @pl.when(used)
    def _():
        lo, hi = _unpack_halves(x_ref[...])
        xb_ref[:, :half] = lo.astype(BF16)
        xb_ref[:, half:] = hi.astype(BF16)
        for c in range(tn // GM1_SUB):
            cols = slice(c * GM1_SUB, (c + 1) * GM1_SUB)
            ucols = slice(tn + c * GM1_SUB, tn + (c + 1) * GM1_SUB)
            g = jnp.dot(xb_ref[...], wbf_ref[:, cols], preferred_element_type=F32)
            u = jnp.dot(xb_ref[...], wbf_ref[:, ucols], preferred_element_type=F32)
            gate = jnp.minimum(g + bg_ref[:, cols], SWIGLU_LIMIT)
            up = jnp.clip(u + bu_ref[:, cols], -SWIGLU_LIMIT, SWIGLU_LIMIT)
            glu = gate * jax.nn.sigmoid(gate * SWIGLU_ALPHA)
            o_ref[:, cols] = ((up + 1.0) * glu).astype(BF16)

    @pl.when(jnp.logical_not(used))
    def _():
        o_ref[...] = jnp.zeros_like(o_ref)


def _gm1(block_e, n_used, xs, w_gu, b_gu, l):
    rows = xs.shape[0]
    nb = rows // MOE_BM
    nj = D_FF // GM1_TN
    b_gu = b_gu.reshape(DEPTH, N_EXPERTS, 1, 2 * D_FF)
    last = lambda i, nu: jnp.minimum(i, nu[0] - 1)
    grid_spec = pltpu.PrefetchScalarGridSpec(
        num_scalar_prefetch=2,
        grid=(nj, nb),
        in_specs=[
            pl.BlockSpec((MOE_BM, D_MODEL // 2), lambda j, i, be, nu: (last(i, nu), 0)),
            pl.BlockSpec((None, None, D_MODEL, GM1_TN), lambda j, i, be, nu: (l, be[i], 0, j)),
            pl.BlockSpec((None, None, D_MODEL, GM1_TN), lambda j, i, be, nu: (l, be[i], 0, nj + j)),
            pl.BlockSpec((None, None, 1, GM1_TN), lambda j, i, be, nu: (l, be[i], 0, j)),
            pl.BlockSpec((None, None, 1, GM1_TN), lambda j, i, be, nu: (l, be[i], 0, nj + j)),
        ],
        out_specs=pl.BlockSpec((MOE_BM, GM1_TN), lambda j, i, be, nu: (i, j)),
        scratch_shapes=[pltpu.VMEM((D_MODEL, 2 * GM1_TN), BF16), pltpu.VMEM((MOE_BM, D_MODEL), BF16)],
    )
    return pl.pallas_call(
        _gm1_kernel,
        grid_spec=grid_spec,
        out_shape=jax.ShapeDtypeStruct((rows, D_FF), BF16),
        compiler_params=_cp(("arbitrary", "arbitrary"), 48),
        name="moe_gate_up",
    )(block_e, n_used, xs, w_gu, w_gu, b_gu, b_gu)


def _gm2_kernel(be_ref, nu_ref, a_ref, w_ref, b_ref, o_ref, wbf_ref):
    i = pl.program_id(1)
    used = i < nu_ref[0]
    half = GM2_TN // 2

    @pl.when(jnp.logical_and(used, _new_expert(be_ref, i)))
    def _():
        wbf_ref[...] = w_ref[...].astype(BF16)

    @pl.when(used)
    def _():
        lo = jnp.dot(a_ref[...], wbf_ref[:, :half], preferred_element_type=F32) + b_ref[:, :half]
        hi = jnp.dot(a_ref[...], wbf_ref[:, half:], preferred_element_type=F32) + b_ref[:, half:]
        o_ref[...] = _pack_halves(lo, hi)

    @pl.when(jnp.logical_not(used))
    def _():
        o_ref[...] = jnp.zeros_like(o_ref)


def _gm2(block_e, n_used, act, w_dn, b_dn, l):
    rows = act.shape[0]
    nb = rows // MOE_BM
    nj = D_MODEL // GM2_TN
    b_dn = b_dn.reshape(DEPTH, N_EXPERTS, 1, D_MODEL)
    last = lambda i, nu: jnp.minimum(i, nu[0] - 1)
    grid_spec = pltpu.PrefetchScalarGridSpec(
        num_scalar_prefetch=2,
        grid=(nj, nb),
        in_specs=[
            pl.BlockSpec((MOE_BM, D_FF), lambda j, i, be, nu: (last(i, nu), 0)),
            pl.BlockSpec((None, None, D_FF, GM2_TN), lambda j, i, be, nu: (l, be[i], 0, j)),
            pl.BlockSpec((None, None, 1, GM2_TN), lambda j, i, be, nu: (l, be[i], 0, j)),
        ],
        out_specs=pl.BlockSpec((MOE_BM, GM2_TN // 2), lambda j, i, be, nu: (i, j)),
        scratch_shapes=[pltpu.VMEM((D_FF, GM2_TN), BF16)],
    )
    return pl.pallas_call(
        _gm2_kernel,
        grid_spec=grid_spec,
        out_shape=jax.ShapeDtypeStruct((rows, D_MODEL // 2), jnp.uint32),
        compiler_params=_cp(("arbitrary", "arbitrary"), 48),
        name="moe_down",
    )(block_e, n_used, act, w_dn, b_dn)


def _block_layout(counts, n_blocks):
    padded = (counts + MOE_BM - 1) // MOE_BM * MOE_BM
    pad_end = jnp.cumsum(padded)
    pad_start = (pad_end - padded).astype(jnp.int32)
    block_e = jnp.minimum(
        jnp.searchsorted(pad_end, jnp.arange(n_blocks, dtype=jnp.int32) * MOE_BM, side='right'),
        N_EXPERTS - 1).astype(jnp.int32)
    n_used = (pad_end[-1] // MOE_BM).astype(jnp.int32).reshape(1)
    return pad_start, block_e, n_used


def _slot_row(e_ref, p_ref, ps_ref, a):
    return ps_ref[e_ref[a]] + p_ref[a]


def _dispatch_kernel(e_ref, p_ref, ps_ref, h_ref, xs_in_ref, xs_ref, sem, *, bt):
    del xs_in_ref
    base = pl.program_id(0) * bt

    def row_copy(tok, k):
        dst = _slot_row(e_ref, p_ref, ps_ref, tok * TOP_K + k)
        return pltpu.make_async_copy(h_ref.at[pl.ds(tok, 1)], xs_ref.at[pl.ds(dst, 1)], sem)

    def body(t, carry):
        for k in range(TOP_K):
            row_copy(base + t, k).start()
        return carry

    lax.fori_loop(0, bt, body, 0)
    for _ in range(TOP_K):
        pltpu.make_async_copy(h_ref.at[pl.ds(0, bt)], xs_ref.at[pl.ds(0, bt)], sem).wait()


def _dispatch(e_flat, p_flat, pad_start, hu, rows):
    t, w = hu.shape
    bt = DISPATCH_BT
    grid_spec = pltpu.PrefetchScalarGridSpec(
        num_scalar_prefetch=3,
        grid=(t // bt,),
        in_specs=[pl.BlockSpec(memory_space=pl.ANY), pl.BlockSpec(memory_space=pl.ANY)],
        out_specs=pl.BlockSpec(memory_space=pl.ANY),
        scratch_shapes=[pltpu.SemaphoreType.DMA(())],
    )
    return pl.pallas_call(
        functools.partial(_dispatch_kernel, bt=bt),
        grid_spec=grid_spec,
        out_shape=jax.ShapeDtypeStruct((rows, w), jnp.uint32),
        input_output_aliases={4: 0},
        compiler_params=pltpu.CompilerParams(dimension_semantics=("arbitrary",), has_side_effects=True),
        name="moe_dispatch",
    )(e_flat, p_flat, pad_start, hu, jnp.zeros((rows, w), jnp.uint32))


def _finish_kernel(e_ref, p_ref, ps_ref, y_ref, rg_ref, x_ref, mod_ref, g_ref, o_ref, buf, sem, *, bt):
    i = pl.program_id(0)
    n = pl.num_programs(0)

    def issue(blk, slot):
        def body(t, carry):
            for k in range(TOP_K):
                src = _slot_row(e_ref, p_ref, ps_ref, (blk * bt + t) * TOP_K + k)
                pltpu.make_async_copy(y_ref.at[pl.ds(src, 1)], buf.at[slot, k, pl.ds(t, 1)], sem.at[slot]).start()
            return carry

        lax.fori_loop(0, bt, body, 0)

    @pl.when(i == 0)
    def _():
        issue(0, 0)

    @pl.when(i + 1 < n)
    def _():
        issue(i + 1, (i + 1) % 2)

    slot = i % 2
    for k in range(TOP_K):
        pltpu.make_async_copy(y_ref.at[pl.ds(0, bt)], buf.at[slot, k], sem.at[slot]).wait()

    rg = rg_ref[...]
    q = GM2_TN // 2
    parts = [None] * (2 * (D_MODEL // GM2_TN))
    for k in range(TOP_K):
        gate = rg[:, k:k + 1]
        for j in range(D_MODEL // GM2_TN):
            lo, hi = _unpack_halves(buf[slot, k, :, j * q:(j + 1) * q])
            for idx, val in ((2 * j, lo), (2 * j + 1, hi)):
                parts[idx] = gate * val if k == 0 else parts[idx] + gate * val
    f = jnp.concatenate(parts, axis=-1)
    fn = f * lax.rsqrt(jnp.mean(f * f, axis=-1, keepdims=True) + EPS) * g_ref[...]
    o_ref[...] = x_ref[...] + mod_ref[0, 5:6, :] * fn


def _finish(e_flat, p_flat, pad_start, yu, rg, x1, mod, g, tp, ls):
    t = x1.shape[0]
    bt = FIN_BM
    row = lambda w: pl.BlockSpec((bt, w), lambda i, e, p, ps: (i, 0))
    grid_spec = pltpu.PrefetchScalarGridSpec(
        num_scalar_prefetch=3,
        grid=(t // bt,),
        in_specs=[pl.BlockSpec(memory_space=pl.ANY),
                  row(ROUTER_PAD),
                  row(D_MODEL),
                  pl.BlockSpec((1, 6, D_MODEL), lambda i, e, p, ps: (_mod_row(i, bt, tp, ls), 0, 0)),
                  pl.BlockSpec((1, D_MODEL), lambda i, e, p, ps: (0, 0))],
        out_specs=row(D_MODEL),
        scratch_shapes=[pltpu.VMEM((2, TOP_K, bt, D_MODEL // 2), jnp.uint32), pltpu.SemaphoreType.DMA((2,))],
    )
    return pl.pallas_call(
        functools.partial(_finish_kernel, bt=bt),
        grid_spec=grid_spec,
        out_shape=jax.ShapeDtypeStruct((t, D_MODEL), F32),
        compiler_params=_cp(("arbitrary",), 48),
        name="moe_finish",
    )(e_flat, p_flat, pad_start, yu, rg, x1, mod, g)


def _moe(hu, ri, rg, cnt, x1, mod, g, w_gu, b_gu, w_dn, b_dn, l, tp, ls):
    t = hu.shape[0]
    n_blocks = -(-t * TOP_K // MOE_BM) + N_EXPERTS
    counts = cnt[0, :N_EXPERTS].astype(jnp.int32)
    pad_start, block_e, n_used = _block_layout(counts, n_blocks)
    e_flat = ri[:, :TOP_K].reshape(-1)
    p_flat = ri[:, TOP_K:2 * TOP_K].reshape(-1)
    xs = _dispatch(e_flat, p_flat, pad_start, hu, n_blocks * MOE_BM)
    act = _gm1(block_e, n_used, xs, w_gu, b_gu, l)
    yu = _gm2(block_e, n_used, act, w_dn, b_dn, l)
    return _finish(e_flat, p_flat, pad_start, yu, rg, x1, mod, g, tp, ls)


def kernel(x_prompt, x_sample, cache_attn_k, cache_attn_v, cache_diff_k, cache_diff_v, c, c_ctx, w_mod, b_mod, norm_g, w_in, w_out, hy_conv_w, hy_conv_b, hy_ffn_w1, hy_ffn_b1, hy_ffn_w2, hy_ffn_b2, hy_ffn_w3, hy_sin_freq, hy_skip, sgu_norm_g, sgu_norm_b, sgu_w, sgu_b, attn_q_g, attn_k_g, diff_lambda, diff_norm_g, w_router, b_router, w_gate_up, b_gate_up, w_down, b_down):
    bp, lp, d = x_prompt.shape
    bs, ls, _ = x_sample.shape
    tp = bp * lp
    past = cache_attn_k.shape[2]
    x = jnp.concatenate([x_prompt.reshape(tp, d), x_sample.reshape(bs * ls, d)], axis=0)

    cond = jnp.concatenate([c_ctx[None, :], c, jnp.zeros((COND_ROWS - 1 - bs, d), F32)], axis=0)
    mod = _modulation(cond, w_mod, b_mod).reshape(DEPTH, COND_ROWS, 6, d)

    w_in_b = w_in.astype(BF16)
    w_out_b = w_out.astype(BF16)
    wr_f = jnp.pad(w_router, ((0, 0), (0, 0), (0, ROUTER_PAD - N_EXPERTS)))
    wr_hi = wr_f.astype(BF16)
    wr = jnp.concatenate([wr_hi, (wr_f - wr_hi.astype(F32)).astype(BF16)], axis=-1)
    br = jnp.pad(b_router, ((0, 0), (0, ROUTER_PAD - N_EXPERTS))).reshape(DEPTH, 1, ROUTER_PAD)

    pk = cache_attn_k.reshape(bs, DEPTH, past, -1)
    pv = cache_attn_v.reshape(bs, DEPTH, past, -1)
    pdk = cache_diff_k.reshape(bs, DEPTH, past, -1)
    pdv = cache_diff_v.reshape(bs, DEPTH, past, -1)

    groups = []
    for L in (lp, ls):
        cm, sm, alt8, fwd, inv = _dft_tables(L)
        z, window = _filter_consts(L)
        groups.append((L, cm, sm, alt8, fwd, inv, z, window))
    qw = ATTN_HEADS * ATTN_HEAD_DIM
    kw = ATTN_KV_HEADS * ATTN_HEAD_DIM
    attn_tables = _rope_tables(ls, ATTN_HEAD_DIM, qw) + _rope_tables(ls, ATTN_HEAD_DIM, kw)
    diff_tables = _rope_tables(ls, DIFF_QK_DIM, GROUP_WIDTH)

    new_k, new_v, new_dk, new_dv = [], [], [], []
    for l in range(DEPTH):
        ng = norm_g[l]
        proj = _in_proj(x, mod[l], ng[0:1], w_in_b[l], tp, ls)

        ys = []
        for gi, (B, row_off) in enumerate(((bp, 0), (bs, tp // ls))):
            L, cm, sm, alt8, fwd, inv, z, window = groups[gi]
            kre, kim = _hyena_filter(L, z, window, cm, sm, alt8, hy_ffn_w1[l], hy_ffn_b1[l], hy_ffn_w2[l],
                                     hy_ffn_b2[l], hy_ffn_w3[l], hy_sin_freq[l])
            ya = _hyena(proj, row_off, B, L, hy_conv_w[l], hy_conv_b[l], kre, kim, hy_skip[l], fwd, inv)
            yb = _sgu(proj, row_off, B, L, sgu_norm_g[l], sgu_norm_b[l], sgu_w[l], sgu_b[l])
            if gi == 0:
                yc, k_ctx = _gqa(proj, row_off, B, L, attn_q_g[l], attn_k_g[l], l)
                yd = _diff(proj, row_off, B, L, diff_lambda[l], diff_norm_g[l], l)
                new_k.append(k_ctx.reshape(bp, lp, ATTN_KV_HEADS, ATTN_HEAD_DIM))
                new_v.append(proj[:tp, 3328:3584].reshape(bp, lp, ATTN_KV_HEADS, ATTN_HEAD_DIM))
                new_dk.append(proj[:tp, 4096:4608].reshape(bp, lp, DIFF_HEADS, 2, DIFF_QK_DIM))
                new_dv.append(proj[:tp, 4608:5120].reshape(bp, lp, DIFF_HEADS, DIFF_V_DIM))
            else:
                yc = _gqa(proj, row_off, B, L, attn_q_g[l], attn_k_g[l], l, past=(pk, pv), tables=attn_tables)
                yd = _diff(proj, row_off, B, L, diff_lambda[l], diff_norm_g[l], l, past=(pdk, pdv),
                           tables=diff_tables)
            ys.append((ya, yb, yc, yd))
        ycat = [jnp.concatenate([ys[0][m], ys[1][m]], axis=0) for m in range(4)]

        x1, hu, ri, rg, cnt = _out_proj(ycat, w_out_b[l], x, mod[l], ng, wr[l], br[l], tp, ls)
        x = _moe(hu, ri, rg, cnt, x1, mod[l], ng[3:4], w_gate_up, b_gate_up, w_down, b_down, l, tp, ls)

    y_prompt = x[:tp].reshape(bp, lp, d)
    y_sample = x[tp:].reshape(bs, ls, d)
    return (y_prompt, y_sample, jnp.stack(new_k, axis=1), jnp.stack(new_v, axis=1),
            jnp.stack(new_dk, axis=1), jnp.stack(new_dv, axis=1))
```

```python
import functools
import math

import numpy as np
import jax
import jax.numpy as jnp
from jax import lax
from jax.experimental import pallas as pl
from jax.experimental.pallas import tpu as pltpu

F32 = jnp.float32
BF16 = jnp.bfloat16
HIGHEST = lax.Precision.HIGHEST

D_MODEL = 2048
DEPTH = 2
GRID_W = 64
GROUP_WIDTH = 512
EPS = 1e-6
IN_COLS = 5120
HYENA_BANDS = 16
HYENA_POS_DIM = 1 + 2 * HYENA_BANDS
HYENA_FFN = 64
HYENA_FAST_DECAY = 0.3
HYENA_SLOW_DECAY = 1.5
HYENA_TARGET = 0.01
POS_PAD = 128
CHUNK = 128
SGU_HEADS = 4
ATTN_HEADS = 4
ATTN_KV_HEADS = 2
ATTN_HEAD_DIM = 128
ROPE_THETA = 10000.0
DIFF_HEADS = 4
DIFF_QK_DIM = 64
DIFF_V_DIM = 128
N_EXPERTS = 32
TOP_K = 4
D_FF = 2048
SWIGLU_LIMIT = 7.0
SWIGLU_ALPHA = 1.702
ROUTER_PAD = 128

IN_BM, IN_BN = 1024, 1024
OUT_BM = 256
MOE_BM = 512
GM1_TN = 1024
GM1_SUB = 256
GM2_TN = 2048
DISPATCH_BT = 512
FIN_BM = 256
COND_ROWS = 16
MOD_TN = 1024
HY_CC = 256
Q_CHUNK = 512
MIB = 2 ** 20


def _cp(sem, vmem_mib):
    return pltpu.CompilerParams(dimension_semantics=sem, vmem_limit_bytes=vmem_mib * MIB)


def _mod_kernel(c_ref, w_ref, b_ref, o_ref):
    c = c_ref[...]
    s = c * jax.nn.sigmoid(c)
    o_ref[...] = jnp.dot(s.astype(BF16), w_ref[...].astype(BF16), preferred_element_type=F32) + b_ref[...]


def _modulation(cond, w_mod, b_mod):
    n = w_mod.shape[-1]
    return pl.pallas_call(
        _mod_kernel,
        grid=(DEPTH, n // MOD_TN),
        in_specs=[
            pl.BlockSpec((COND_ROWS, D_MODEL), lambda l, j: (0, 0)),
            pl.BlockSpec((None, D_MODEL, MOD_TN), lambda l, j: (l, 0, j)),
            pl.BlockSpec((None, 1, MOD_TN), lambda l, j: (l, 0, j)),
        ],
        out_specs=pl.BlockSpec((None, COND_ROWS, MOD_TN), lambda l, j: (l, 0, j)),
        out_shape=jax.ShapeDtypeStruct((DEPTH, COND_ROWS, n), F32),
        compiler_params=_cp(("arbitrary", "arbitrary"), 40),
        name="modulation",
    )(cond, w_mod, b_mod.reshape(DEPTH, 1, n))


def _mod_row(i, bm, tp, ls):
    nbp = tp // bm
    return jnp.where(i < nbp, 0, 1 + (i - nbp) // (ls // bm))


def _in_proj_kernel(x_ref, mod_ref, g_ref, w_ref, o_ref, h_ref):
    @pl.when(pl.program_id(1) == 0)
    def _():
        x = x_ref[...]
        r = lax.rsqrt(jnp.mean(x * x, axis=-1, keepdims=True) + EPS)
        h = (x * r) * g_ref[...] * (1.0 + mod_ref[0, 1:2, :]) + mod_ref[0, 0:1, :]
        h_ref[...] = h.astype(BF16)

    o_ref[...] = jnp.dot(h_ref[...], w_ref[...], preferred_element_type=F32)


def _in_proj(x, mod, g, w, tp, ls):
    t = x.shape[0]
    bm = min(IN_BM, ls, tp)
    return pl.pallas_call(
        _in_proj_kernel,
        grid=(t // bm, IN_COLS // IN_BN),
        in_specs=[
            pl.BlockSpec((bm, D_MODEL), lambda i, j: (i, 0)),
            pl.BlockSpec((1, 6, D_MODEL), lambda i, j: (_mod_row(i, bm, tp, ls), 0, 0)),
            pl.BlockSpec((1, D_MODEL), lambda i, j: (0, 0)),
            pl.BlockSpec((D_MODEL, IN_BN), lambda i, j: (0, j)),
        ],
        out_specs=pl.BlockSpec((bm, IN_BN), lambda i, j: (i, j)),
        out_shape=jax.ShapeDtypeStruct((t, IN_COLS), F32),
        scratch_shapes=[pltpu.VMEM((bm, D_MODEL), BF16)],
        compiler_params=_cp(("arbitrary", "arbitrary"), 56),
        name="in_proj",
    )(x, mod, g, w)


def _dft_tables(L):
    k = np.arange(L, dtype=np.int64)
    ang = ((k[:, None] * k[None, :]) % (2 * L)).astype(np.float64) * (math.pi / L)
    cm = np.cos(ang)
    sm = np.sin(ang)
    alt = np.where(k % 2 == 0, 1.0, -1.0)
    sf = -sm
    sf[0] = alt
    fwd = np.concatenate([cm, sf], axis=0)
    wk = np.where(k == 0, 1.0, 2.0) / (2 * L)
    si = -sm / L
    si[:, 0] = alt / (2 * L)
    inv = np.concatenate([cm * wk[None, :], si], axis=1)
    alt8 = np.zeros((8, L))
    alt8[0] = alt
    return (jnp.asarray(cm, F32), jnp.asarray(sm, F32), jnp.asarray(alt8, F32),
            jnp.asarray(fwd.astype(BF16)), jnp.asarray(inv.astype(BF16)))


def _filter_consts(L):
    t = np.linspace(0.0, 1.0, L, dtype=np.float32)[:, None]
    bands = np.linspace(1e-4, HYENA_BANDS - 1, HYENA_BANDS, dtype=np.float32)[None, :]
    z = np.concatenate([t, np.cos(2 * math.pi * bands * t), np.sin(2 * math.pi * bands * t)], axis=-1)
    z = np.pad(z, ((0, 0), (0, POS_PAD - HYENA_POS_DIM)))
    max_decay = math.log(HYENA_TARGET) / HYENA_FAST_DECAY
    min_decay = math.log(HYENA_TARGET) / HYENA_SLOW_DECAY
    deltas = np.abs(np.linspace(min_decay, max_decay, GROUP_WIDTH, dtype=np.float32))
    window = np.exp(-t * deltas[None, :])
    return jnp.asarray(z, F32), jnp.asarray(window, F32)


def _filter_kernel(z_ref, w1_ref, b1_ref, w2_ref, b2_ref, w3_ref, fr_ref, win_ref, cm_ref, sm_ref, alt_ref,
                   kre_ref, kim_ref):
    c = GROUP_WIDTH
    h = jnp.dot(z_ref[...], w1_ref[...], precision=HIGHEST, preferred_element_type=F32) + b1_ref[...]
    h = jnp.sin(fr_ref[0:1, :] * h)
    h = jnp.dot(h, w2_ref[...], precision=HIGHEST, preferred_element_type=F32) + b2_ref[...]
    h = jnp.sin(fr_ref[1:2, :] * h)
    hfb = jnp.dot(h, w3_ref[...], precision=HIGHEST, preferred_element_type=F32)
    win = win_ref[...]
    hf = hfb[:, :c] * win
    hb = hfb[:, c:] * win
    row = lax.broadcasted_iota(jnp.int32, hf.shape, 0)
    g = hf + jnp.where(row == 0, 0.0, hb)
    kre = jnp.dot(cm_ref[...], g, precision=HIGHEST, preferred_element_type=F32)
    kim = jnp.dot(sm_ref[...], hb - hf, precision=HIGHEST, preferred_element_type=F32)
    kn = jnp.dot(alt_ref[...], g, precision=HIGHEST, preferred_element_type=F32)[0:1, :]
    kre_ref[...] = kre
    kim_ref[...] = jnp.where(row == 0, kn, kim)


def _hyena_filter(L, z, window, cm, sm, alt8, w1, b1, w2, b2, w3, freq):
    w1p = jnp.pad(w1, ((0, POS_PAD - HYENA_POS_DIM), (0, 0)))
    return pl.pallas_call(
        _filter_kernel,
        out_shape=(jax.ShapeDtypeStruct((L, GROUP_WIDTH), F32), jax.ShapeDtypeStruct((L, GROUP_WIDTH), F32)),
        compiler_params=pltpu.CompilerParams(vmem_limit_bytes=56 * MIB),
        name="hyena_filter",
    )(z, w1p, b1.reshape(1, -1), w2, b2.reshape(1, -1), w3, freq, window, cm, sm, alt8)


def _short_conv(a, w_ref, b_ref, row, L):
    prev = jnp.where(row == 0, 0.0, pltpu.roll(a, 1, 0))
    nxt = jnp.where(row == L - 1, 0.0, pltpu.roll(a, L - 1, 0))
    return b_ref[...] + prev * w_ref[0:1, :] + a * w_ref[1:2, :] + nxt * w_ref[2:3, :]


def _hyena_kernel(a0_ref, a1_ref, a2_ref, w0_ref, w1_ref, w2_ref, b0_ref, b1_ref, b2_ref, kre_ref, kim_ref,
                  skip_ref, fwd_ref, inv_ref, o_ref, *, L):
    row = lax.broadcasted_iota(jnp.int32, (L, HY_CC), 0)
    x1 = _short_conv(a1_ref[...], w1_ref, b1_ref, row, L)
    hv = _short_conv(a2_ref[...], w2_ref, b2_ref, row, L)
    u = hv * x1
    spec = jnp.dot(fwd_ref[...], u.astype(BF16), preferred_element_type=F32)
    ure, uim = spec[:L], spec[L:]
    kre, kim = kre_ref[...], kim_ref[...]
    first = row == 0
    yre = ure * kre - jnp.where(first, 0.0, uim * kim)
    yim = jnp.where(first, uim * kim, ure * kim + uim * kre)
    yspec = jnp.concatenate([yre, yim], axis=0).astype(BF16)
    y = jnp.dot(inv_ref[...], yspec, preferred_element_type=F32)
    x0 = _short_conv(a0_ref[...], w0_ref, b0_ref, row, L)
    o_ref[...] = (x0 * (y + u * skip_ref[...])).astype(BF16)


def _hyena(proj, row_off, B, L, conv_w, conv_b, kre, kim, skip, fwd, inv):
    ncc = GROUP_WIDTH // HY_CC

    def a_spec(part):
        return pl.BlockSpec((L, HY_CC), lambda b, cc: (row_off + b, part * ncc + cc))

    def w_spec(part, rows):
        return pl.BlockSpec((rows, HY_CC), lambda b, cc: (0, part * ncc + cc))

    chan = pl.BlockSpec((L, HY_CC), lambda b, cc: (0, cc))
    conv_b = conv_b.reshape(1, -1)
    return pl.pallas_call(
        functools.partial(_hyena_kernel, L=L),
        grid=(B, ncc),
        in_specs=[a_spec(0), a_spec(1), a_spec(2), w_spec(0, 3), w_spec(1, 3), w_spec(2, 3),
                  w_spec(0, 1), w_spec(1, 1), w_spec(2, 1), chan, chan,
                  pl.BlockSpec((1, HY_CC), lambda b, cc: (0, cc)),
                  pl.BlockSpec((2 * L, L), lambda b, cc: (0, 0)),
                  pl.BlockSpec((L, 2 * L), lambda b, cc: (0, 0))],
        out_specs=pl.BlockSpec((L, HY_CC), lambda b, cc: (b, cc)),
        out_shape=jax.ShapeDtypeStruct((B * L, GROUP_WIDTH), BF16),
        compiler_params=_cp(("arbitrary", "arbitrary"), 56),
        name="hyena_conv",
    )(proj, proj, proj, conv_w, conv_w, conv_w, conv_b, conv_b, conv_b, kre, kim, skip.reshape(1, -1), fwd, inv)


def _sgu_kernel(u_ref, v_ref, g_ref, b_ref, w_ref, bs_ref, o_ref, *, L):
    v = v_ref[...]
    mu = jnp.mean(v, axis=-1, keepdims=True)
    xc = v - mu
    vn = xc * lax.rsqrt(jnp.mean(xc * xc, axis=-1, keepdims=True) + EPS) * g_ref[...] + b_ref[...]
    vnb = vn.astype(BF16)
    hc = GROUP_WIDTH // SGU_HEADS
    for h in range(SGU_HEADS):
        wh = w_ref[h].astype(BF16)
        bias = bs_ref[h]
        cols = slice(h * hc, (h + 1) * hc)
        for n in range(L // CHUNK):
            rows = slice(n * CHUNK, (n + 1) * CHUNK)
            mixed = jnp.dot(wh, vnb[rows, cols], preferred_element_type=F32) + bias
            o_ref[rows, cols] = (u_ref[rows, cols] * mixed).astype(BF16)


def _sgu(proj, row_off, B, L, g, b, w_s, b_s):
    hc = GROUP_WIDTH // SGU_HEADS
    bias = jnp.broadcast_to(b_s[:, :, None], (SGU_HEADS, CHUNK, hc))
    return pl.pallas_call(
        functools.partial(_sgu_kernel, L=L),
        grid=(B,),
        in_specs=[pl.BlockSpec((L, GROUP_WIDTH), lambda i: (row_off + i, 3)),
                  pl.BlockSpec((L, GROUP_WIDTH), lambda i: (row_off + i, 4)),
                  pl.BlockSpec((1, GROUP_WIDTH), lambda i: (0, 0)),
                  pl.BlockSpec((1, GROUP_WIDTH), lambda i: (0, 0)),
                  pl.BlockSpec((SGU_HEADS, CHUNK, CHUNK), lambda i: (0, 0, 0)),
                  pl.BlockSpec((SGU_HEADS, CHUNK, hc), lambda i: (0, 0, 0))],
        out_specs=pl.BlockSpec((L, GROUP_WIDTH), lambda i: (i, 0)),
        out_shape=jax.ShapeDtypeStruct((B * L, GROUP_WIDTH), BF16),
        compiler_params=_cp(("arbitrary",), 40),
        name="sgu",
    )(proj, proj, g.reshape(1, -1), b.reshape(1, -1), w_s, bias)


def _rope_tables(L, dh, width):
    quarter = dh // 4
    pos = np.arange(L)
    rowp = (pos // GRID_W).astype(np.float32)
    colp = (pos % GRID_W).astype(np.float32)
    inv_freq = (ROPE_THETA ** (-np.arange(quarter, dtype=np.float32) / quarter)).astype(np.float32)
    ar = (rowp[:, None] * inv_freq[None, :]).astype(np.float64)
    ac = (colp[:, None] * inv_freq[None, :]).astype(np.float64)
    cos = np.concatenate([np.cos(ar), np.cos(ar), np.cos(ac), np.cos(ac)], axis=-1)
    sin = np.concatenate([-np.sin(ar), np.sin(ar), -np.sin(ac), np.sin(ac)], axis=-1)
    reps = width // dh
    return jnp.asarray(np.tile(cos, (1, reps)), F32), jnp.asarray(np.tile(sin, (1, reps)), F32)


def _rope(x, cos, sin, quarter):
    w = x.shape[-1]
    lane = lax.broadcasted_iota(jnp.int32, x.shape, 1)
    partner = jnp.where((lane & quarter) != 0, pltpu.roll(x, quarter, 1), pltpu.roll(x, w - quarter, 1))
    return x * cos + partner * sin


def _head_rms(x, g, heads, dh):
    outs = []
    for h in range(heads):
        xh = x[:, h * dh:(h + 1) * dh]
        outs.append(xh * lax.rsqrt(jnp.mean(xh * xh, axis=-1, keepdims=True) + EPS) * g)
    return jnp.concatenate(outs, axis=-1)


def _softmax_pv(q, k, v):
    s = lax.dot_general(q, k, (((1,), (1,)), ((), ())), preferred_element_type=F32)
    e = jnp.exp(s - jnp.max(s, axis=-1, keepdims=True))
    inv_l = 1.0 / jnp.sum(e, axis=-1, keepdims=True)
    return jnp.dot(e.astype(BF16), v, preferred_element_type=F32), inv_l


def _gqa_body(q, k_ref, v_ref, o_ref, L):
    dh = ATTN_HEAD_DIM
    group = ATTN_HEADS // ATTN_KV_HEADS
    qs = (q * dh ** -0.5).astype(BF16)
    qc = min(Q_CHUNK, L)
    for h in range(ATTN_HEADS):
        kv = h // group
        kh = k_ref[:, kv * dh:(kv + 1) * dh]
        vh = v_ref[:, kv * dh:(kv + 1) * dh]
        for c in range(L // qc):
            rows = slice(c * qc, (c + 1) * qc)
            pv, inv_l = _softmax_pv(qs[rows, h * dh:(h + 1) * dh], kh, vh)
            o_ref[rows, h * dh:(h + 1) * dh] = (pv * inv_l).astype(BF16)


def _gqa_ctx_kernel(q_ref, k_ref, v_ref, gq_ref, gk_ref, o_ref, ko_ref, vo_ref, kb_ref, vb_ref, *, L):
    q = _head_rms(q_ref[...], gq_ref[...], ATTN_HEADS, ATTN_HEAD_DIM)
    k = _head_rms(k_ref[...], gk_ref[...], ATTN_KV_HEADS, ATTN_HEAD_DIM)
    v = v_ref[...]
    ko_ref[...] = k
    vo_ref[...] = v
    kb_ref[...] = k.astype(BF16)
    vb_ref[...] = v.astype(BF16)
    _gqa_body(q, kb_ref, vb_ref, o_ref, L)


def _gqa_lat_kernel(q_ref, k_ref, v_ref, gq_ref, gk_ref, cq_ref, sq_ref, ck_ref, sk_ref, pk_ref, pv_ref, o_ref,
                    kb_ref, vb_ref, *, L, P):
    quarter = ATTN_HEAD_DIM // 4
    q = _rope(_head_rms(q_ref[...], gq_ref[...], ATTN_HEADS, ATTN_HEAD_DIM), cq_ref[...], sq_ref[...], quarter)
    k = _rope(_head_rms(k_ref[...], gk_ref[...], ATTN_KV_HEADS, ATTN_HEAD_DIM), ck_ref[...], sk_ref[...], quarter)
    kb_ref[0:P, :] = pk_ref[...].astype(BF16)
    kb_ref[P:P + L, :] = k.astype(BF16)
    vb_ref[0:P, :] = pv_ref[...].astype(BF16)
    vb_ref[P:P + L, :] = v_ref[...].astype(BF16)
    _gqa_body(q, kb_ref, vb_ref, o_ref, L)


def _gqa(proj, row_off, B, L, gq, gk, l, past=None, tables=None):
    qw = ATTN_HEADS * ATTN_HEAD_DIM
    kw = ATTN_KV_HEADS * ATTN_HEAD_DIM
    qoff = 5
    koff = (2560 + qw) // kw
    in_specs = [pl.BlockSpec((L, qw), lambda i: (row_off + i, qoff)),
                pl.BlockSpec((L, kw), lambda i: (row_off + i, koff)),
                pl.BlockSpec((L, kw), lambda i: (row_off + i, koff + 1)),
                pl.BlockSpec((1, ATTN_HEAD_DIM), lambda i: (0, 0)),
                pl.BlockSpec((1, ATTN_HEAD_DIM), lambda i: (0, 0))]
    args = [proj, proj, proj, gq.reshape(1, -1), gk.reshape(1, -1)]
    y_spec = pl.BlockSpec((L, qw), lambda i: (i, 0))
    y_shape = jax.ShapeDtypeStruct((B * L, qw), BF16)
    if past is None:
        return pl.pallas_call(
            functools.partial(_gqa_ctx_kernel, L=L),
            grid=(B,),
            in_specs=in_specs,
            out_specs=(y_spec, pl.BlockSpec((L, kw), lambda i: (i, 0)), pl.BlockSpec((L, kw), lambda i: (i, 0))),
            out_shape=(y_shape, jax.ShapeDtypeStruct((B * L, kw), F32), jax.ShapeDtypeStruct((B * L, kw), F32)),
            scratch_shapes=[pltpu.VMEM((L, kw), BF16), pltpu.VMEM((L, kw), BF16)],
            compiler_params=_cp(("arbitrary",), 40),
            name="gqa_ctx",
        )(*args)
    pk, pv = past
    P = pk.shape[2]
    cq, sq, ck, sk = tables
    const = lambda w: pl.BlockSpec((L, w), lambda i: (0, 0))
    cache = pl.BlockSpec((None, None, P, kw), lambda i: (i, l, 0, 0))
    return pl.pallas_call(
        functools.partial(_gqa_lat_kernel, L=L, P=P),
        grid=(B,),
        in_specs=in_specs + [const(qw), const(qw), const(kw), const(kw), cache, cache],
        out_specs=y_spec,
        out_shape=y_shape,
        scratch_shapes=[pltpu.VMEM((P + L, kw), BF16), pltpu.VMEM((P + L, kw), BF16)],
        compiler_params=_cp(("arbitrary",), 56),
        name="gqa_lat",
    )(*args, cq, sq, ck, sk, pk, pv)


def _diff_body(dq, k_ref, v_ref, dl_ref, g_ref, o_ref, L, lambda_init):
    dv = DIFF_V_DIM
    dl = dl_ref[...]
    lam = (jnp.exp(jnp.sum(dl[0:1] * dl[1:2], axis=-1, keepdims=True))
           - jnp.exp(jnp.sum(dl[2:3] * dl[3:4], axis=-1, keepdims=True)) + lambda_init)
    qs = dq * DIFF_QK_DIM ** -0.5
    first_map = lax.broadcasted_iota(jnp.int32, (1, dv), 1) < DIFF_QK_DIM
    qc = min(Q_CHUNK, L)
    for h in range(DIFF_HEADS):
        cols = slice(h * dv, (h + 1) * dv)
        kh = k_ref[:, cols]
        vh = v_ref[:, cols]
        for c in range(L // qc):
            rows = slice(c * qc, (c + 1) * qc)
            qh = qs[rows, cols]
            pv0, inv0 = _softmax_pv(jnp.where(first_map, qh, 0.0).astype(BF16), kh, vh)
            pv1, inv1 = _softmax_pv(jnp.where(first_map, 0.0, qh).astype(BF16), kh, vh)
            o = pv0 * inv0 - pv1 * (lam * inv1)
            r = lax.rsqrt(jnp.mean(o * o, axis=-1, keepdims=True) + EPS)
            o_ref[rows, cols] = ((o * r) * g_ref[...] * (1.0 - lambda_init)).astype(BF16)


def _diff_ctx_kernel(q_ref, k_ref, v_ref, dl_ref, g_ref, o_ref, ko_ref, vo_ref, kb_ref, vb_ref, *, L, lambda_init):
    k = k_ref[...]
    v = v_ref[...]
    ko_ref[...] = k
    vo_ref[...] = v
    kb_ref[...] = k.astype(BF16)
    vb_ref[...] = v.astype(BF16)
    _diff_body(q_ref[...], kb_ref, vb_ref, dl_ref, g_ref, o_ref, L, lambda_init)


def _diff_lat_kernel(q_ref, k_ref, v_ref, dl_ref, g_ref, cos_ref, sin_ref, pk_ref, pv_ref, o_ref, kb_ref, vb_ref,
                     *, L, P, lambda_init):
    quarter = DIFF_QK_DIM // 4
    cos, sin = cos_ref[...], sin_ref[...]
    q = _rope(q_ref[...], cos, sin, quarter)
    k = _rope(k_ref[...], cos, sin, quarter)
    kb_ref[0:P, :] = pk_ref[...].astype(BF16)
    kb_ref[P:P + L, :] = k.astype(BF16)
    vb_ref[0:P, :] = pv_ref[...].astype(BF16)
    vb_ref[P:P + L, :] = v_ref[...].astype(BF16)
    _diff_body(q, kb_ref, vb_ref, dl_ref, g_ref, o_ref, L, lambda_init)


def _diff(proj, row_off, B, L, dl, g, l, past=None, tables=None):
    w = GROUP_WIDTH
    lambda_init = 0.8 - 0.6 * math.exp(-0.3 * l)
    in_specs = [pl.BlockSpec((L, w), lambda i: (row_off + i, 7)),
                pl.BlockSpec((L, w), lambda i: (row_off + i, 8)),
                pl.BlockSpec((L, w), lambda i: (row_off + i, 9)),
                pl.BlockSpec((4, DIFF_QK_DIM), lambda i: (0, 0)),
                pl.BlockSpec((1, DIFF_V_DIM), lambda i: (0, 0))]
    args = [proj, proj, proj, dl, g.reshape(1, -1)]
    y_spec = pl.BlockSpec((L, w), lambda i: (i, 0))
    y_shape = jax.ShapeDtypeStruct((B * L, w), BF16)
    if past is None:
        return pl.pallas_call(
            functools.partial(_diff_ctx_kernel, L=L, lambda_init=lambda_init),
            grid=(B,),
            in_specs=in_specs,
            out_specs=(y_spec, pl.BlockSpec((L, w), lambda i: (i, 0)), pl.BlockSpec((L, w), lambda i: (i, 0))),
            out_shape=(y_shape, jax.ShapeDtypeStruct((B * L, w), F32), jax.ShapeDtypeStruct((B * L, w), F32)),
            scratch_shapes=[pltpu.VMEM((L, w), BF16), pltpu.VMEM((L, w), BF16)],
            compiler_params=_cp(("arbitrary",), 40),
            name="diff_ctx",
        )(*args)
    pk, pv = past
    P = pk.shape[2]
    cos, sin = tables
    const = pl.BlockSpec((L, w), lambda i: (0, 0))
    cache = pl.BlockSpec((None, None, P, w), lambda i: (i, l, 0, 0))
    return pl.pallas_call(
        functools.partial(_diff_lat_kernel, L=L, P=P, lambda_init=lambda_init),
        grid=(B,),
        in_specs=in_specs + [const, const, cache, cache],
        out_specs=y_spec,
        out_shape=y_shape,
        scratch_shapes=[pltpu.VMEM((P + L, w), BF16), pltpu.VMEM((P + L, w), BF16)],
        compiler_params=_cp(("arbitrary",), 56),
        name="diff_lat",
    )(*args, cos, sin, pk, pv)


def _pack_halves(lo, hi):
    lo_bits = pltpu.bitcast(lo.astype(BF16).astype(F32), jnp.uint32) >> 16
    hi_bits = pltpu.bitcast(hi.astype(BF16).astype(F32), jnp.uint32) & jnp.uint32(0xFFFF0000)
    return lo_bits | hi_bits


def _unpack_halves(words):
    lo = pltpu.bitcast(words << 16, F32)
    hi = pltpu.bitcast(words & jnp.uint32(0xFFFF0000), F32)
    return lo, hi


def _route_rows(logits, prefix_base, tri_ref):
    lane = lax.broadcasted_iota(jnp.int32, logits.shape, 1)
    neg = jnp.float32(-jnp.inf)
    work = jnp.where(lane < N_EXPERTS, logits, neg)
    sels, vals, idxs = [], [], []
    for _ in range(TOP_K):
        mk = jnp.max(work, axis=-1, keepdims=True)
        ik = jnp.min(jnp.where(work == mk, lane, ROUTER_PAD), axis=-1, keepdims=True)
        sel = lane == ik
        work = jnp.where(sel, neg, work)
        sels.append(sel)
        vals.append(mk)
        idxs.append(ik)
    es = [jnp.exp(v - vals[0]) for v in vals]
    inv = 1.0 / (es[0] + es[1] + es[2] + es[3])
    member = jnp.zeros(logits.shape, F32)
    for sel in sels:
        member = member + jnp.where(sel, 1.0, 0.0)
    prefix = jnp.dot(tri_ref[...], member.astype(BF16), preferred_element_type=F32) + prefix_base
    ri = jnp.zeros(logits.shape, jnp.int32)
    rg = jnp.zeros(logits.shape, F32)
    for k in range(TOP_K):
        rank = jnp.sum(jnp.where(sels[k], prefix, 0.0), axis=-1, keepdims=True).astype(jnp.int32)
        ri = jnp.where(lane == k, idxs[k], ri)
        ri = jnp.where(lane == TOP_K + k, rank, ri)
        rg = jnp.where(lane == k, es[k] * inv, rg)
    return ri, rg, jnp.sum(member, axis=0, keepdims=True)


def _out_proj_kernel(pa_ref, pb_ref, pc_ref, pd_ref, sa_ref, sb_ref, sc_ref, sd_ref, w_ref, x_ref, mod_ref, ng_ref,
                     wr_ref, br_ref, tri_ref, x1_ref, hu_ref, ri_ref, rg_ref, cnt_ref, cat_ref, carry_ref, *, nbp):
    i = pl.program_id(0)

    @pl.when(i == 0)
    def _():
        carry_ref[...] = jnp.zeros_like(carry_ref)

    gw = GROUP_WIDTH

    @pl.when(i < nbp)
    def _():
        for m_, ref in enumerate((pa_ref, pb_ref, pc_ref, pd_ref)):
            cat_ref[:, m_ * gw:(m_ + 1) * gw] = ref[...]

    @pl.when(i >= nbp)
    def _():
        for m_, ref in enumerate((sa_ref, sb_ref, sc_ref, sd_ref)):
            cat_ref[:, m_ * gw:(m_ + 1) * gw] = ref[...]

    m = jnp.dot(cat_ref[...], w_ref[...], preferred_element_type=F32)
    mn = m * lax.rsqrt(jnp.mean(m * m, axis=-1, keepdims=True) + EPS) * ng_ref[1:2, :]
    x1 = x_ref[...] + mod_ref[0, 2:3, :] * mn
    x1_ref[...] = x1
    hn = x1 * lax.rsqrt(jnp.mean(x1 * x1, axis=-1, keepdims=True) + EPS) * ng_ref[2:3, :]
    h = hn * (1.0 + mod_ref[0, 4:5, :]) + mod_ref[0, 3:4, :]
    half = D_MODEL // 2
    hu_ref[...] = _pack_halves(h[:, :half], h[:, half:])
    hh = h.astype(BF16)
    hl = (h - hh.astype(F32)).astype(BF16)
    both = jnp.dot(hh, wr_ref[...], preferred_element_type=F32)
    low = jnp.dot(hl, wr_ref[:, :ROUTER_PAD], preferred_element_type=F32)
    logits = both[:, :ROUTER_PAD] + both[:, ROUTER_PAD:] + low + br_ref[...]
    ri, rg, counts = _route_rows(logits, carry_ref[0:1, :], tri_ref)
    ri_ref[...] = ri
    rg_ref[...] = rg
    carry_ref[...] = carry_ref[...] + counts
    cnt_ref[...] = carry_ref[...]


def _out_proj(ys_prompt, ys_latent, w, x, mod, ng, wr, br, tp, ls):
    t = x.shape[0]
    bm = OUT_BM
    nbp = tp // bm
    pspec = pl.BlockSpec((bm, GROUP_WIDTH), lambda i: (jnp.minimum(i, nbp - 1), 0))
    sspec = pl.BlockSpec((bm, GROUP_WIDTH), lambda i: (jnp.maximum(i - nbp, 0), 0))
    row = pl.BlockSpec((bm, D_MODEL), lambda i: (i, 0))
    info = pl.BlockSpec((bm, ROUTER_PAD), lambda i: (i, 0))
    tri = jnp.asarray(np.tril(np.ones((bm, bm), np.float32), -1).astype(BF16))
    return pl.pallas_call(
        functools.partial(_out_proj_kernel, nbp=nbp),
        grid=(t // bm,),
        in_specs=[pspec, pspec, pspec, pspec, sspec, sspec, sspec, sspec,
                  pl.BlockSpec((D_MODEL, D_MODEL), lambda i: (0, 0)),
                  row,
                  pl.BlockSpec((1, 6, D_MODEL), lambda i: (_mod_row(i, bm, tp, ls), 0, 0)),
                  pl.BlockSpec((4, D_MODEL), lambda i: (0, 0)),
                  pl.BlockSpec((D_MODEL, 2 * ROUTER_PAD), lambda i: (0, 0)),
                  pl.BlockSpec((1, ROUTER_PAD), lambda i: (0, 0)),
                  pl.BlockSpec((bm, bm), lambda i: (0, 0))],
        out_specs=(row, pl.BlockSpec((bm, D_MODEL // 2), lambda i: (i, 0)), info, info,
                   pl.BlockSpec((8, ROUTER_PAD), lambda i: (0, 0))),
        out_shape=(jax.ShapeDtypeStruct((t, D_MODEL), F32), jax.ShapeDtypeStruct((t, D_MODEL // 2), jnp.uint32),
                   jax.ShapeDtypeStruct((t, ROUTER_PAD), jnp.int32), jax.ShapeDtypeStruct((t, ROUTER_PAD), F32),
                   jax.ShapeDtypeStruct((8, ROUTER_PAD), F32)),
        scratch_shapes=[pltpu.VMEM((bm, D_MODEL), BF16), pltpu.VMEM((8, ROUTER_PAD), F32)],
        compiler_params=_cp(("arbitrary",), 56),
        name="out_proj",
    )(*ys_prompt, *ys_latent, w, x, mod, ng, wr, br, tri)


def _new_expert(be_ref, i):
    return jnp.logical_or(i == 0, be_ref[jnp.maximum(i - 1, 0)] != be_ref[i])


def _gm1_kernel(be_ref, nu_ref, x_ref, wg_ref, wu_ref, bg_ref, bu_ref, o_ref, wbf_ref, xb_ref):
    i = pl.program_id(1)
    used = i < nu_ref[0]
    tn = GM1_TN
    half = D_MODEL // 2

    @pl.when(jnp.logical_and(used, _new_expert(be_ref, i)))
    def _():
        wbf_ref[:, :tn] = wg_ref[...].astype(BF16)
        wbf_ref[:, tn:] = wu_ref[...].astype(BF16)

    @pl.when(used)
    def _():
        lo, hi = _unpack_halves(x_ref[...])
        xb_ref[:, :half] = lo.astype(BF16)
        xb_ref[:, half:] = hi.astype(BF16)
        for c in range(tn // GM1_SUB):
            cols = slice(c * GM1_SUB, (c + 1) * GM1_SUB)
            ucols = slice(tn + c * GM1_SUB, tn + (c + 1) * GM1_SUB)
            g = jnp.dot(xb_ref[...], wbf_ref[:, cols], preferred_element_type=F32)
            u = jnp.dot(xb_ref[...], wbf_ref[:, ucols], preferred_element_type=F32)
            gate = jnp.minimum(g + bg_ref[:, cols], SWIGLU_LIMIT)
            up = jnp.clip(u + bu_ref[:, cols], -SWIGLU_LIMIT, SWIGLU_LIMIT)
            glu = gate * jax.nn.sigmoid(gate * SWIGLU_ALPHA)
            o_ref[:, cols] = ((up + 1.0) * glu).astype(BF16)

    @pl.when(jnp.logical_not(used))
    def _():
        o_ref[...] = jnp.zeros_like(o_ref)


def _gm1(block_e, n_used, xs, w_gu, b_gu, l):
    rows = xs.shape[0]
    nb = rows // MOE_BM
    nj = D_FF // GM1_TN
    b_gu = b_gu.reshape(DEPTH, N_EXPERTS, 1, 2 * D_FF)
    last = lambda i, nu: jnp.minimum(i, nu[0] - 1)
    grid_spec = pltpu.PrefetchScalarGridSpec(
        num_scalar_prefetch=2,
        grid=(nj, nb),
        in_specs=[
            pl.BlockSpec((MOE_BM, D_MODEL // 2), lambda j, i, be, nu: (last(i, nu), 0)),
            pl.BlockSpec((None, None, D_MODEL, GM1_TN), lambda j, i, be, nu: (l, be[i], 0, j)),
            pl.BlockSpec((None, None, D_MODEL, GM1_TN), lambda j, i, be, nu: (l, be[i], 0, nj + j)),
            pl.BlockSpec((None, None, 1, GM1_TN), lambda j, i, be, nu: (l, be[i], 0, j)),
            pl.BlockSpec((None, None, 1, GM1_TN), lambda j, i, be, nu: (l, be[i], 0, nj + j)),
        ],
        out_specs=pl.BlockSpec((MOE_BM, GM1_TN), lambda j, i, be, nu: (i, j)),
        scratch_shapes=[pltpu.VMEM((D_MODEL, 2 * GM1_TN), BF16), pltpu.VMEM((MOE_BM, D_MODEL), BF16)],
    )
    return pl.pallas_call(
        _gm1_kernel,
        grid_spec=grid_spec,
        out_shape=jax.ShapeDtypeStruct((rows, D_FF), BF16),
        compiler_params=_cp(("arbitrary", "arbitrary"), 60),
        name="moe_gate_up",
    )(block_e, n_used, xs, w_gu, w_gu, b_gu, b_gu)


def _gm2_kernel(be_ref, nu_ref, a_ref, w_ref, b_ref, o_ref, wbf_ref):
    i = pl.program_id(1)
    used = i < nu_ref[0]
    half = GM2_TN // 2

    @pl.when(jnp.logical_and(used, _new_expert(be_ref, i)))
    def _():
        wbf_ref[...] = w_ref[...].astype(BF16)

    @pl.when(used)
    def _():
        lo = jnp.dot(a_ref[...], wbf_ref[:, :half], preferred_element_type=F32) + b_ref[:, :half]
        hi = jnp.dot(a_ref[...], wbf_ref[:, half:], preferred_element_type=F32) + b_ref[:, half:]
        o_ref[...] = _pack_halves(lo, hi)

    @pl.when(jnp.logical_not(used))
    def _():
        o_ref[...] = jnp.zeros_like(o_ref)


def _gm2(block_e, n_used, act, w_dn, b_dn, l):
    rows = act.shape[0]
    nb = rows // MOE_BM
    nj = D_MODEL // GM2_TN
    b_dn = b_dn.reshape(DEPTH, N_EXPERTS, 1, D_MODEL)
    last = lambda i, nu: jnp.minimum(i, nu[0] - 1)
    grid_spec = pltpu.PrefetchScalarGridSpec(
        num_scalar_prefetch=2,
        grid=(nj, nb),
        in_specs=[
            pl.BlockSpec((MOE_BM, D_FF), lambda j, i, be, nu: (last(i, nu), 0)),
            pl.BlockSpec((None, None, D_FF, GM2_TN), lambda j, i, be, nu: (l, be[i], 0, j)),
            pl.BlockSpec((None, None, 1, GM2_TN), lambda j, i, be, nu: (l, be[i], 0, j)),
        ],
        out_specs=pl.BlockSpec((MOE_BM, GM2_TN // 2), lambda j, i, be, nu: (i, j)),
        scratch_shapes=[pltpu.VMEM((D_FF, GM2_TN), BF16)],
    )
    return pl.pallas_call(
        _gm2_kernel,
        grid_spec=grid_spec,
        out_shape=jax.ShapeDtypeStruct((rows, D_MODEL // 2), jnp.uint32),
        compiler_params=_cp(("arbitrary", "arbitrary"), 60),
        name="moe_down",
    )(block_e, n_used, act, w_dn, b_dn)


def _block_layout(counts, n_blocks):
    padded = (counts + MOE_BM - 1) // MOE_BM * MOE_BM
    pad_end = jnp.cumsum(padded)
    pad_start = (pad_end - padded).astype(jnp.int32)
    starts = jnp.arange(n_blocks, dtype=jnp.int32) * MOE_BM
    block_e = jnp.minimum(jnp.sum(pad_end[None, :] <= starts[:, None], axis=1), N_EXPERTS - 1).astype(jnp.int32)
    n_used = (pad_end[-1] // MOE_BM).astype(jnp.int32).reshape(1)
    return pad_start, block_e, n_used


def _slot_row(e_ref, p_ref, ps_ref, a):
    return ps_ref[e_ref[a]] + p_ref[a]


def _dispatch_kernel(e_ref, p_ref, ps_ref, h_ref, xs_in_ref, xs_ref, sem, *, bt):
    del xs_in_ref
    base = pl.program_id(0) * bt

    def body(t, carry):
        for k in range(TOP_K):
            dst = _slot_row(e_ref, p_ref, ps_ref, (base + t) * TOP_K + k)
            pltpu.make_async_copy(h_ref.at[pl.ds(t, 1)], xs_ref.at[pl.ds(dst, 1)], sem).start()
        return carry

    lax.fori_loop(0, bt, body, 0, unroll=2)
    for _ in range(TOP_K):
        pltpu.make_async_copy(h_ref, xs_ref.at[pl.ds(0, bt)], sem).wait()


def _dispatch(e_flat, p_flat, pad_start, hu, rows):
    t, w = hu.shape
    bt = DISPATCH_BT
    grid_spec = pltpu.PrefetchScalarGridSpec(
        num_scalar_prefetch=3,
        grid=(t // bt,),
        in_specs=[pl.BlockSpec((bt, w), lambda i, e, p, ps: (i, 0)), pl.BlockSpec(memory_space=pl.ANY)],
        out_specs=pl.BlockSpec(memory_space=pl.ANY),
        scratch_shapes=[pltpu.SemaphoreType.DMA(())],
    )
    return pl.pallas_call(
        functools.partial(_dispatch_kernel, bt=bt),
        grid_spec=grid_spec,
        out_shape=jax.ShapeDtypeStruct((rows, w), jnp.uint32),
        input_output_aliases={4: 0},
        compiler_params=pltpu.CompilerParams(dimension_semantics=("arbitrary",), has_side_effects=True),
        name="moe_dispatch",
    )(e_flat, p_flat, pad_start, hu, jnp.zeros((rows, w), jnp.uint32))


def _finish_kernel(e_ref, p_ref, ps_ref, y_ref, rg_ref, x_ref, mod_ref, g_ref, *rest, bt, nbp):
    if nbp is None:
        o_ref, buf, sem = rest
    else:
        op_ref, os_ref, buf, sem = rest
    i = pl.program_id(0)
    n = pl.num_programs(0)

    def issue(blk, slot):
        def body(t, carry):
            for k in range(TOP_K):
                src = _slot_row(e_ref, p_ref, ps_ref, (blk * bt + t) * TOP_K + k)
                pltpu.make_async_copy(y_ref.at[pl.ds(src, 1)], buf.at[slot, k, pl.ds(t, 1)], sem.at[slot]).start()
            return carry

        lax.fori_loop(0, bt, body, 0)

    @pl.when(i == 0)
    def _():
        issue(0, 0)

    @pl.when(i + 1 < n)
    def _():
        issue(i + 1, (i + 1) % 2)

    slot = i % 2
    for k in range(TOP_K):
        pltpu.make_async_copy(y_ref.at[pl.ds(0, bt)], buf.at[slot, k], sem.at[slot]).wait()

    rg = rg_ref[...]
    q = GM2_TN // 2
    parts = [None] * (2 * (D_MODEL // GM2_TN))
    for k in range(TOP_K):
        gate = rg[:, k:k + 1]
        for j in range(D_MODEL // GM2_TN):
            lo, hi = _unpack_halves(buf[slot, k, :, j * q:(j + 1) * q])
            for idx, val in ((2 * j, lo), (2 * j + 1, hi)):
                parts[idx] = gate * val if k == 0 else parts[idx] + gate * val
    f = jnp.concatenate(parts, axis=-1)
    fn = f * lax.rsqrt(jnp.mean(f * f, axis=-1, keepdims=True) + EPS) * g_ref[...]
    res = x_ref[...] + mod_ref[0, 5:6, :] * fn
    if nbp is None:
        o_ref[...] = res
    else:
        @pl.when(i < nbp)
        def _():
            op_ref[...] = res

        @pl.when(i >= nbp)
        def _():
            os_ref[...] = res


def _finish(e_flat, p_flat, pad_start, yu, rg, x1, mod, g, tp, ls, split):
    t = x1.shape[0]
    bt = FIN_BM
    row = lambda w: pl.BlockSpec((bt, w), lambda i, e, p, ps: (i, 0))
    if split:
        nbp = tp // bt
        out_specs = (pl.BlockSpec((bt, D_MODEL), lambda i, e, p, ps: (jnp.minimum(i, nbp - 1), 0)),
                     pl.BlockSpec((bt, D_MODEL), lambda i, e, p, ps: (jnp.maximum(i - nbp, 0), 0)))
        out_shape = (jax.ShapeDtypeStruct((tp, D_MODEL), F32), jax.ShapeDtypeStruct((t - tp, D_MODEL), F32))
    else:
        nbp = None
        out_specs = row(D_MODEL)
        out_shape = jax.ShapeDtypeStruct((t, D_MODEL), F32)
    grid_spec = pltpu.PrefetchScalarGridSpec(
        num_scalar_prefetch=3,
        grid=(t // bt,),
        in_specs=[pl.BlockSpec(memory_space=pl.ANY),
                  row(ROUTER_PAD),
                  row(D_MODEL),
                  pl.BlockSpec((1, 6, D_MODEL), lambda i, e, p, ps: (_mod_row(i, bt, tp, ls), 0, 0)),
                  pl.BlockSpec((1, D_MODEL), lambda i, e, p, ps: (0, 0))],
        out_specs=out_specs,
        scratch_shapes=[pltpu.VMEM((2, TOP_K, bt, D_MODEL // 2), jnp.uint32), pltpu.SemaphoreType.DMA((2,))],
    )
    return pl.pallas_call(
        functools.partial(_finish_kernel, bt=bt, nbp=nbp),
        grid_spec=grid_spec,
        out_shape=out_shape,
        compiler_params=_cp(("arbitrary",), 48),
        name="moe_finish",
    )(e_flat, p_flat, pad_start, yu, rg, x1, mod, g)


def _moe(hu, ri, rg, cnt, x1, mod, g, w_gu, b_gu, w_dn, b_dn, l, tp, ls, split=False):
    t = hu.shape[0]
    n_blocks = -(-t * TOP_K // MOE_BM) + N_EXPERTS
    counts = cnt[0, :N_EXPERTS].astype(jnp.int32)
    pad_start, block_e, n_used = _block_layout(counts, n_blocks)
    e_flat = ri[:, :TOP_K].reshape(-1)
    p_flat = ri[:, TOP_K:2 * TOP_K].reshape(-1)
    xs = _dispatch(e_flat, p_flat, pad_start, hu, n_blocks * MOE_BM)
    act = _gm1(block_e, n_used, xs, w_gu, b_gu, l)
    yu = _gm2(block_e, n_used, act, w_dn, b_dn, l)
    return _finish(e_flat, p_flat, pad_start, yu, rg, x1, mod, g, tp, ls, split)


def kernel(x_prompt, x_sample, cache_attn_k, cache_attn_v, cache_diff_k, cache_diff_v, c, c_ctx, w_mod, b_mod, norm_g, w_in, w_out, hy_conv_w, hy_conv_b, hy_ffn_w1, hy_ffn_b1, hy_ffn_w2, hy_ffn_b2, hy_ffn_w3, hy_sin_freq, hy_skip, sgu_norm_g, sgu_norm_b, sgu_w, sgu_b, attn_q_g, attn_k_g, diff_lambda, diff_norm_g, w_router, b_router, w_gate_up, b_gate_up, w_down, b_down):
    bp, lp, d = x_prompt.shape
    bs, ls, _ = x_sample.shape
    tp = bp * lp
    past = cache_attn_k.shape[2]
    x = jnp.concatenate([x_prompt.reshape(tp, d), x_sample.reshape(bs * ls, d)], axis=0)

    cond = jnp.concatenate([c_ctx[None, :], c, jnp.zeros((COND_ROWS - 1 - bs, d), F32)], axis=0)
    mod = _modulation(cond, w_mod, b_mod).reshape(DEPTH, COND_ROWS, 6, d)

    w_in_b = w_in.astype(BF16)
    w_out_b = w_out.astype(BF16)
    wr_f = jnp.pad(w_router, ((0, 0), (0, 0), (0, ROUTER_PAD - N_EXPERTS)))
    wr_hi = wr_f.astype(BF16)
    wr = jnp.concatenate([wr_hi, (wr_f - wr_hi.astype(F32)).astype(BF16)], axis=-1)
    br = jnp.pad(b_router, ((0, 0), (0, ROUTER_PAD - N_EXPERTS))).reshape(DEPTH, 1, ROUTER_PAD)

    pk = cache_attn_k.reshape(bs, DEPTH, past, -1)
    pv = cache_attn_v.reshape(bs, DEPTH, past, -1)
    pdk = cache_diff_k.reshape(bs, DEPTH, past, -1)
    pdv = cache_diff_v.reshape(bs, DEPTH, past, -1)

    groups = []
    for L in (lp, ls):
        cm, sm, alt8, fwd, inv = _dft_tables(L)
        z, window = _filter_consts(L)
        groups.append((L, cm, sm, alt8, fwd, inv, z, window))
    qw = ATTN_HEADS * ATTN_HEAD_DIM
    kw = ATTN_KV_HEADS * ATTN_HEAD_DIM
    attn_tables = _rope_tables(ls, ATTN_HEAD_DIM, qw) + _rope_tables(ls, ATTN_HEAD_DIM, kw)
    diff_tables = _rope_tables(ls, DIFF_QK_DIM, GROUP_WIDTH)

    new_k, new_v, new_dk, new_dv = [], [], [], []
    for l in range(DEPTH):
        ng = norm_g[l]
        proj = _in_proj(x, mod[l], ng[0:1], w_in_b[l], tp, ls)

        ys = []
        for gi, (B, row_off) in enumerate(((bp, 0), (bs, tp // ls))):
            L, cm, sm, alt8, fwd, inv, z, window = groups[gi]
            kre, kim = _hyena_filter(L, z, window, cm, sm, alt8, hy_ffn_w1[l], hy_ffn_b1[l], hy_ffn_w2[l],
                                     hy_ffn_b2[l], hy_ffn_w3[l], hy_sin_freq[l])
            ya = _hyena(proj, row_off, B, L, hy_conv_w[l], hy_conv_b[l], kre, kim, hy_skip[l], fwd, inv)
            yb = _sgu(proj, row_off, B, L, sgu_norm_g[l], sgu_norm_b[l], sgu_w[l], sgu_b[l])
            if gi == 0:
                yc, k_ctx, v_ctx = _gqa(proj, row_off, B, L, attn_q_g[l], attn_k_g[l], l)
                yd, dk_ctx, dv_ctx = _diff(proj, row_off, B, L, diff_lambda[l], diff_norm_g[l], l)
                new_k.append(k_ctx.reshape(bp, lp, ATTN_KV_HEADS, ATTN_HEAD_DIM))
                new_v.append(v_ctx.reshape(bp, lp, ATTN_KV_HEADS, ATTN_HEAD_DIM))
                new_dk.append(dk_ctx.reshape(bp, lp, DIFF_HEADS, 2, DIFF_QK_DIM))
                new_dv.append(dv_ctx.reshape(bp, lp, DIFF_HEADS, DIFF_V_DIM))
            else:
                yc = _gqa(proj, row_off, B, L, attn_q_g[l], attn_k_g[l], l, past=(pk, pv), tables=attn_tables)
                yd = _diff(proj, row_off, B, L, diff_lambda[l], diff_norm_g[l], l, past=(pdk, pdv),
                           tables=diff_tables)
            ys.append((ya, yb, yc, yd))
        x1, hu, ri, rg, cnt = _out_proj(ys[0], ys[1], w_out_b[l], x, mod[l], ng, wr[l], br[l], tp, ls)
        x = _moe(hu, ri, rg, cnt, x1, mod[l], ng[3:4], w_gate_up, b_gate_up, w_down, b_down, l, tp, ls,
                 split=(l == DEPTH - 1))

    y_prompt = x[0].reshape(bp, lp, d)
    y_sample = x[1].reshape(bs, ls, d)
    return (y_prompt, y_sample, jnp.stack(new_k, axis=1), jnp.stack(new_v, axis=1),
            jnp.stack(new_dk, axis=1), jnp.stack(new_dv, axis=1))
```

```python
import functools
import math

import numpy as np
import jax
import jax.numpy as jnp
from jax import lax
from jax.experimental import pallas as pl
from jax.experimental.pallas import tpu as pltpu

F32 = jnp.float32
BF16 = jnp.bfloat16
HIGHEST = lax.Precision.HIGHEST

D_MODEL = 2048
DEPTH = 2
GRID_W = 64
GROUP_WIDTH = 512
EPS = 1e-6
IN_COLS = 5120
HYENA_BANDS = 16
HYENA_POS_DIM = 1 + 2 * HYENA_BANDS
HYENA_FFN = 64
HYENA_FAST_DECAY = 0.3
HYENA_SLOW_DECAY = 1.5
HYENA_TARGET = 0.01
POS_PAD = 128
CHUNK = 128
SGU_HEADS = 4
ATTN_HEADS = 4
ATTN_KV_HEADS = 2
ATTN_HEAD_DIM = 128
ROPE_THETA = 10000.0
DIFF_HEADS = 4
DIFF_QK_DIM = 64
DIFF_V_DIM = 128
N_EXPERTS = 32
TOP_K = 4
D_FF = 2048
SWIGLU_LIMIT = 7.0
SWIGLU_ALPHA = 1.702
ROUTER_PAD = 128

IN_BM, IN_BN = 1024, 1024
OUT_BM = 256
MOE_BM = 512
GM1_TN = 1024
GM1_SUB = 256
GM2_TN = 2048
DISPATCH_BT = 512
FIN_BM = 256
FIN_ROWS = 32
FIN_UNROLL = 2
COND_ROWS = 16
MOD_TN = 1024
HY_CC = 256
Q_CHUNK = 512
MIB = 2 ** 20


def _cp(sem, vmem_mib):
    return pltpu.CompilerParams(dimension_semantics=sem, vmem_limit_bytes=vmem_mib * MIB)


def _mod_kernel(c_ref, w_ref, b_ref, o_ref):
    c = c_ref[...]
    s = c * jax.nn.sigmoid(c)
    o_ref[...] = jnp.dot(s.astype(BF16), w_ref[...].astype(BF16), preferred_element_type=F32) + b_ref[...]


def _modulation(cond, w_mod, b_mod):
    n = w_mod.shape[-1]
    return pl.pallas_call(
        _mod_kernel,
        grid=(DEPTH, n // MOD_TN),
        in_specs=[
            pl.BlockSpec((COND_ROWS, D_MODEL), lambda l, j: (0, 0)),
            pl.BlockSpec((None, D_MODEL, MOD_TN), lambda l, j: (l, 0, j)),
            pl.BlockSpec((None, 1, MOD_TN), lambda l, j: (l, 0, j)),
        ],
        out_specs=pl.BlockSpec((None, COND_ROWS, MOD_TN), lambda l, j: (l, 0, j)),
        out_shape=jax.ShapeDtypeStruct((DEPTH, COND_ROWS, n), F32),
        compiler_params=_cp(("arbitrary", "arbitrary"), 40),
        name="modulation",
    )(cond, w_mod, b_mod.reshape(DEPTH, 1, n))


def _mod_row(i, bm, tp, ls):
    nbp = tp // bm
    return jnp.where(i < nbp, 0, 1 + (i - nbp) // (ls // bm))


def _in_proj_kernel(x_ref, mod_ref, g_ref, w_ref, o_ref, h_ref):
    @pl.when(pl.program_id(1) == 0)
    def _():
        x = x_ref[...]
        r = lax.rsqrt(jnp.mean(x * x, axis=-1, keepdims=True) + EPS)
        h = (x * r) * g_ref[...] * (1.0 + mod_ref[0, 1:2, :]) + mod_ref[0, 0:1, :]
        h_ref[...] = h.astype(BF16)

    o_ref[...] = jnp.dot(h_ref[...], w_ref[...], preferred_element_type=F32)


def _in_proj(x, mod, g, w, tp, ls):
    t = x.shape[0]
    bm = min(IN_BM, ls, tp)
    return pl.pallas_call(
        _in_proj_kernel,
        grid=(t // bm, IN_COLS // IN_BN),
        in_specs=[
            pl.BlockSpec((bm, D_MODEL), lambda i, j: (i, 0)),
            pl.BlockSpec((1, 6, D_MODEL), lambda i, j: (_mod_row(i, bm, tp, ls), 0, 0)),
            pl.BlockSpec((1, D_MODEL), lambda i, j: (0, 0)),
            pl.BlockSpec((D_MODEL, IN_BN), lambda i, j: (0, j)),
        ],
        out_specs=pl.BlockSpec((bm, IN_BN), lambda i, j: (i, j)),
        out_shape=jax.ShapeDtypeStruct((t, IN_COLS), F32),
        scratch_shapes=[pltpu.VMEM((bm, D_MODEL), BF16)],
        compiler_params=_cp(("arbitrary", "arbitrary"), 56),
        name="in_proj",
    )(x, mod, g, w)


def _dft_tables(L):
    k = np.arange(L, dtype=np.int64)
    ang = ((k[:, None] * k[None, :]) % (2 * L)).astype(np.float64) * (math.pi / L)
    cm = np.cos(ang)
    sm = np.sin(ang)
    alt = np.where(k % 2 == 0, 1.0, -1.0)
    sf = -sm
    sf[0] = alt
    fwd = np.concatenate([cm, sf], axis=0)
    wk = np.where(k == 0, 1.0, 2.0) / (2 * L)
    si = -sm / L
    si[:, 0] = alt / (2 * L)
    inv = np.concatenate([cm * wk[None, :], si], axis=1)
    alt8 = np.zeros((8, L))
    alt8[0] = alt
    return (jnp.asarray(cm, F32), jnp.asarray(sm, F32), jnp.asarray(alt8, F32),
            jnp.asarray(fwd.astype(BF16)), jnp.asarray(inv.astype(BF16)))


def _filter_consts(L):
    t = np.linspace(0.0, 1.0, L, dtype=np.float32)[:, None]
    bands = np.linspace(1e-4, HYENA_BANDS - 1, HYENA_BANDS, dtype=np.float32)[None, :]
    z = np.concatenate([t, np.cos(2 * math.pi * bands * t), np.sin(2 * math.pi * bands * t)], axis=-1)
    z = np.pad(z, ((0, 0), (0, POS_PAD - HYENA_POS_DIM)))
    max_decay = math.log(HYENA_TARGET) / HYENA_FAST_DECAY
    min_decay = math.log(HYENA_TARGET) / HYENA_SLOW_DECAY
    deltas = np.abs(np.linspace(min_decay, max_decay, GROUP_WIDTH, dtype=np.float32))
    window = np.exp(-t * deltas[None, :])
    return jnp.asarray(z, F32), jnp.asarray(window, F32)


def _filter_kernel(z_ref, w1_ref, b1_ref, w2_ref, b2_ref, w3_ref, fr_ref, win_ref, cm_ref, sm_ref, alt_ref,
                   kre_ref, kim_ref):
    c = GROUP_WIDTH
    h = jnp.dot(z_ref[...], w1_ref[...], precision=HIGHEST, preferred_element_type=F32) + b1_ref[...]
    h = jnp.sin(fr_ref[0:1, :] * h)
    h = jnp.dot(h, w2_ref[...], precision=HIGHEST, preferred_element_type=F32) + b2_ref[...]
    h = jnp.sin(fr_ref[1:2, :] * h)
    hfb = jnp.dot(h, w3_ref[...], precision=HIGHEST, preferred_element_type=F32)
    win = win_ref[...]
    hf = hfb[:, :c] * win
    hb = hfb[:, c:] * win
    row = lax.broadcasted_iota(jnp.int32, hf.shape, 0)
    g = hf + jnp.where(row == 0, 0.0, hb)
    kre = jnp.dot(cm_ref[...], g, precision=HIGHEST, preferred_element_type=F32)
    kim = jnp.dot(sm_ref[...], hb - hf, precision=HIGHEST, preferred_element_type=F32)
    kn = jnp.dot(alt_ref[...], g, precision=HIGHEST, preferred_element_type=F32)[0:1, :]
    kre_ref[...] = kre
    kim_ref[...] = jnp.where(row == 0, kn, kim)


def _hyena_filter(L, z, window, cm, sm, alt8, w1, b1, w2, b2, w3, freq):
    w1p = jnp.pad(w1, ((0, POS_PAD - HYENA_POS_DIM), (0, 0)))
    return pl.pallas_call(
        _filter_kernel,
        out_shape=(jax.ShapeDtypeStruct((L, GROUP_WIDTH), F32), jax.ShapeDtypeStruct((L, GROUP_WIDTH), F32)),
        compiler_params=pltpu.CompilerParams(vmem_limit_bytes=56 * MIB),
        name="hyena_filter",
    )(z, w1p, b1.reshape(1, -1), w2, b2.reshape(1, -1), w3, freq, window, cm, sm, alt8)


def _short_conv(a, w_ref, b_ref, row, L):
    prev = jnp.where(row == 0, 0.0, pltpu.roll(a, 1, 0))
    nxt = jnp.where(row == L - 1, 0.0, pltpu.roll(a, L - 1, 0))
    return b_ref[...] + prev * w_ref[0:1, :] + a * w_ref[1:2, :] + nxt * w_ref[2:3, :]


def _hyena_kernel(a0_ref, a1_ref, a2_ref, w0_ref, w1_ref, w2_ref, b0_ref, b1_ref, b2_ref, kre_ref, kim_ref,
                  skip_ref, fwd_ref, inv_ref, o_ref, *, L):
    row = lax.broadcasted_iota(jnp.int32, (L, HY_CC), 0)
    x1 = _short_conv(a1_ref[...], w1_ref, b1_ref, row, L)
    hv = _short_conv(a2_ref[...], w2_ref, b2_ref, row, L)
    u = hv * x1
    spec = jnp.dot(fwd_ref[...], u.astype(BF16), preferred_element_type=F32)
    ure, uim = spec[:L], spec[L:]
    kre, kim = kre_ref[...], kim_ref[...]
    first = row == 0
    yre = ure * kre - jnp.where(first, 0.0, uim * kim)
    yim = jnp.where(first, uim * kim, ure * kim + uim * kre)
    yspec = jnp.concatenate([yre, yim], axis=0).astype(BF16)
    y = jnp.dot(inv_ref[...], yspec, preferred_element_type=F32)
    x0 = _short_conv(a0_ref[...], w0_ref, b0_ref, row, L)
    o_ref[...] = (x0 * (y + u * skip_ref[...])).astype(BF16)


def _hyena(proj, row_off, B, L, conv_w, conv_b, kre, kim, skip, fwd, inv):
    ncc = GROUP_WIDTH // HY_CC

    def a_spec(part):
        return pl.BlockSpec((L, HY_CC), lambda b, cc: (row_off + b, part * ncc + cc))

    def w_spec(part, rows):
        return pl.BlockSpec((rows, HY_CC), lambda b, cc: (0, part * ncc + cc))

    chan = pl.BlockSpec((L, HY_CC), lambda b, cc: (0, cc))
    conv_b = conv_b.reshape(1, -1)
    return pl.pallas_call(
        functools.partial(_hyena_kernel, L=L),
        grid=(B, ncc),
        in_specs=[a_spec(0), a_spec(1), a_spec(2), w_spec(0, 3), w_spec(1, 3), w_spec(2, 3),
                  w_spec(0, 1), w_spec(1, 1), w_spec(2, 1), chan, chan,
                  pl.BlockSpec((1, HY_CC), lambda b, cc: (0, cc)),
                  pl.BlockSpec((2 * L, L), lambda b, cc: (0, 0)),
                  pl.BlockSpec((L, 2 * L), lambda b, cc: (0, 0))],
        out_specs=pl.BlockSpec((L, HY_CC), lambda b, cc: (b, cc)),
        out_shape=jax.ShapeDtypeStruct((B * L, GROUP_WIDTH), BF16),
        compiler_params=_cp(("arbitrary", "arbitrary"), 56),
        name="hyena_conv",
    )(proj, proj, proj, conv_w, conv_w, conv_w, conv_b, conv_b, conv_b, kre, kim, skip.reshape(1, -1), fwd, inv)


def _sgu_kernel(u_ref, v_ref, g_ref, b_ref, w_ref, bs_ref, o_ref, *, L):
    v = v_ref[...]
    mu = jnp.mean(v, axis=-1, keepdims=True)
    xc = v - mu
    vn = xc * lax.rsqrt(jnp.mean(xc * xc, axis=-1, keepdims=True) + EPS) * g_ref[...] + b_ref[...]
    vnb = vn.astype(BF16)
    hc = GROUP_WIDTH // SGU_HEADS
    for h in range(SGU_HEADS):
        wh = w_ref[h].astype(BF16)
        bias = bs_ref[h]
        cols = slice(h * hc, (h + 1) * hc)
        for n in range(L // CHUNK):
            rows = slice(n * CHUNK, (n + 1) * CHUNK)
            mixed = jnp.dot(wh, vnb[rows, cols], preferred_element_type=F32) + bias
            o_ref[rows, cols] = (u_ref[rows, cols] * mixed).astype(BF16)


def _sgu(proj, row_off, B, L, g, b, w_s, b_s):
    hc = GROUP_WIDTH // SGU_HEADS
    bias = jnp.broadcast_to(b_s[:, :, None], (SGU_HEADS, CHUNK, hc))
    return pl.pallas_call(
        functools.partial(_sgu_kernel, L=L),
        grid=(B,),
        in_specs=[pl.BlockSpec((L, GROUP_WIDTH), lambda i: (row_off + i, 3)),
                  pl.BlockSpec((L, GROUP_WIDTH), lambda i: (row_off + i, 4)),
                  pl.BlockSpec((1, GROUP_WIDTH), lambda i: (0, 0)),
                  pl.BlockSpec((1, GROUP_WIDTH), lambda i: (0, 0)),
                  pl.BlockSpec((SGU_HEADS, CHUNK, CHUNK), lambda i: (0, 0, 0)),
                  pl.BlockSpec((SGU_HEADS, CHUNK, hc), lambda i: (0, 0, 0))],
        out_specs=pl.BlockSpec((L, GROUP_WIDTH), lambda i: (i, 0)),
        out_shape=jax.ShapeDtypeStruct((B * L, GROUP_WIDTH), BF16),
        compiler_params=_cp(("arbitrary",), 40),
        name="sgu",
    )(proj, proj, g.reshape(1, -1), b.reshape(1, -1), w_s, bias)


def _rope_tables(L, dh, width):
    quarter = dh // 4
    pos = np.arange(L)
    rowp = (pos // GRID_W).astype(np.float32)
    colp = (pos % GRID_W).astype(np.float32)
    inv_freq = (ROPE_THETA ** (-np.arange(quarter, dtype=np.float32) / quarter)).astype(np.float32)
    ar = (rowp[:, None] * inv_freq[None, :]).astype(np.float64)
    ac = (colp[:, None] * inv_freq[None, :]).astype(np.float64)
    cos = np.concatenate([np.cos(ar), np.cos(ar), np.cos(ac), np.cos(ac)], axis=-1)
    sin = np.concatenate([-np.sin(ar), np.sin(ar), -np.sin(ac), np.sin(ac)], axis=-1)
    reps = width // dh
    return jnp.asarray(np.tile(cos, (1, reps)), F32), jnp.asarray(np.tile(sin, (1, reps)), F32)


def _rope(x, cos, sin, quarter):
    w = x.shape[-1]
    lane = lax.broadcasted_iota(jnp.int32, x.shape, 1)
    partner = jnp.where((lane & quarter) != 0, pltpu.roll(x, quarter, 1), pltpu.roll(x, w - quarter, 1))
    return x * cos + partner * sin


def _head_rms(x, g, heads, dh):
    outs = []
    for h in range(heads):
        xh = x[:, h * dh:(h + 1) * dh]
        outs.append(xh * lax.rsqrt(jnp.mean(xh * xh, axis=-1, keepdims=True) + EPS) * g)
    return jnp.concatenate(outs, axis=-1)


def _softmax_pv(q, k, v):
    s = lax.dot_general(q, k, (((1,), (1,)), ((), ())), preferred_element_type=F32)
    e = jnp.exp(s - jnp.max(s, axis=-1, keepdims=True))
    inv_l = 1.0 / jnp.sum(e, axis=-1, keepdims=True)
    return jnp.dot(e.astype(BF16), v, preferred_element_type=F32), inv_l


def _gqa_body(q, k_ref, v_ref, o_ref, L):
    dh = ATTN_HEAD_DIM
    group = ATTN_HEADS // ATTN_KV_HEADS
    qs = (q * dh ** -0.5).astype(BF16)
    qc = min(Q_CHUNK, L)
    for h in range(ATTN_HEADS):
        kv = h // group
        kh = k_ref[:, kv * dh:(kv + 1) * dh]
        vh = v_ref[:, kv * dh:(kv + 1) * dh]
        for c in range(L // qc):
            rows = slice(c * qc, (c + 1) * qc)
            pv, inv_l = _softmax_pv(qs[rows, h * dh:(h + 1) * dh], kh, vh)
            o_ref[rows, h * dh:(h + 1) * dh] = (pv * inv_l).astype(BF16)


def _gqa_ctx_kernel(q_ref, k_ref, v_ref, gq_ref, gk_ref, o_ref, ko_ref, vo_ref, kb_ref, vb_ref, *, L):
    q = _head_rms(q_ref[...], gq_ref[...], ATTN_HEADS, ATTN_HEAD_DIM)
    k = _head_rms(k_ref[...], gk_ref[...], ATTN_KV_HEADS, ATTN_HEAD_DIM)
    v = v_ref[...]
    ko_ref[...] = k
    vo_ref[...] = v
    kb_ref[...] = k.astype(BF16)
    vb_ref[...] = v.astype(BF16)
    _gqa_body(q, kb_ref, vb_ref, o_ref, L)


def _gqa_lat_kernel(q_ref, k_ref, v_ref, gq_ref, gk_ref, cq_ref, sq_ref, ck_ref, sk_ref, pk_ref, pv_ref, o_ref,
                    kb_ref, vb_ref, *, L, P):
    quarter = ATTN_HEAD_DIM // 4
    q = _rope(_head_rms(q_ref[...], gq_ref[...], ATTN_HEADS, ATTN_HEAD_DIM), cq_ref[...], sq_ref[...], quarter)
    k = _rope(_head_rms(k_ref[...], gk_ref[...], ATTN_KV_HEADS, ATTN_HEAD_DIM), ck_ref[...], sk_ref[...], quarter)
    kb_ref[0:P, :] = pk_ref[...].astype(BF16)
    kb_ref[P:P + L, :] = k.astype(BF16)
    vb_ref[0:P, :] = pv_ref[...].astype(BF16)
    vb_ref[P:P + L, :] = v_ref[...].astype(BF16)
    _gqa_body(q, kb_ref, vb_ref, o_ref, L)


def _gqa(proj, row_off, B, L, gq, gk, l, past=None, tables=None):
    qw = ATTN_HEADS * ATTN_HEAD_DIM
    kw = ATTN_KV_HEADS * ATTN_HEAD_DIM
    qoff = 5
    koff = (2560 + qw) // kw
    in_specs = [pl.BlockSpec((L, qw), lambda i: (row_off + i, qoff)),
                pl.BlockSpec((L, kw), lambda i: (row_off + i, koff)),
                pl.BlockSpec((L, kw), lambda i: (row_off + i, koff + 1)),
                pl.BlockSpec((1, ATTN_HEAD_DIM), lambda i: (0, 0)),
                pl.BlockSpec((1, ATTN_HEAD_DIM), lambda i: (0, 0))]
    args = [proj, proj, proj, gq.reshape(1, -1), gk.reshape(1, -1)]
    y_spec = pl.BlockSpec((L, qw), lambda i: (i, 0))
    y_shape = jax.ShapeDtypeStruct((B * L, qw), BF16)
    if past is None:
        return pl.pallas_call(
            functools.partial(_gqa_ctx_kernel, L=L),
            grid=(B,),
            in_specs=in_specs,
            out_specs=(y_spec, pl.BlockSpec((L, kw), lambda i: (i, 0)), pl.BlockSpec((L, kw), lambda i: (i, 0))),
            out_shape=(y_shape, jax.ShapeDtypeStruct((B * L, kw), F32), jax.ShapeDtypeStruct((B * L, kw), F32)),
            scratch_shapes=[pltpu.VMEM((L, kw), BF16), pltpu.VMEM((L, kw), BF16)],
            compiler_params=_cp(("arbitrary",), 40),
            name="gqa_ctx",
        )(*args)
    pk, pv = past
    P = pk.shape[2]
    cq, sq, ck, sk = tables
    const = lambda w: pl.BlockSpec((L, w), lambda i: (0, 0))
    cache = pl.BlockSpec((None, None, P, kw), lambda i: (i, l, 0, 0))
    return pl.pallas_call(
        functools.partial(_gqa_lat_kernel, L=L, P=P),
        grid=(B,),
        in_specs=in_specs + [const(qw), const(qw), const(kw), const(kw), cache, cache],
        out_specs=y_spec,
        out_shape=y_shape,
        scratch_shapes=[pltpu.VMEM((P + L, kw), BF16), pltpu.VMEM((P + L, kw), BF16)],
        compiler_params=_cp(("arbitrary",), 56),
        name="gqa_lat",
    )(*args, cq, sq, ck, sk, pk, pv)


def _diff_body(dq, k_ref, v_ref, dl_ref, g_ref, o_ref, L, lambda_init):
    dv = DIFF_V_DIM
    dl = dl_ref[...]
    lam = (jnp.exp(jnp.sum(dl[0:1] * dl[1:2], axis=-1, keepdims=True))
           - jnp.exp(jnp.sum(dl[2:3] * dl[3:4], axis=-1, keepdims=True)) + lambda_init)
    qs = dq * DIFF_QK_DIM ** -0.5
    first_map = lax.broadcasted_iota(jnp.int32, (1, dv), 1) < DIFF_QK_DIM
    qc = min(Q_CHUNK, L)
    for h in range(DIFF_HEADS):
        cols = slice(h * dv, (h + 1) * dv)
        kh = k_ref[:, cols]
        vh = v_ref[:, cols]
        for c in range(L // qc):
            rows = slice(c * qc, (c + 1) * qc)
            qh = qs[rows, cols]
            pv0, inv0 = _softmax_pv(jnp.where(first_map, qh, 0.0).astype(BF16), kh, vh)
            pv1, inv1 = _softmax_pv(jnp.where(first_map, 0.0, qh).astype(BF16), kh, vh)
            o = pv0 * inv0 - pv1 * (lam * inv1)
            r = lax.rsqrt(jnp.mean(o * o, axis=-1, keepdims=True) + EPS)
            o_ref[rows, cols] = ((o * r) * g_ref[...] * (1.0 - lambda_init)).astype(BF16)


def _diff_ctx_kernel(q_ref, k_ref, v_ref, dl_ref, g_ref, o_ref, ko_ref, vo_ref, kb_ref, vb_ref, *, L, lambda_init):
    k = k_ref[...]
    v = v_ref[...]
    ko_ref[...] = k
    vo_ref[...] = v
    kb_ref[...] = k.astype(BF16)
    vb_ref[...] = v.astype(BF16)
    _diff_body(q_ref[...], kb_ref, vb_ref, dl_ref, g_ref, o_ref, L, lambda_init)


def _diff_lat_kernel(q_ref, k_ref, v_ref, dl_ref, g_ref, cos_ref, sin_ref, pk_ref, pv_ref, o_ref, kb_ref, vb_ref,
                     *, L, P, lambda_init):
    quarter = DIFF_QK_DIM // 4
    cos, sin = cos_ref[...], sin_ref[...]
    q = _rope(q_ref[...], cos, sin, quarter)
    k = _rope(k_ref[...], cos, sin, quarter)
    kb_ref[0:P, :] = pk_ref[...].astype(BF16)
    kb_ref[P:P + L, :] = k.astype(BF16)
    vb_ref[0:P, :] = pv_ref[...].astype(BF16)
    vb_ref[P:P + L, :] = v_ref[...].astype(BF16)
    _diff_body(q, kb_ref, vb_ref, dl_ref, g_ref, o_ref, L, lambda_init)


def _diff(proj, row_off, B, L, dl, g, l, past=None, tables=None):
    w = GROUP_WIDTH
    lambda_init = 0.8 - 0.6 * math.exp(-0.3 * l)
    in_specs = [pl.BlockSpec((L, w), lambda i: (row_off + i, 7)),
                pl.BlockSpec((L, w), lambda i: (row_off + i, 8)),
                pl.BlockSpec((L, w), lambda i: (row_off + i, 9)),
                pl.BlockSpec((4, DIFF_QK_DIM), lambda i: (0, 0)),
                pl.BlockSpec((1, DIFF_V_DIM), lambda i: (0, 0))]
    args = [proj, proj, proj, dl, g.reshape(1, -1)]
    y_spec = pl.BlockSpec((L, w), lambda i: (i, 0))
    y_shape = jax.ShapeDtypeStruct((B * L, w), BF16)
    if past is None:
        return pl.pallas_call(
            functools.partial(_diff_ctx_kernel, L=L, lambda_init=lambda_init),
            grid=(B,),
            in_specs=in_specs,
            out_specs=(y_spec, pl.BlockSpec((L, w), lambda i: (i, 0)), pl.BlockSpec((L, w), lambda i: (i, 0))),
            out_shape=(y_shape, jax.ShapeDtypeStruct((B * L, w), F32), jax.ShapeDtypeStruct((B * L, w), F32)),
            scratch_shapes=[pltpu.VMEM((L, w), BF16), pltpu.VMEM((L, w), BF16)],
            compiler_params=_cp(("arbitrary",), 40),
            name="diff_ctx",
        )(*args)
    pk, pv = past
    P = pk.shape[2]
    cos, sin = tables
    const = pl.BlockSpec((L, w), lambda i: (0, 0))
    cache = pl.BlockSpec((None, None, P, w), lambda i: (i, l, 0, 0))
    return pl.pallas_call(
        functools.partial(_diff_lat_kernel, L=L, P=P, lambda_init=lambda_init),
        grid=(B,),
        in_specs=in_specs + [const, const, cache, cache],
        out_specs=y_spec,
        out_shape=y_shape,
        scratch_shapes=[pltpu.VMEM((P + L, w), BF16), pltpu.VMEM((P + L, w), BF16)],
        compiler_params=_cp(("arbitrary",), 56),
        name="diff_lat",
    )(*args, cos, sin, pk, pv)


def _pack_halves(lo, hi):
    lo_bits = pltpu.bitcast(lo.astype(BF16).astype(F32), jnp.uint32) >> 16
    hi_bits = pltpu.bitcast(hi.astype(BF16).astype(F32), jnp.uint32) & jnp.uint32(0xFFFF0000)
    return lo_bits | hi_bits


def _unpack_halves(words):
    lo = pltpu.bitcast(words << 16, F32)
    hi = pltpu.bitcast(words & jnp.uint32(0xFFFF0000), F32)
    return lo, hi


def _route_rows(logits, prefix_base, tri_ref):
    lane = lax.broadcasted_iota(jnp.int32, logits.shape, 1)
    neg = jnp.float32(-jnp.inf)
    work = jnp.where(lane < N_EXPERTS, logits, neg)
    sels, vals, idxs = [], [], []
    for _ in range(TOP_K):
        mk = jnp.max(work, axis=-1, keepdims=True)
        ik = jnp.min(jnp.where(work == mk, lane, ROUTER_PAD), axis=-1, keepdims=True)
        sel = lane == ik
        work = jnp.where(sel, neg, work)
        sels.append(sel)
        vals.append(mk)
        idxs.append(ik)
    es = [jnp.exp(v - vals[0]) for v in vals]
    inv = 1.0 / (es[0] + es[1] + es[2] + es[3])
    member = jnp.zeros(logits.shape, F32)
    for sel in sels:
        member = member + jnp.where(sel, 1.0, 0.0)
    prefix = jnp.dot(tri_ref[...], member.astype(BF16), preferred_element_type=F32) + prefix_base
    ri = jnp.zeros(logits.shape, jnp.int32)
    rg = jnp.zeros(logits.shape, F32)
    for k in range(TOP_K):
        rank = jnp.sum(jnp.where(sels[k], prefix, 0.0), axis=-1, keepdims=True).astype(jnp.int32)
        ri = jnp.where(lane == k, idxs[k], ri)
        ri = jnp.where(lane == TOP_K + k, rank, ri)
        rg = jnp.where(lane == k, es[k] * inv, rg)
    return ri, rg, jnp.sum(member, axis=0, keepdims=True)


def _out_proj_kernel(pa_ref, pb_ref, pc_ref, pd_ref, sa_ref, sb_ref, sc_ref, sd_ref, w_ref, x_ref, mod_ref, ng_ref,
                     wr_ref, br_ref, tri_ref, x1_ref, hu_ref, ri_ref, rg_ref, cnt_ref, cat_ref, carry_ref, *, nbp):
    i = pl.program_id(0)

    @pl.when(i == 0)
    def _():
        carry_ref[...] = jnp.zeros_like(carry_ref)

    gw = GROUP_WIDTH

    @pl.when(i < nbp)
    def _():
        for m_, ref in enumerate((pa_ref, pb_ref, pc_ref, pd_ref)):
            cat_ref[:, m_ * gw:(m_ + 1) * gw] = ref[...]

    @pl.when(i >= nbp)
    def _():
        for m_, ref in enumerate((sa_ref, sb_ref, sc_ref, sd_ref)):
            cat_ref[:, m_ * gw:(m_ + 1) * gw] = ref[...]

    m = jnp.dot(cat_ref[...], w_ref[...], preferred_element_type=F32)
    mn = m * lax.rsqrt(jnp.mean(m * m, axis=-1, keepdims=True) + EPS) * ng_ref[1:2, :]
    x1 = x_ref[...] + mod_ref[0, 2:3, :] * mn
    x1_ref[...] = x1
    hn = x1 * lax.rsqrt(jnp.mean(x1 * x1, axis=-1, keepdims=True) + EPS) * ng_ref[2:3, :]
    h = hn * (1.0 + mod_ref[0, 4:5, :]) + mod_ref[0, 3:4, :]
    half = D_MODEL // 2
    hu_ref[...] = _pack_halves(h[:, :half], h[:, half:])
    hh = h.astype(BF16)
    hl = (h - hh.astype(F32)).astype(BF16)
    both = jnp.dot(hh, wr_ref[...], preferred_element_type=F32)
    low = jnp.dot(hl, wr_ref[:, :ROUTER_PAD], preferred_element_type=F32)
    logits = both[:, :ROUTER_PAD] + both[:, ROUTER_PAD:] + low + br_ref[...]
    ri, rg, counts = _route_rows(logits, carry_ref[0:1, :], tri_ref)
    ri_ref[...] = ri
    rg_ref[...] = rg
    carry_ref[...] = carry_ref[...] + counts
    cnt_ref[...] = carry_ref[...]


def _out_proj(ys_prompt, ys_latent, w, x, mod, ng, wr, br, tp, ls):
    t = x.shape[0]
    bm = OUT_BM
    nbp = tp // bm
    pspec = pl.BlockSpec((bm, GROUP_WIDTH), lambda i: (jnp.minimum(i, nbp - 1), 0))
    sspec = pl.BlockSpec((bm, GROUP_WIDTH), lambda i: (jnp.maximum(i - nbp, 0), 0))
    row = pl.BlockSpec((bm, D_MODEL), lambda i: (i, 0))
    info = pl.BlockSpec((bm, ROUTER_PAD), lambda i: (i, 0))
    tri = jnp.asarray(np.tril(np.ones((bm, bm), np.float32), -1).astype(BF16))
    return pl.pallas_call(
        functools.partial(_out_proj_kernel, nbp=nbp),
        grid=(t // bm,),
        in_specs=[pspec, pspec, pspec, pspec, sspec, sspec, sspec, sspec,
                  pl.BlockSpec((D_MODEL, D_MODEL), lambda i: (0, 0)),
                  row,
                  pl.BlockSpec((1, 6, D_MODEL), lambda i: (_mod_row(i, bm, tp, ls), 0, 0)),
                  pl.BlockSpec((4, D_MODEL), lambda i: (0, 0)),
                  pl.BlockSpec((D_MODEL, 2 * ROUTER_PAD), lambda i: (0, 0)),
                  pl.BlockSpec((1, ROUTER_PAD), lambda i: (0, 0)),
                  pl.BlockSpec((bm, bm), lambda i: (0, 0))],
        out_specs=(row, pl.BlockSpec((bm, D_MODEL // 2), lambda i: (i, 0)), info, info,
                   pl.BlockSpec((8, ROUTER_PAD), lambda i: (0, 0))),
        out_shape=(jax.ShapeDtypeStruct((t, D_MODEL), F32), jax.ShapeDtypeStruct((t, D_MODEL // 2), jnp.uint32),
                   jax.ShapeDtypeStruct((t, ROUTER_PAD), jnp.int32), jax.ShapeDtypeStruct((t, ROUTER_PAD), F32),
                   jax.ShapeDtypeStruct((8, ROUTER_PAD), F32)),
        scratch_shapes=[pltpu.VMEM((bm, D_MODEL), BF16), pltpu.VMEM((8, ROUTER_PAD), F32)],
        compiler_params=_cp(("arbitrary",), 56),
        name="out_proj",
    )(*ys_prompt, *ys_latent, w, x, mod, ng, wr, br, tri)


def _new_expert(be_ref, i):
    return jnp.logical_or(i == 0, be_ref[jnp.maximum(i - 1, 0)] != be_ref[i])


def _gm1_kernel(be_ref, nu_ref, hf_ref, x_ref, wg_ref, wu_ref, bg_ref, bu_ref, o_ref, wbf_ref, xb_ref):
    i = pl.program_id(1)
    used = i < nu_ref[0]
    tn = GM1_TN
    half = D_MODEL // 2

    @pl.when(jnp.logical_and(used, _new_expert(be_ref, i)))
    def _():
        wbf_ref[:, :tn] = wg_ref[...].astype(BF16)
        wbf_ref[:, tn:] = wu_ref[...].astype(BF16)

    def compute(rows):
        lo, hi = _unpack_halves(x_ref[:rows, :])
        xb_ref[:rows, :half] = lo.astype(BF16)
        xb_ref[:rows, half:] = hi.astype(BF16)
        for c in range(tn // GM1_SUB):
            cols = slice(c * GM1_SUB, (c + 1) * GM1_SUB)
            ucols = slice(tn + c * GM1_SUB, tn + (c + 1) * GM1_SUB)
            g = jnp.dot(xb_ref[:rows, :], wbf_ref[:, cols], preferred_element_type=F32)
            u = jnp.dot(xb_ref[:rows, :], wbf_ref[:, ucols], preferred_element_type=F32)
            gate = jnp.minimum(g + bg_ref[:, cols], SWIGLU_LIMIT)
            up = jnp.clip(u + bu_ref[:, cols], -SWIGLU_LIMIT, SWIGLU_LIMIT)
            glu = gate * jax.nn.sigmoid(gate * SWIGLU_ALPHA)
            o_ref[:rows, cols] = ((up + 1.0) * glu).astype(BF16)
        if rows < MOE_BM:
            o_ref[rows:, :] = jnp.zeros((MOE_BM - rows, tn), BF16)

    small = hf_ref[i] != 0

    @pl.when(jnp.logical_and(used, jnp.logical_not(small)))
    def _():
        compute(MOE_BM)

    @pl.when(jnp.logical_and(used, small))
    def _():
        compute(MOE_BM // 2)

    @pl.when(jnp.logical_not(used))
    def _():
        o_ref[...] = jnp.zeros_like(o_ref)


def _gm1(block_e, n_used, block_half, xs, w_gu, b_gu, l):
    rows = xs.shape[0]
    nb = rows // MOE_BM
    nj = D_FF // GM1_TN
    b_gu = b_gu.reshape(DEPTH, N_EXPERTS, 1, 2 * D_FF)
    last = lambda i, nu: jnp.minimum(i, nu[0] - 1)
    grid_spec = pltpu.PrefetchScalarGridSpec(
        num_scalar_prefetch=3,
        grid=(nj, nb),
        in_specs=[
            pl.BlockSpec((MOE_BM, D_MODEL // 2), lambda j, i, be, nu, hf: (last(i, nu), 0)),
            pl.BlockSpec((None, None, D_MODEL, GM1_TN), lambda j, i, be, nu, hf: (l, be[i], 0, j)),
            pl.BlockSpec((None, None, D_MODEL, GM1_TN), lambda j, i, be, nu, hf: (l, be[i], 0, nj + j)),
            pl.BlockSpec((None, None, 1, GM1_TN), lambda j, i, be, nu, hf: (l, be[i], 0, j)),
            pl.BlockSpec((None, None, 1, GM1_TN), lambda j, i, be, nu, hf: (l, be[i], 0, nj + j)),
        ],
        out_specs=pl.BlockSpec((MOE_BM, GM1_TN), lambda j, i, be, nu, hf: (i, j)),
        scratch_shapes=[pltpu.VMEM((D_MODEL, 2 * GM1_TN), BF16), pltpu.VMEM((MOE_BM, D_MODEL), BF16)],
    )
    return pl.pallas_call(
        _gm1_kernel,
        grid_spec=grid_spec,
        out_shape=jax.ShapeDtypeStruct((rows, D_FF), BF16),
        compiler_params=_cp(("arbitrary", "arbitrary"), 60),
        name="moe_gate_up",
    )(block_e, n_used, block_half, xs, w_gu, w_gu, b_gu, b_gu)


def _gm2_kernel(be_ref, nu_ref, hf_ref, a_ref, w_ref, b_ref, o_ref, wbf_ref):
    i = pl.program_id(1)
    used = i < nu_ref[0]
    half = GM2_TN // 2

    @pl.when(jnp.logical_and(used, _new_expert(be_ref, i)))
    def _():
        wbf_ref[...] = w_ref[...].astype(BF16)

    def compute(rows):
        lo = jnp.dot(a_ref[:rows, :], wbf_ref[:, :half], preferred_element_type=F32) + b_ref[:, :half]
        hi = jnp.dot(a_ref[:rows, :], wbf_ref[:, half:], preferred_element_type=F32) + b_ref[:, half:]
        o_ref[:rows, :] = _pack_halves(lo, hi)
        if rows < MOE_BM:
            o_ref[rows:, :] = jnp.zeros((MOE_BM - rows, half), jnp.uint32)

    small = hf_ref[i] != 0

    @pl.when(jnp.logical_and(used, jnp.logical_not(small)))
    def _():
        compute(MOE_BM)

    @pl.when(jnp.logical_and(used, small))
    def _():
        compute(MOE_BM // 2)

    @pl.when(jnp.logical_not(used))
    def _():
        o_ref[...] = jnp.zeros_like(o_ref)


def _gm2(block_e, n_used, block_half, act, w_dn, b_dn, l):
    rows = act.shape[0]
    nb = rows // MOE_BM
    nj = D_MODEL // GM2_TN
    b_dn = b_dn.reshape(DEPTH, N_EXPERTS, 1, D_MODEL)
    last = lambda i, nu: jnp.minimum(i, nu[0] - 1)
    grid_spec = pltpu.PrefetchScalarGridSpec(
        num_scalar_prefetch=3,
        grid=(nj, nb),
        in_specs=[
            pl.BlockSpec((MOE_BM, D_FF), lambda j, i, be, nu, hf: (last(i, nu), 0)),
            pl.BlockSpec((None, None, D_FF, GM2_TN), lambda j, i, be, nu, hf: (l, be[i], 0, j)),
            pl.BlockSpec((None, None, 1, GM2_TN), lambda j, i, be, nu, hf: (l, be[i], 0, j)),
        ],
        out_specs=pl.BlockSpec((MOE_BM, GM2_TN // 2), lambda j, i, be, nu, hf: (i, j)),
        scratch_shapes=[pltpu.VMEM((D_FF, GM2_TN), BF16)],
    )
    return pl.pallas_call(
        _gm2_kernel,
        grid_spec=grid_spec,
        out_shape=jax.ShapeDtypeStruct((rows, D_MODEL // 2), jnp.uint32),
        compiler_params=_cp(("arbitrary", "arbitrary"), 60),
        name="moe_down",
    )(block_e, n_used, block_half, act, w_dn, b_dn)


def _block_layout(counts, n_blocks):
    padded = (counts + MOE_BM - 1) // MOE_BM * MOE_BM
    pad_end = jnp.cumsum(padded)
    pad_start = (pad_end - padded).astype(jnp.int32)
    starts = jnp.arange(n_blocks, dtype=jnp.int32) * MOE_BM
    block_e = jnp.minimum(jnp.sum(pad_end[None, :] <= starts[:, None], axis=1), N_EXPERTS - 1).astype(jnp.int32)
    n_used = (pad_end[-1] // MOE_BM).astype(jnp.int32).reshape(1)
    valid = jnp.clip(counts[block_e] - (starts - pad_start[block_e]), 0, MOE_BM)
    block_half = (valid <= MOE_BM // 2).astype(jnp.int32)
    return pad_start, block_e, n_used, block_half


def _dispatch_kernel(dest_ref, ps_ref, cn_ref, h_ref, xs_ref, zero_ref, sem, zsem, *, bt):
    i = pl.program_id(0)
    base = i * bt

    def zero_rows(issue):
        def per_expert(e, carry):
            start = ps_ref[e] + cn_ref[e]
            end = (start + MOE_BM - 1) // MOE_BM * MOE_BM
            head_end = jnp.minimum((start + 7) // 8 * 8, end)

            def single(r, c):
                cp = pltpu.make_async_copy(zero_ref.at[pl.ds(0, 1)], xs_ref.at[pl.ds(r, 1)], zsem)
                cp.start() if issue else cp.wait()
                return c

            def chunk(q, c):
                r = pl.multiple_of(head_end + q * 8, 8)
                cp = pltpu.make_async_copy(zero_ref.at[pl.ds(0, 8)], xs_ref.at[pl.ds(r, 8)], zsem)
                cp.start() if issue else cp.wait()
                return c

            lax.fori_loop(start, head_end, single, 0)
            lax.fori_loop(0, (end - head_end) // 8, chunk, 0)
            return carry

        lax.fori_loop(0, N_EXPERTS, per_expert, 0)

        last = N_EXPERTS - 1
        used_rows = ps_ref[last] + (cn_ref[last] + MOE_BM - 1) // MOE_BM * MOE_BM

        def block(b, c):
            r = pl.multiple_of(b * MOE_BM, MOE_BM)
            cp = pltpu.make_async_copy(zero_ref, xs_ref.at[pl.ds(r, MOE_BM)], zsem)
            cp.start() if issue else cp.wait()
            return c

        lax.fori_loop(used_rows // MOE_BM, xs_ref.shape[0] // MOE_BM, block, 0)

    @pl.when(i == 0)
    def _():
        zero_ref[...] = jnp.zeros_like(zero_ref)
        zero_rows(True)

    def group(g, carry):
        r0 = pl.multiple_of(g * 8, 8)
        rows8 = h_ref.at[pl.ds(r0, 8)]
        for r in range(8):
            for k in range(TOP_K):
                dst = dest_ref[(base + r0 + r) * TOP_K + k]
                pltpu.make_async_copy(rows8.at[pl.ds(r, 1)], xs_ref.at[pl.ds(dst, 1)], sem).start(priority=k % 2)
        return carry

    lax.fori_loop(0, bt // 8, group, 0)
    for _ in range(TOP_K):
        pltpu.make_async_copy(h_ref, xs_ref.at[pl.ds(0, bt)], sem).wait()

    @pl.when(i == pl.num_programs(0) - 1)
    def _():
        zero_rows(False)


def _dispatch(dest, pad_start, counts, hu, rows):
    t, w = hu.shape
    bt = DISPATCH_BT
    grid_spec = pltpu.PrefetchScalarGridSpec(
        num_scalar_prefetch=3,
        grid=(t // bt,),
        in_specs=[pl.BlockSpec((bt, w), lambda i, dst, ps, cn: (i, 0))],
        out_specs=pl.BlockSpec(memory_space=pl.ANY),
        scratch_shapes=[pltpu.VMEM((MOE_BM, w), jnp.uint32), pltpu.SemaphoreType.DMA(()),
                        pltpu.SemaphoreType.DMA(())],
    )
    return pl.pallas_call(
        functools.partial(_dispatch_kernel, bt=bt),
        grid_spec=grid_spec,
        out_shape=jax.ShapeDtypeStruct((rows, w), jnp.uint32),
        compiler_params=pltpu.CompilerParams(dimension_semantics=("arbitrary",), has_side_effects=True),
        name="moe_dispatch",
    )(dest, pad_start, counts, hu)


def _finish_kernel(dest_ref, y_ref, rg_ref, x_ref, mod_ref, g_ref, *rest, bt, nbp):
    if nbp is None:
        o_ref, buf, sem = rest
    else:
        op_ref, os_ref, buf, sem = rest
    i = pl.program_id(0)
    n = pl.num_programs(0)
    slot = i % 2
    q = GM2_TN // 2
    nj = D_MODEL // GM2_TN

    def issue8(blk, to_slot, r0):
        for k in range(TOP_K):
            rows8 = buf.at[to_slot, k, pl.ds(r0, 8)]
            for r in range(8):
                src = dest_ref[(blk * bt + r0 + r) * TOP_K + k]
                pltpu.make_async_copy(y_ref.at[pl.ds(src, 1)], rows8.at[pl.ds(r, 1)], sem.at[to_slot]).start(
                    priority=k % 2)

    def issue_block(blk, to_slot):
        lax.fori_loop(0, bt // 8, lambda g, c: (issue8(blk, to_slot, pl.multiple_of(g * 8, 8)), c)[1], 0)

    def combine_rows(out_ref, r0):
        rows = pl.ds(r0, FIN_ROWS)
        rg = rg_ref[rows, :]
        parts = [None] * (2 * nj)
        for k in range(TOP_K):
            gate = rg[:, k:k + 1]
            for j in range(nj):
                lo, hi = _unpack_halves(buf[slot, k, rows, j * q:(j + 1) * q])
                for idx, val in ((2 * j, lo), (2 * j + 1, hi)):
                    parts[idx] = gate * val if k == 0 else parts[idx] + gate * val
        f = jnp.concatenate(parts, axis=-1)
        fn = f * lax.rsqrt(jnp.mean(f * f, axis=-1, keepdims=True) + EPS) * g_ref[...]
        out_ref[rows, :] = x_ref[rows, :] + mod_ref[0, 5:6, :] * fn

    def combine_all(out_ref):
        def chunk(c, carry):
            combine_rows(out_ref, pl.multiple_of(c * FIN_ROWS, FIN_ROWS))
            return carry

        lax.fori_loop(0, bt // FIN_ROWS, chunk, 0, unroll=FIN_UNROLL)

    @pl.when(i == 0)
    def _():
        issue_block(0, 0)

    @pl.when(i + 1 < n)
    def _():
        issue_block(i + 1, (i + 1) % 2)

    for k in range(TOP_K):
        pltpu.make_async_copy(y_ref.at[pl.ds(0, bt)], buf.at[slot, k], sem.at[slot]).wait()

    if nbp is None:
        combine_all(o_ref)
    else:
        @pl.when(i < nbp)
        def _():
            combine_all(op_ref)

        @pl.when(i >= nbp)
        def _():
            combine_all(os_ref)


def _finish(dest, yu, rg, x1, mod, g, tp, ls, split):
    t = x1.shape[0]
    bt = FIN_BM
    row = lambda w: pl.BlockSpec((bt, w), lambda i, dst: (i, 0))
    if split:
        nbp = tp // bt
        out_specs = (pl.BlockSpec((bt, D_MODEL), lambda i, dst: (jnp.minimum(i, nbp - 1), 0)),
                     pl.BlockSpec((bt, D_MODEL), lambda i, dst: (jnp.maximum(i - nbp, 0), 0)))
        out_shape = (jax.ShapeDtypeStruct((tp, D_MODEL), F32), jax.ShapeDtypeStruct((t - tp, D_MODEL), F32))
    else:
        nbp = None
        out_specs = row(D_MODEL)
        out_shape = jax.ShapeDtypeStruct((t, D_MODEL), F32)
    grid_spec = pltpu.PrefetchScalarGridSpec(
        num_scalar_prefetch=1,
        grid=(t // bt,),
        in_specs=[pl.BlockSpec(memory_space=pl.ANY),
                  row(ROUTER_PAD),
                  row(D_MODEL),
                  pl.BlockSpec((1, 6, D_MODEL), lambda i, dst: (_mod_row(i, bt, tp, ls), 0, 0)),
                  pl.BlockSpec((1, D_MODEL), lambda i, dst: (0, 0))],
        out_specs=out_specs,
        scratch_shapes=[pltpu.VMEM((2, TOP_K, bt, D_MODEL // 2), jnp.uint32), pltpu.SemaphoreType.DMA((2,))],
    )
    return pl.pallas_call(
        functools.partial(_finish_kernel, bt=bt, nbp=nbp),
        grid_spec=grid_spec,
        out_shape=out_shape,
        compiler_params=_cp(("arbitrary",), 48),
        name="moe_finish",
    )(dest, yu, rg, x1, mod, g)


def _moe(hu, ri, rg, cnt, x1, mod, g, w_gu, b_gu, w_dn, b_dn, l, tp, ls, split=False):
    t = hu.shape[0]
    n_blocks = -(-t * TOP_K // MOE_BM) + N_EXPERTS
    counts = cnt[0, :N_EXPERTS].astype(jnp.int32)
    pad_start, block_e, n_used, block_half = _block_layout(counts, n_blocks)
    e_flat = ri[:, :TOP_K].reshape(-1)
    p_flat = ri[:, TOP_K:2 * TOP_K].reshape(-1)
    first = jnp.where(e_flat[:, None] == jnp.arange(N_EXPERTS, dtype=jnp.int32)[None, :], pad_start[None, :], 0)
    dest = jnp.sum(first, axis=1) + p_flat
    xs = _dispatch(dest, pad_start, counts, hu, n_blocks * MOE_BM)
    act = _gm1(block_e, n_used, block_half, xs, w_gu, b_gu, l)
    yu = _gm2(block_e, n_used, block_half, act, w_dn, b_dn, l)
    return _finish(dest, yu, rg, x1, mod, g, tp, ls, split)


def kernel(x_prompt, x_sample, cache_attn_k, cache_attn_v, cache_diff_k, cache_diff_v, c, c_ctx, w_mod, b_mod, norm_g, w_in, w_out, hy_conv_w, hy_conv_b, hy_ffn_w1, hy_ffn_b1, hy_ffn_w2, hy_ffn_b2, hy_ffn_w3, hy_sin_freq, hy_skip, sgu_norm_g, sgu_norm_b, sgu_w, sgu_b, attn_q_g, attn_k_g, diff_lambda, diff_norm_g, w_router, b_router, w_gate_up, b_gate_up, w_down, b_down):
    bp, lp, d = x_prompt.shape
    bs, ls, _ = x_sample.shape
    tp = bp * lp
    past = cache_attn_k.shape[2]
    x = jnp.concatenate([x_prompt.reshape(tp, d), x_sample.reshape(bs * ls, d)], axis=0)

    cond = jnp.concatenate([c_ctx[None, :], c, jnp.zeros((COND_ROWS - 1 - bs, d), F32)], axis=0)
    mod = _modulation(cond, w_mod, b_mod).reshape(DEPTH, COND_ROWS, 6, d)

    w_in_b = w_in.astype(BF16)
    w_out_b = w_out.astype(BF16)
    wr_f = jnp.pad(w_router, ((0, 0), (0, 0), (0, ROUTER_PAD - N_EXPERTS)))
    wr_hi = wr_f.astype(BF16)
    wr = jnp.concatenate([wr_hi, (wr_f - wr_hi.astype(F32)).astype(BF16)], axis=-1)
    br = jnp.pad(b_router, ((0, 0), (0, ROUTER_PAD - N_EXPERTS))).reshape(DEPTH, 1, ROUTER_PAD)

    pk = cache_attn_k.reshape(bs, DEPTH, past, -1)
    pv = cache_attn_v.reshape(bs, DEPTH, past, -1)
    pdk = cache_diff_k.reshape(bs, DEPTH, past, -1)
    pdv = cache_diff_v.reshape(bs, DEPTH, past, -1)

    groups = []
    for L in (lp, ls):
        cm, sm, alt8, fwd, inv = _dft_tables(L)
        z, window = _filter_consts(L)
        groups.append((L, cm, sm, alt8, fwd, inv, z, window))
    qw = ATTN_HEADS * ATTN_HEAD_DIM
    kw = ATTN_KV_HEADS * ATTN_HEAD_DIM
    attn_tables = _rope_tables(ls, ATTN_HEAD_DIM, qw) + _rope_tables(ls, ATTN_HEAD_DIM, kw)
    diff_tables = _rope_tables(ls, DIFF_QK_DIM, GROUP_WIDTH)

    new_k, new_v, new_dk, new_dv = [], [], [], []
    for l in range(DEPTH):
        ng = norm_g[l]
        proj = _in_proj(x, mod[l], ng[0:1], w_in_b[l], tp, ls)

        ys = []
        for gi, (B, row_off) in enumerate(((bp, 0), (bs, tp // ls))):
            L, cm, sm, alt8, fwd, inv, z, window = groups[gi]
            kre, kim = _hyena_filter(L, z, window, cm, sm, alt8, hy_ffn_w1[l], hy_ffn_b1[l], hy_ffn_w2[l],
                                     hy_ffn_b2[l], hy_ffn_w3[l], hy_sin_freq[l])
            ya = _hyena(proj, row_off, B, L, hy_conv_w[l], hy_conv_b[l], kre, kim, hy_skip[l], fwd, inv)
            yb = _sgu(proj, row_off, B, L, sgu_norm_g[l], sgu_norm_b[l], sgu_w[l], sgu_b[l])
            if gi == 0:
                yc, k_ctx, v_ctx = _gqa(proj, row_off, B, L, attn_q_g[l], attn_k_g[l], l)
                yd, dk_ctx, dv_ctx = _diff(proj, row_off, B, L, diff_lambda[l], diff_norm_g[l], l)
                new_k.append(k_ctx.reshape(bp, lp, ATTN_KV_HEADS, ATTN_HEAD_DIM))
                new_v.append(v_ctx.reshape(bp, lp, ATTN_KV_HEADS, ATTN_HEAD_DIM))
                new_dk.append(dk_ctx.reshape(bp, lp, DIFF_HEADS, 2, DIFF_QK_DIM))
                new_dv.append(dv_ctx.reshape(bp, lp, DIFF_HEADS, DIFF_V_DIM))
            else:
                yc = _gqa(proj, row_off, B, L, attn_q_g[l], attn_k_g[l], l, past=(pk, pv), tables=attn_tables)
                yd = _diff(proj, row_off, B, L, diff_lambda[l], diff_norm_g[l], l, past=(pdk, pdv),
                           tables=diff_tables)
            ys.append((ya, yb, yc, yd))
        x1, hu, ri, rg, cnt = _out_proj(ys[0], ys[1], w_out_b[l], x, mod[l], ng, wr[l], br[l], tp, ls)
        x = _moe(hu, ri, rg, cnt, x1, mod[l], ng[3:4], w_gate_up, b_gate_up, w_down, b_down, l, tp, ls,
                 split=(l == DEPTH - 1))

    y_prompt = x[0].reshape(bp, lp, d)
    y_sample = x[1].reshape(bs, ls, d)
    return (y_prompt, y_sample, jnp.stack(new_k, axis=1), jnp.stack(new_v, axis=1),
            jnp.stack(new_dk, axis=1), jnp.stack(new_dv, axis=1))
```

```python
import functools
import math

import numpy as np
import jax
import jax.numpy as jnp
from jax import lax
from jax.experimental import pallas as pl
from jax.experimental.pallas import tpu as pltpu

F32 = jnp.float32
BF16 = jnp.bfloat16
HIGHEST = lax.Precision.HIGHEST

D_MODEL = 2048
DEPTH = 2
GRID_W = 64
GROUP_WIDTH = 512
EPS = 1e-6
IN_COLS = 5120
HYENA_BANDS = 16
HYENA_POS_DIM = 1 + 2 * HYENA_BANDS
HYENA_FFN = 64
HYENA_FAST_DECAY = 0.3
HYENA_SLOW_DECAY = 1.5
HYENA_TARGET = 0.01
POS_PAD = 128
CHUNK = 128
SGU_HEADS = 4
ATTN_HEADS = 4
ATTN_KV_HEADS = 2
ATTN_HEAD_DIM = 128
ROPE_THETA = 10000.0
DIFF_HEADS = 4
DIFF_QK_DIM = 64
DIFF_V_DIM = 128
N_EXPERTS = 32
TOP_K = 4
D_FF = 2048
SWIGLU_LIMIT = 7.0
SWIGLU_ALPHA = 1.702
ROUTER_PAD = 128

IN_BM, IN_BN = 1024, 1024
OUT_BM = 256
MOE_BM = 512
GM1_TN = 1024
GM1_SUB = 256
GM2_TN = 2048
DISPATCH_BT = 512
FIN_BM = 256
FIN_ROWS = 32
FIN_UNROLL = 2
COND_ROWS = 16
MOD_TN = 1024
HY_CC = 256
Q_CHUNK = 512
MIB = 2 ** 20


def _cp(sem, vmem_mib):
    return pltpu.CompilerParams(dimension_semantics=sem, vmem_limit_bytes=vmem_mib * MIB)


def _mod_kernel(c_ref, w_ref, b_ref, o_ref):
    c = c_ref[...]
    s = c * jax.nn.sigmoid(c)
    o_ref[...] = jnp.dot(s.astype(BF16), w_ref[...].astype(BF16), preferred_element_type=F32) + b_ref[...]


def _modulation(cond, w_mod, b_mod):
    n = w_mod.shape[-1]
    return pl.pallas_call(
        _mod_kernel,
        grid=(DEPTH, n // MOD_TN),
        in_specs=[
            pl.BlockSpec((COND_ROWS, D_MODEL), lambda l, j: (0, 0)),
            pl.BlockSpec((None, D_MODEL, MOD_TN), lambda l, j: (l, 0, j)),
            pl.BlockSpec((None, 1, MOD_TN), lambda l, j: (l, 0, j)),
        ],
        out_specs=pl.BlockSpec((None, COND_ROWS, MOD_TN), lambda l, j: (l, 0, j)),
        out_shape=jax.ShapeDtypeStruct((DEPTH, COND_ROWS, n), F32),
        compiler_params=_cp(("arbitrary", "arbitrary"), 40),
        name="modulation",
    )(cond, w_mod, b_mod.reshape(DEPTH, 1, n))


def _mod_row(i, bm, tp, ls):
    nbp = tp // bm
    return jnp.where(i < nbp, 0, 1 + (i - nbp) // (ls // bm))


def _in_proj_kernel(x_ref, mod_ref, g_ref, w_ref, o_ref, h_ref):
    @pl.when(pl.program_id(1) == 0)
    def _():
        x = x_ref[...]
        r = lax.rsqrt(jnp.mean(x * x, axis=-1, keepdims=True) + EPS)
        h = (x * r) * g_ref[...] * (1.0 + mod_ref[0, 1:2, :]) + mod_ref[0, 0:1, :]
        h_ref[...] = h.astype(BF16)

    o_ref[...] = jnp.dot(h_ref[...], w_ref[...], preferred_element_type=F32)


def _in_proj(x, mod, g, w, tp, ls):
    t = x.shape[0]
    bm = min(IN_BM, ls, tp)
    return pl.pallas_call(
        _in_proj_kernel,
        grid=(t // bm, IN_COLS // IN_BN),
        in_specs=[
            pl.BlockSpec((bm, D_MODEL), lambda i, j: (i, 0)),
            pl.BlockSpec((1, 6, D_MODEL), lambda i, j: (_mod_row(i, bm, tp, ls), 0, 0)),
            pl.BlockSpec((1, D_MODEL), lambda i, j: (0, 0)),
            pl.BlockSpec((D_MODEL, IN_BN), lambda i, j: (0, j)),
        ],
        out_specs=pl.BlockSpec((bm, IN_BN), lambda i, j: (i, j)),
        out_shape=jax.ShapeDtypeStruct((t, IN_COLS), F32),
        scratch_shapes=[pltpu.VMEM((bm, D_MODEL), BF16)],
        compiler_params=_cp(("arbitrary", "arbitrary"), 56),
        name="in_proj",
    )(x, mod, g, w)


def _dft_tables(L):
    k = np.arange(L, dtype=np.int64)
    ang = ((k[:, None] * k[None, :]) % (2 * L)).astype(np.float64) * (math.pi / L)
    cm = np.cos(ang)
    sm = np.sin(ang)
    alt = np.where(k % 2 == 0, 1.0, -1.0)
    sf = -sm
    sf[0] = alt
    fwd = np.concatenate([cm, sf], axis=0)
    wk = np.where(k == 0, 1.0, 2.0) / (2 * L)
    si = -sm / L
    si[:, 0] = alt / (2 * L)
    inv = np.concatenate([cm * wk[None, :], si], axis=1)
    alt8 = np.zeros((8, L))
    alt8[0] = alt
    return (jnp.asarray(cm, F32), jnp.asarray(sm, F32), jnp.asarray(alt8, F32),
            jnp.asarray(fwd.astype(BF16)), jnp.asarray(inv.astype(BF16)))


def _filter_consts(L):
    t = np.linspace(0.0, 1.0, L, dtype=np.float32)[:, None]
    bands = np.linspace(1e-4, HYENA_BANDS - 1, HYENA_BANDS, dtype=np.float32)[None, :]
    z = np.concatenate([t, np.cos(2 * math.pi * bands * t), np.sin(2 * math.pi * bands * t)], axis=-1)
    z = np.pad(z, ((0, 0), (0, POS_PAD - HYENA_POS_DIM)))
    max_decay = math.log(HYENA_TARGET) / HYENA_FAST_DECAY
    min_decay = math.log(HYENA_TARGET) / HYENA_SLOW_DECAY
    deltas = np.abs(np.linspace(min_decay, max_decay, GROUP_WIDTH, dtype=np.float32))
    window = np.exp(-t * deltas[None, :])
    return jnp.asarray(z, F32), jnp.asarray(window, F32)


def _filter_kernel(z_ref, w1_ref, b1_ref, w2_ref, b2_ref, w3_ref, fr_ref, win_ref, cm_ref, sm_ref, alt_ref,
                   kre_ref, kim_ref):
    c = GROUP_WIDTH
    h = jnp.dot(z_ref[...], w1_ref[...], precision=HIGHEST, preferred_element_type=F32) + b1_ref[...]
    h = jnp.sin(fr_ref[0:1, :] * h)
    h = jnp.dot(h, w2_ref[...], precision=HIGHEST, preferred_element_type=F32) + b2_ref[...]
    h = jnp.sin(fr_ref[1:2, :] * h)
    hfb = jnp.dot(h, w3_ref[...], precision=HIGHEST, preferred_element_type=F32)
    win = win_ref[...]
    hf = hfb[:, :c] * win
    hb = hfb[:, c:] * win
    row = lax.broadcasted_iota(jnp.int32, hf.shape, 0)
    g = hf + jnp.where(row == 0, 0.0, hb)
    kre = jnp.dot(cm_ref[...], g, precision=HIGHEST, preferred_element_type=F32)
    kim = jnp.dot(sm_ref[...], hb - hf, precision=HIGHEST, preferred_element_type=F32)
    kn = jnp.dot(alt_ref[...], g, precision=HIGHEST, preferred_element_type=F32)[0:1, :]
    kre_ref[...] = kre
    kim_ref[...] = jnp.where(row == 0, kn, kim)


def _hyena_filter(L, z, window, cm, sm, alt8, w1, b1, w2, b2, w3, freq):
    w1p = jnp.pad(w1, ((0, POS_PAD - HYENA_POS_DIM), (0, 0)))
    return pl.pallas_call(
        _filter_kernel,
        out_shape=(jax.ShapeDtypeStruct((L, GROUP_WIDTH), F32), jax.ShapeDtypeStruct((L, GROUP_WIDTH), F32)),
        compiler_params=pltpu.CompilerParams(vmem_limit_bytes=56 * MIB),
        name="hyena_filter",
    )(z, w1p, b1.reshape(1, -1), w2, b2.reshape(1, -1), w3, freq, window, cm, sm, alt8)


def _short_conv(a, w_ref, b_ref, row, L):
    prev = jnp.where(row == 0, 0.0, pltpu.roll(a, 1, 0))
    nxt = jnp.where(row == L - 1, 0.0, pltpu.roll(a, L - 1, 0))
    return b_ref[...] + prev * w_ref[0:1, :] + a * w_ref[1:2, :] + nxt * w_ref[2:3, :]


def _hyena_kernel(a0_ref, a1_ref, a2_ref, w0_ref, w1_ref, w2_ref, b0_ref, b1_ref, b2_ref, kre_ref, kim_ref,
                  skip_ref, fwd_ref, inv_ref, o_ref, *, L):
    row = lax.broadcasted_iota(jnp.int32, (L, HY_CC), 0)
    x1 = _short_conv(a1_ref[...], w1_ref, b1_ref, row, L)
    hv = _short_conv(a2_ref[...], w2_ref, b2_ref, row, L)
    u = hv * x1
    spec = jnp.dot(fwd_ref[...], u.astype(BF16), preferred_element_type=F32)
    ure, uim = spec[:L], spec[L:]
    kre, kim = kre_ref[...], kim_ref[...]
    first = row == 0
    yre = ure * kre - jnp.where(first, 0.0, uim * kim)
    yim = jnp.where(first, uim * kim, ure * kim + uim * kre)
    yspec = jnp.concatenate([yre, yim], axis=0).astype(BF16)
    y = jnp.dot(inv_ref[...], yspec, preferred_element_type=F32)
    x0 = _short_conv(a0_ref[...], w0_ref, b0_ref, row, L)
    o_ref[...] = (x0 * (y + u * skip_ref[...])).astype(BF16)


def _hyena(proj, row_off, B, L, conv_w, conv_b, kre, kim, skip, fwd, inv):
    ncc = GROUP_WIDTH // HY_CC

    def a_spec(part):
        return pl.BlockSpec((L, HY_CC), lambda b, cc: (row_off + b, part * ncc + cc))

    def w_spec(part, rows):
        return pl.BlockSpec((rows, HY_CC), lambda b, cc: (0, part * ncc + cc))

    chan = pl.BlockSpec((L, HY_CC), lambda b, cc: (0, cc))
    conv_b = conv_b.reshape(1, -1)
    return pl.pallas_call(
        functools.partial(_hyena_kernel, L=L),
        grid=(B, ncc),
        in_specs=[a_spec(0), a_spec(1), a_spec(2), w_spec(0, 3), w_spec(1, 3), w_spec(2, 3),
                  w_spec(0, 1), w_spec(1, 1), w_spec(2, 1), chan, chan,
                  pl.BlockSpec((1, HY_CC), lambda b, cc: (0, cc)),
                  pl.BlockSpec((2 * L, L), lambda b, cc: (0, 0)),
                  pl.BlockSpec((L, 2 * L), lambda b, cc: (0, 0))],
        out_specs=pl.BlockSpec((L, HY_CC), lambda b, cc: (b, cc)),
        out_shape=jax.ShapeDtypeStruct((B * L, GROUP_WIDTH), BF16),
        compiler_params=_cp(("arbitrary", "arbitrary"), 56),
        name="hyena_conv",
    )(proj, proj, proj, conv_w, conv_w, conv_w, conv_b, conv_b, conv_b, kre, kim, skip.reshape(1, -1), fwd, inv)


def _sgu_kernel(u_ref, v_ref, g_ref, b_ref, w_ref, bs_ref, o_ref, *, L):
    v = v_ref[...]
    mu = jnp.mean(v, axis=-1, keepdims=True)
    xc = v - mu
    vn = xc * lax.rsqrt(jnp.mean(xc * xc, axis=-1, keepdims=True) + EPS) * g_ref[...] + b_ref[...]
    vnb = vn.astype(BF16)
    hc = GROUP_WIDTH // SGU_HEADS
    for h in range(SGU_HEADS):
        wh = w_ref[h].astype(BF16)
        bias = bs_ref[h]
        cols = slice(h * hc, (h + 1) * hc)
        for n in range(L // CHUNK):
            rows = slice(n * CHUNK, (n + 1) * CHUNK)
            mixed = jnp.dot(wh, vnb[rows, cols], preferred_element_type=F32) + bias
            o_ref[rows, cols] = (u_ref[rows, cols] * mixed).astype(BF16)


def _sgu(proj, row_off, B, L, g, b, w_s, b_s):
    hc = GROUP_WIDTH // SGU_HEADS
    bias = jnp.broadcast_to(b_s[:, :, None], (SGU_HEADS, CHUNK, hc))
    return pl.pallas_call(
        functools.partial(_sgu_kernel, L=L),
        grid=(B,),
        in_specs=[pl.BlockSpec((L, GROUP_WIDTH), lambda i: (row_off + i, 3)),
                  pl.BlockSpec((L, GROUP_WIDTH), lambda i: (row_off + i, 4)),
                  pl.BlockSpec((1, GROUP_WIDTH), lambda i: (0, 0)),
                  pl.BlockSpec((1, GROUP_WIDTH), lambda i: (0, 0)),
                  pl.BlockSpec((SGU_HEADS, CHUNK, CHUNK), lambda i: (0, 0, 0)),
                  pl.BlockSpec((SGU_HEADS, CHUNK, hc), lambda i: (0, 0, 0))],
        out_specs=pl.BlockSpec((L, GROUP_WIDTH), lambda i: (i, 0)),
        out_shape=jax.ShapeDtypeStruct((B * L, GROUP_WIDTH), BF16),
        compiler_params=_cp(("arbitrary",), 40),
        name="sgu",
    )(proj, proj, g.reshape(1, -1), b.reshape(1, -1), w_s, bias)


def _rope_tables(L, dh, width):
    quarter = dh // 4
    pos = np.arange(L)
    rowp = (pos // GRID_W).astype(np.float32)
    colp = (pos % GRID_W).astype(np.float32)
    inv_freq = (ROPE_THETA ** (-np.arange(quarter, dtype=np.float32) / quarter)).astype(np.float32)
    ar = (rowp[:, None] * inv_freq[None, :]).astype(np.float64)
    ac = (colp[:, None] * inv_freq[None, :]).astype(np.float64)
    cos = np.concatenate([np.cos(ar), np.cos(ar), np.cos(ac), np.cos(ac)], axis=-1)
    sin = np.concatenate([-np.sin(ar), np.sin(ar), -np.sin(ac), np.sin(ac)], axis=-1)
    reps = width // dh
    return jnp.asarray(np.tile(cos, (1, reps)), F32), jnp.asarray(np.tile(sin, (1, reps)), F32)


def _rope(x, cos, sin, quarter):
    w = x.shape[-1]
    lane = lax.broadcasted_iota(jnp.int32, x.shape, 1)
    partner = jnp.where((lane & quarter) != 0, pltpu.roll(x, quarter, 1), pltpu.roll(x, w - quarter, 1))
    return x * cos + partner * sin


def _head_rms(x, g, heads, dh):
    outs = []
    for h in range(heads):
        xh = x[:, h * dh:(h + 1) * dh]
        outs.append(xh * lax.rsqrt(jnp.mean(xh * xh, axis=-1, keepdims=True) + EPS) * g)
    return jnp.concatenate(outs, axis=-1)


def _softmax_pv(q, k, v):
    s = lax.dot_general(q, k, (((1,), (1,)), ((), ())), preferred_element_type=F32)
    e = jnp.exp(s - jnp.max(s, axis=-1, keepdims=True))
    inv_l = 1.0 / jnp.sum(e, axis=-1, keepdims=True)
    return jnp.dot(e.astype(BF16), v, preferred_element_type=F32), inv_l


def _gqa_body(q, k_ref, v_ref, o_ref, L):
    dh = ATTN_HEAD_DIM
    group = ATTN_HEADS // ATTN_KV_HEADS
    qs = (q * dh ** -0.5).astype(BF16)
    qc = min(Q_CHUNK, L)
    for h in range(ATTN_HEADS):
        kv = h // group
        kh = k_ref[:, kv * dh:(kv + 1) * dh]
        vh = v_ref[:, kv * dh:(kv + 1) * dh]
        for c in range(L // qc):
            rows = slice(c * qc, (c + 1) * qc)
            pv, inv_l = _softmax_pv(qs[rows, h * dh:(h + 1) * dh], kh, vh)
            o_ref[rows, h * dh:(h + 1) * dh] = (pv * inv_l).astype(BF16)


def _gqa_ctx_kernel(q_ref, k_ref, v_ref, gq_ref, gk_ref, o_ref, ko_ref, vo_ref, kb_ref, vb_ref, *, L):
    q = _head_rms(q_ref[...], gq_ref[...], ATTN_HEADS, ATTN_HEAD_DIM)
    k = _head_rms(k_ref[...], gk_ref[...], ATTN_KV_HEADS, ATTN_HEAD_DIM)
    v = v_ref[...]
    ko_ref[...] = k
    vo_ref[...] = v
    kb_ref[...] = k.astype(BF16)
    vb_ref[...] = v.astype(BF16)
    _gqa_body(q, kb_ref, vb_ref, o_ref, L)


def _gqa_lat_kernel(q_ref, k_ref, v_ref, gq_ref, gk_ref, cq_ref, sq_ref, ck_ref, sk_ref, pk_ref, pv_ref, o_ref,
                    kb_ref, vb_ref, *, L, P):
    quarter = ATTN_HEAD_DIM // 4
    q = _rope(_head_rms(q_ref[...], gq_ref[...], ATTN_HEADS, ATTN_HEAD_DIM), cq_ref[...], sq_ref[...], quarter)
    k = _rope(_head_rms(k_ref[...], gk_ref[...], ATTN_KV_HEADS, ATTN_HEAD_DIM), ck_ref[...], sk_ref[...], quarter)
    kb_ref[0:P, :] = pk_ref[...].astype(BF16)
    kb_ref[P:P + L, :] = k.astype(BF16)
    vb_ref[0:P, :] = pv_ref[...].astype(BF16)
    vb_ref[P:P + L, :] = v_ref[...].astype(BF16)
    _gqa_body(q, kb_ref, vb_ref, o_ref, L)


def _gqa(proj, row_off, B, L, gq, gk, l, past=None, tables=None):
    qw = ATTN_HEADS * ATTN_HEAD_DIM
    kw = ATTN_KV_HEADS * ATTN_HEAD_DIM
    qoff = 5
    koff = (2560 + qw) // kw
    in_specs = [pl.BlockSpec((L, qw), lambda i: (row_off + i, qoff)),
                pl.BlockSpec((L, kw), lambda i: (row_off + i, koff)),
                pl.BlockSpec((L, kw), lambda i: (row_off + i, koff + 1)),
                pl.BlockSpec((1, ATTN_HEAD_DIM), lambda i: (0, 0)),
                pl.BlockSpec((1, ATTN_HEAD_DIM), lambda i: (0, 0))]
    args = [proj, proj, proj, gq.reshape(1, -1), gk.reshape(1, -1)]
    y_spec = pl.BlockSpec((L, qw), lambda i: (i, 0))
    y_shape = jax.ShapeDtypeStruct((B * L, qw), BF16)
    if past is None:
        return pl.pallas_call(
            functools.partial(_gqa_ctx_kernel, L=L),
            grid=(B,),
            in_specs=in_specs,
            out_specs=(y_spec, pl.BlockSpec((L, kw), lambda i: (i, 0)), pl.BlockSpec((L, kw), lambda i: (i, 0))),
            out_shape=(y_shape, jax.ShapeDtypeStruct((B * L, kw), F32), jax.ShapeDtypeStruct((B * L, kw), F32)),
            scratch_shapes=[pltpu.VMEM((L, kw), BF16), pltpu.VMEM((L, kw), BF16)],
            compiler_params=_cp(("arbitrary",), 40),
            name="gqa_ctx",
        )(*args)
    pk, pv = past
    P = pk.shape[2]
    cq, sq, ck, sk = tables
    const = lambda w: pl.BlockSpec((L, w), lambda i: (0, 0))
    cache = pl.BlockSpec((None, None, P, kw), lambda i: (i, l, 0, 0))
    return pl.pallas_call(
        functools.partial(_gqa_lat_kernel, L=L, P=P),
        grid=(B,),
        in_specs=in_specs + [const(qw), const(qw), const(kw), const(kw), cache, cache],
        out_specs=y_spec,
        out_shape=y_shape,
        scratch_shapes=[pltpu.VMEM((P + L, kw), BF16), pltpu.VMEM((P + L, kw), BF16)],
        compiler_params=_cp(("arbitrary",), 56),
        name="gqa_lat",
    )(*args, cq, sq, ck, sk, pk, pv)


def _diff_body(dq, k_ref, v_ref, dl_ref, g_ref, o_ref, L, lambda_init):
    dv = DIFF_V_DIM
    dl = dl_ref[...]
    lam = (jnp.exp(jnp.sum(dl[0:1] * dl[1:2], axis=-1, keepdims=True))
           - jnp.exp(jnp.sum(dl[2:3] * dl[3:4], axis=-1, keepdims=True)) + lambda_init)
    qs = dq * DIFF_QK_DIM ** -0.5
    first_map = lax.broadcasted_iota(jnp.int32, (1, dv), 1) < DIFF_QK_DIM
    qc = min(Q_CHUNK, L)
    for h in range(DIFF_HEADS):
        cols = slice(h * dv, (h + 1) * dv)
        kh = k_ref[:, cols]
        vh = v_ref[:, cols]
        for c in range(L // qc):
            rows = slice(c * qc, (c + 1) * qc)
            qh = qs[rows, cols]
            pv0, inv0 = _softmax_pv(jnp.where(first_map, qh, 0.0).astype(BF16), kh, vh)
            pv1, inv1 = _softmax_pv(jnp.where(first_map, 0.0, qh).astype(BF16), kh, vh)
            o = pv0 * inv0 - pv1 * (lam * inv1)
            r = lax.rsqrt(jnp.mean(o * o, axis=-1, keepdims=True) + EPS)
            o_ref[rows, cols] = ((o * r) * g_ref[...] * (1.0 - lambda_init)).astype(BF16)


def _diff_ctx_kernel(q_ref, k_ref, v_ref, dl_ref, g_ref, o_ref, ko_ref, vo_ref, kb_ref, vb_ref, *, L, lambda_init):
    k = k_ref[...]
    v = v_ref[...]
    ko_ref[...] = k
    vo_ref[...] = v
    kb_ref[...] = k.astype(BF16)
    vb_ref[...] = v.astype(BF16)
    _diff_body(q_ref[...], kb_ref, vb_ref, dl_ref, g_ref, o_ref, L, lambda_init)


def _diff_lat_kernel(q_ref, k_ref, v_ref, dl_ref, g_ref, cos_ref, sin_ref, pk_ref, pv_ref, o_ref, kb_ref, vb_ref,
                     *, L, P, lambda_init):
    quarter = DIFF_QK_DIM // 4
    cos, sin = cos_ref[...], sin_ref[...]
    q = _rope(q_ref[...], cos, sin, quarter)
    k = _rope(k_ref[...], cos, sin, quarter)
    kb_ref[0:P, :] = pk_ref[...].astype(BF16)
    kb_ref[P:P + L, :] = k.astype(BF16)
    vb_ref[0:P, :] = pv_ref[...].astype(BF16)
    vb_ref[P:P + L, :] = v_ref[...].astype(BF16)
    _diff_body(q, kb_ref, vb_ref, dl_ref, g_ref, o_ref, L, lambda_init)


def _diff(proj, row_off, B, L, dl, g, l, past=None, tables=None):
    w = GROUP_WIDTH
    lambda_init = 0.8 - 0.6 * math.exp(-0.3 * l)
    in_specs = [pl.BlockSpec((L, w), lambda i: (row_off + i, 7)),
                pl.BlockSpec((L, w), lambda i: (row_off + i, 8)),
                pl.BlockSpec((L, w), lambda i: (row_off + i, 9)),
                pl.BlockSpec((4, DIFF_QK_DIM), lambda i: (0, 0)),
                pl.BlockSpec((1, DIFF_V_DIM), lambda i: (0, 0))]
    args = [proj, proj, proj, dl, g.reshape(1, -1)]
    y_spec = pl.BlockSpec((L, w), lambda i: (i, 0))
    y_shape = jax.ShapeDtypeStruct((B * L, w), BF16)
    if past is None:
        return pl.pallas_call(
            functools.partial(_diff_ctx_kernel, L=L, lambda_init=lambda_init),
            grid=(B,),
            in_specs=in_specs,
            out_specs=(y_spec, pl.BlockSpec((L, w), lambda i: (i, 0)), pl.BlockSpec((L, w), lambda i: (i, 0))),
            out_shape=(y_shape, jax.ShapeDtypeStruct((B * L, w), F32), jax.ShapeDtypeStruct((B * L, w), F32)),
            scratch_shapes=[pltpu.VMEM((L, w), BF16), pltpu.VMEM((L, w), BF16)],
            compiler_params=_cp(("arbitrary",), 40),
            name="diff_ctx",
        )(*args)
    pk, pv = past
    P = pk.shape[2]
    cos, sin = tables
    const = pl.BlockSpec((L, w), lambda i: (0, 0))
    cache = pl.BlockSpec((None, None, P, w), lambda i: (i, l, 0, 0))
    return pl.pallas_call(
        functools.partial(_diff_lat_kernel, L=L, P=P, lambda_init=lambda_init),
        grid=(B,),
        in_specs=in_specs + [const, const, cache, cache],
        out_specs=y_spec,
        out_shape=y_shape,
        scratch_shapes=[pltpu.VMEM((P + L, w), BF16), pltpu.VMEM((P + L, w), BF16)],
        compiler_params=_cp(("arbitrary",), 56),
        name="diff_lat",
    )(*args, cos, sin, pk, pv)


def _pack_halves(lo, hi):
    lo_bits = pltpu.bitcast(lo.astype(BF16).astype(F32), jnp.uint32) >> 16
    hi_bits = pltpu.bitcast(hi.astype(BF16).astype(F32), jnp.uint32) & jnp.uint32(0xFFFF0000)
    return lo_bits | hi_bits


def _unpack_halves(words):
    lo = pltpu.bitcast(words << 16, F32)
    hi = pltpu.bitcast(words & jnp.uint32(0xFFFF0000), F32)
    return lo, hi


def _route_rows(logits, prefix_base, tri_ref):
    lane = lax.broadcasted_iota(jnp.int32, logits.shape, 1)
    neg = jnp.float32(-jnp.inf)
    work = jnp.where(lane < N_EXPERTS, logits, neg)
    sels, vals, idxs = [], [], []
    for _ in range(TOP_K):
        mk = jnp.max(work, axis=-1, keepdims=True)
        ik = jnp.min(jnp.where(work == mk, lane, ROUTER_PAD), axis=-1, keepdims=True)
        sel = lane == ik
        work = jnp.where(sel, neg, work)
        sels.append(sel)
        vals.append(mk)
        idxs.append(ik)
    es = [jnp.exp(v - vals[0]) for v in vals]
    inv = 1.0 / (es[0] + es[1] + es[2] + es[3])
    member = jnp.zeros(logits.shape, F32)
    for sel in sels:
        member = member + jnp.where(sel, 1.0, 0.0)
    prefix = jnp.dot(tri_ref[...], member.astype(BF16), preferred_element_type=F32) + prefix_base
    ri = jnp.zeros(logits.shape, jnp.int32)
    rg = jnp.zeros(logits.shape, F32)
    for k in range(TOP_K):
        rank = jnp.sum(jnp.where(sels[k], prefix, 0.0), axis=-1, keepdims=True).astype(jnp.int32)
        ri = jnp.where(lane == k, idxs[k], ri)
        ri = jnp.where(lane == TOP_K + k, rank, ri)
        rg = jnp.where(lane == k, es[k] * inv, rg)
    return ri, rg, jnp.sum(member, axis=0, keepdims=True)


def _out_proj_kernel(pa_ref, pb_ref, pc_ref, pd_ref, sa_ref, sb_ref, sc_ref, sd_ref, w_ref, x_ref, mod_ref, ng_ref,
                     wr_ref, br_ref, tri_ref, x1_ref, hu_ref, ri_ref, rg_ref, cnt_ref, cat_ref, carry_ref, *, nbp):
    i = pl.program_id(0)

    @pl.when(i == 0)
    def _():
        carry_ref[...] = jnp.zeros_like(carry_ref)

    gw = GROUP_WIDTH

    @pl.when(i < nbp)
    def _():
        for m_, ref in enumerate((pa_ref, pb_ref, pc_ref, pd_ref)):
            cat_ref[:, m_ * gw:(m_ + 1) * gw] = ref[...]

    @pl.when(i >= nbp)
    def _():
        for m_, ref in enumerate((sa_ref, sb_ref, sc_ref, sd_ref)):
            cat_ref[:, m_ * gw:(m_ + 1) * gw] = ref[...]

    m = jnp.dot(cat_ref[...], w_ref[...], preferred_element_type=F32)
    mn = m * lax.rsqrt(jnp.mean(m * m, axis=-1, keepdims=True) + EPS) * ng_ref[1:2, :]
    x1 = x_ref[...] + mod_ref[0, 2:3, :] * mn
    x1_ref[...] = x1
    hn = x1 * lax.rsqrt(jnp.mean(x1 * x1, axis=-1, keepdims=True) + EPS) * ng_ref[2:3, :]
    h = hn * (1.0 + mod_ref[0, 4:5, :]) + mod_ref[0, 3:4, :]
    half = D_MODEL // 2
    hu_ref[...] = _pack_halves(h[:, :half], h[:, half:])
    hh = h.astype(BF16)
    hl = (h - hh.astype(F32)).astype(BF16)
    both = jnp.dot(hh, wr_ref[...], preferred_element_type=F32)
    low = jnp.dot(hl, wr_ref[:, :ROUTER_PAD], preferred_element_type=F32)
    logits = both[:, :ROUTER_PAD] + both[:, ROUTER_PAD:] + low + br_ref[...]
    ri, rg, counts = _route_rows(logits, carry_ref[0:1, :], tri_ref)
    ri_ref[...] = ri
    rg_ref[...] = rg
    carry_ref[...] = carry_ref[...] + counts
    cnt_ref[...] = carry_ref[...]


def _out_proj(ys_prompt, ys_latent, w, x, mod, ng, wr, br, tp, ls):
    t = x.shape[0]
    bm = OUT_BM
    nbp = tp // bm
    pspec = pl.BlockSpec((bm, GROUP_WIDTH), lambda i: (jnp.minimum(i, nbp - 1), 0))
    sspec = pl.BlockSpec((bm, GROUP_WIDTH), lambda i: (jnp.maximum(i - nbp, 0), 0))
    row = pl.BlockSpec((bm, D_MODEL), lambda i: (i, 0))
    info = pl.BlockSpec((bm, ROUTER_PAD), lambda i: (i, 0))
    tri = jnp.asarray(np.tril(np.ones((bm, bm), np.float32), -1).astype(BF16))
    return pl.pallas_call(
        functools.partial(_out_proj_kernel, nbp=nbp),
        grid=(t // bm,),
        in_specs=[pspec, pspec, pspec, pspec, sspec, sspec, sspec, sspec,
                  pl.BlockSpec((D_MODEL, D_MODEL), lambda i: (0, 0)),
                  row,
                  pl.BlockSpec((1, 6, D_MODEL), lambda i: (_mod_row(i, bm, tp, ls), 0, 0)),
                  pl.BlockSpec((4, D_MODEL), lambda i: (0, 0)),
                  pl.BlockSpec((D_MODEL, 2 * ROUTER_PAD), lambda i: (0, 0)),
                  pl.BlockSpec((1, ROUTER_PAD), lambda i: (0, 0)),
                  pl.BlockSpec((bm, bm), lambda i: (0, 0))],
        out_specs=(row, pl.BlockSpec((bm, D_MODEL // 2), lambda i: (i, 0)), info, info,
                   pl.BlockSpec((8, ROUTER_PAD), lambda i: (0, 0))),
        out_shape=(jax.ShapeDtypeStruct((t, D_MODEL), F32), jax.ShapeDtypeStruct((t, D_MODEL // 2), jnp.uint32),
                   jax.ShapeDtypeStruct((t, ROUTER_PAD), jnp.int32), jax.ShapeDtypeStruct((t, ROUTER_PAD), F32),
                   jax.ShapeDtypeStruct((8, ROUTER_PAD), F32)),
        scratch_shapes=[pltpu.VMEM((bm, D_MODEL), BF16), pltpu.VMEM((8, ROUTER_PAD), F32)],
        compiler_params=_cp(("arbitrary",), 56),
        name="out_proj",
    )(*ys_prompt, *ys_latent, w, x, mod, ng, wr, br, tri)


def _new_expert(be_ref, i):
    return jnp.logical_or(i == 0, be_ref[jnp.maximum(i - 1, 0)] != be_ref[i])


def _next_tile(nx_ref, be_ref, i, j, nj):
    nxt = nx_ref[i]
    has_run = nxt >= 0
    return jnp.logical_or(has_run, j + 1 < nj), jnp.where(has_run, j, j + 1), jnp.where(has_run, nxt, be_ref[0])


def _gm1_kernel(be_ref, nu_ref, hf_ref, nx_ref, x_ref, w_ref, bg_ref, bu_ref, o_ref, wst_ref, wbf_ref, xb_ref, sem,
                *, l, nj):
    j = pl.program_id(0)
    i = pl.program_id(1)
    used = i < nu_ref[0]
    tn = GM1_TN
    half = D_MODEL // 2

    def tile_copies(jt, e):
        gcol = pl.multiple_of(jt * tn, tn)
        ucol = pl.multiple_of(D_FF + jt * tn, tn)
        return (pltpu.make_async_copy(w_ref.at[l, e, :, pl.ds(gcol, tn)], wst_ref.at[:, pl.ds(0, tn)], sem.at[0]),
                pltpu.make_async_copy(w_ref.at[l, e, :, pl.ds(ucol, tn)], wst_ref.at[:, pl.ds(tn, tn)], sem.at[1]))

    @pl.when(jnp.logical_and(used, _new_expert(be_ref, i)))
    def _():
        e = be_ref[i]

        @pl.when(jnp.logical_and(j == 0, i == 0))
        def _():
            for cp in tile_copies(0, e):
                cp.start()

        for cp in tile_copies(j, e):
            cp.wait()
        wbf_ref[...] = wst_ref[...].astype(BF16)
        exists, jn, en = _next_tile(nx_ref, be_ref, i, j, nj)

        @pl.when(exists)
        def _():
            for cp in tile_copies(jn, en):
                cp.start()

    def compute(rows):
        lo, hi = _unpack_halves(x_ref[:rows, :])
        xb_ref[:rows, :half] = lo.astype(BF16)
        xb_ref[:rows, half:] = hi.astype(BF16)
        for c in range(tn // GM1_SUB):
            cols = slice(c * GM1_SUB, (c + 1) * GM1_SUB)
            ucols = slice(tn + c * GM1_SUB, tn + (c + 1) * GM1_SUB)
            g = jnp.dot(xb_ref[:rows, :], wbf_ref[:, cols], preferred_element_type=F32)
            u = jnp.dot(xb_ref[:rows, :], wbf_ref[:, ucols], preferred_element_type=F32)
            gate = jnp.minimum(g + bg_ref[:, cols], SWIGLU_LIMIT)
            up = jnp.clip(u + bu_ref[:, cols], -SWIGLU_LIMIT, SWIGLU_LIMIT)
            glu = gate * jax.nn.sigmoid(gate * SWIGLU_ALPHA)
            o_ref[:rows, cols] = ((up + 1.0) * glu).astype(BF16)
        if rows < MOE_BM:
            o_ref[rows:, :] = jnp.zeros((MOE_BM - rows, tn), BF16)

    small = hf_ref[i] != 0

    @pl.when(jnp.logical_and(used, jnp.logical_not(small)))
    def _():
        compute(MOE_BM)

    @pl.when(jnp.logical_and(used, small))
    def _():
        compute(MOE_BM // 2)

    @pl.when(jnp.logical_not(used))
    def _():
        o_ref[...] = jnp.zeros_like(o_ref)


def _gm1(layout, xs, w_gu, b_gu, l):
    rows = xs.shape[0]
    nb = rows // MOE_BM
    nj = D_FF // GM1_TN
    b_gu = b_gu.reshape(DEPTH, N_EXPERTS, 1, 2 * D_FF)
    last = lambda i, nu: jnp.minimum(i, nu[0] - 1)
    grid_spec = pltpu.PrefetchScalarGridSpec(
        num_scalar_prefetch=4,
        grid=(nj, nb),
        in_specs=[
            pl.BlockSpec((MOE_BM, D_MODEL // 2), lambda j, i, be, nu, hf, nx: (last(i, nu), 0)),
            pl.BlockSpec(memory_space=pl.ANY),
            pl.BlockSpec((None, None, 1, GM1_TN), lambda j, i, be, nu, hf, nx: (l, be[i], 0, j)),
            pl.BlockSpec((None, None, 1, GM1_TN), lambda j, i, be, nu, hf, nx: (l, be[i], 0, nj + j)),
        ],
        out_specs=pl.BlockSpec((MOE_BM, GM1_TN), lambda j, i, be, nu, hf, nx: (i, j)),
        scratch_shapes=[pltpu.VMEM((D_MODEL, 2 * GM1_TN), F32), pltpu.VMEM((D_MODEL, 2 * GM1_TN), BF16),
                        pltpu.VMEM((MOE_BM, D_MODEL), BF16), pltpu.SemaphoreType.DMA((2,))],
    )
    return pl.pallas_call(
        functools.partial(_gm1_kernel, l=l, nj=nj),
        grid_spec=grid_spec,
        out_shape=jax.ShapeDtypeStruct((rows, D_FF), BF16),
        compiler_params=_cp(("arbitrary", "arbitrary"), 56),
        name="moe_gate_up",
    )(*layout, xs, w_gu, b_gu, b_gu)


def _gm2_kernel(be_ref, nu_ref, hf_ref, nx_ref, a_ref, w_ref, b_ref, o_ref, wst_ref, wbf_ref, sem, *, l, nj):
    j = pl.program_id(0)
    i = pl.program_id(1)
    used = i < nu_ref[0]
    tn = GM2_TN
    half = tn // 2

    def tile_copy(jt, e):
        col = pl.multiple_of(jt * tn, tn)
        return pltpu.make_async_copy(w_ref.at[l, e, :, pl.ds(col, tn)], wst_ref, sem)

    @pl.when(jnp.logical_and(used, _new_expert(be_ref, i)))
    def _():
        e = be_ref[i]

        @pl.when(jnp.logical_and(j == 0, i == 0))
        def _():
            tile_copy(0, e).start()

        tile_copy(j, e).wait()
        wbf_ref[...] = wst_ref[...].astype(BF16)
        exists, jn, en = _next_tile(nx_ref, be_ref, i, j, nj)

        @pl.when(exists)
        def _():
            tile_copy(jn, en).start()

    def compute(rows):
        lo = jnp.dot(a_ref[:rows, :], wbf_ref[:, :half], preferred_element_type=F32) + b_ref[:, :half]
        hi = jnp.dot(a_ref[:rows, :], wbf_ref[:, half:], preferred_element_type=F32) + b_ref[:, half:]
        o_ref[:rows, :] = _pack_halves(lo, hi)
        if rows < MOE_BM:
            o_ref[rows:, :] = jnp.zeros((MOE_BM - rows, half), jnp.uint32)

    small = hf_ref[i] != 0

    @pl.when(jnp.logical_and(used, jnp.logical_not(small)))
    def _():
        compute(MOE_BM)

    @pl.when(jnp.logical_and(used, small))
    def _():
        compute(MOE_BM // 2)

    @pl.when(jnp.logical_not(used))
    def _():
        o_ref[...] = jnp.zeros_like(o_ref)


def _gm2(layout, act, w_dn, b_dn, l):
    rows = act.shape[0]
    nb = rows // MOE_BM
    nj = D_MODEL // GM2_TN
    b_dn = b_dn.reshape(DEPTH, N_EXPERTS, 1, D_MODEL)
    last = lambda i, nu: jnp.minimum(i, nu[0] - 1)
    grid_spec = pltpu.PrefetchScalarGridSpec(
        num_scalar_prefetch=4,
        grid=(nj, nb),
        in_specs=[
            pl.BlockSpec((MOE_BM, D_FF), lambda j, i, be, nu, hf, nx: (last(i, nu), 0)),
            pl.BlockSpec(memory_space=pl.ANY),
            pl.BlockSpec((None, None, 1, GM2_TN), lambda j, i, be, nu, hf, nx: (l, be[i], 0, j)),
        ],
        out_specs=pl.BlockSpec((MOE_BM, GM2_TN // 2), lambda j, i, be, nu, hf, nx: (i, j)),
        scratch_shapes=[pltpu.VMEM((D_FF, GM2_TN), F32), pltpu.VMEM((D_FF, GM2_TN), BF16),
                        pltpu.SemaphoreType.DMA(())],
    )
    return pl.pallas_call(
        functools.partial(_gm2_kernel, l=l, nj=nj),
        grid_spec=grid_spec,
        out_shape=jax.ShapeDtypeStruct((rows, D_MODEL // 2), jnp.uint32),
        compiler_params=_cp(("arbitrary", "arbitrary"), 56),
        name="moe_down",
    )(*layout, act, w_dn, b_dn)


def _block_layout(counts, n_blocks):
    padded = (counts + MOE_BM - 1) // MOE_BM * MOE_BM
    pad_end = jnp.cumsum(padded)
    pad_start = (pad_end - padded).astype(jnp.int32)
    starts = jnp.arange(n_blocks, dtype=jnp.int32) * MOE_BM
    block_e = jnp.minimum(jnp.sum(pad_end[None, :] <= starts[:, None], axis=1), N_EXPERTS - 1).astype(jnp.int32)
    n_used = (pad_end[-1] // MOE_BM).astype(jnp.int32).reshape(1)
    valid = jnp.clip(counts[block_e] - (starts - pad_start[block_e]), 0, MOE_BM)
    block_half = (valid <= MOE_BM // 2).astype(jnp.int32)
    idx = jnp.arange(N_EXPERTS, dtype=jnp.int32)
    later = jnp.where((counts > 0)[None, :] & (idx[None, :] > idx[:, None]), idx[None, :], N_EXPERTS)
    next_e = jnp.min(later, axis=1)
    block_next = jnp.where(next_e == N_EXPERTS, -1, next_e)[block_e].astype(jnp.int32)
    return pad_start, (block_e, n_used, block_half, block_next)


def _dispatch_kernel(dest_ref, ps_ref, cn_ref, h_ref, xs_ref, zero_ref, sem, zsem, *, bt):
    i = pl.program_id(0)
    base = i * bt

    def zero_rows(issue):
        def per_expert(e, carry):
            start = ps_ref[e] + cn_ref[e]
            end = (start + MOE_BM - 1) // MOE_BM * MOE_BM
            head_end = jnp.minimum((start + 7) // 8 * 8, end)

            def single(r, c):
                cp = pltpu.make_async_copy(zero_ref.at[pl.ds(0, 1)], xs_ref.at[pl.ds(r, 1)], zsem)
                cp.start() if issue else cp.wait()
                return c

            def chunk(q, c):
                r = pl.multiple_of(head_end + q * 8, 8)
                cp = pltpu.make_async_copy(zero_ref.at[pl.ds(0, 8)], xs_ref.at[pl.ds(r, 8)], zsem)
                cp.start() if issue else cp.wait()
                return c

            lax.fori_loop(start, head_end, single, 0)
            lax.fori_loop(0, (end - head_end) // 8, chunk, 0)
            return carry

        lax.fori_loop(0, N_EXPERTS, per_expert, 0)

        last = N_EXPERTS - 1
        used_rows = ps_ref[last] + (cn_ref[last] + MOE_BM - 1) // MOE_BM * MOE_BM

        def block(b, c):
            r = pl.multiple_of(b * MOE_BM, MOE_BM)
            cp = pltpu.make_async_copy(zero_ref, xs_ref.at[pl.ds(r, MOE_BM)], zsem)
            cp.start() if issue else cp.wait()
            return c

        lax.fori_loop(used_rows // MOE_BM, xs_ref.shape[0] // MOE_BM, block, 0)

    @pl.when(i == 0)
    def _():
        zero_ref[...] = jnp.zeros_like(zero_ref)
        zero_rows(True)

    def group(g, carry):
        r0 = pl.multiple_of(g * 8, 8)
        rows8 = h_ref.at[pl.ds(r0, 8)]
        for r in range(8):
            for k in range(TOP_K):
                dst = dest_ref[(base + r0 + r) * TOP_K + k]
                pltpu.make_async_copy(rows8.at[pl.ds(r, 1)], xs_ref.at[pl.ds(dst, 1)], sem).start(priority=k % 2)
        return carry

    lax.fori_loop(0, bt // 8, group, 0)
    for _ in range(TOP_K):
        pltpu.make_async_copy(h_ref, xs_ref.at[pl.ds(0, bt)], sem).wait()

    @pl.when(i == pl.num_programs(0) - 1)
    def _():
        zero_rows(False)


def _dispatch(dest, pad_start, counts, hu, rows):
    t, w = hu.shape
    bt = DISPATCH_BT
    grid_spec = pltpu.PrefetchScalarGridSpec(
        num_scalar_prefetch=3,
        grid=(t // bt,),
        in_specs=[pl.BlockSpec((bt, w), lambda i, dst, ps, cn: (i, 0))],
        out_specs=pl.BlockSpec(memory_space=pl.ANY),
        scratch_shapes=[pltpu.VMEM((MOE_BM, w), jnp.uint32), pltpu.SemaphoreType.DMA(()),
                        pltpu.SemaphoreType.DMA(())],
    )
    return pl.pallas_call(
        functools.partial(_dispatch_kernel, bt=bt),
        grid_spec=grid_spec,
        out_shape=jax.ShapeDtypeStruct((rows, w), jnp.uint32),
        compiler_params=pltpu.CompilerParams(dimension_semantics=("arbitrary",), has_side_effects=True),
        name="moe_dispatch",
    )(dest, pad_start, counts, hu)


def _finish_kernel(dest_ref, y_ref, rg_ref, x_ref, mod_ref, g_ref, *rest, bt, nbp):
    if nbp is None:
        o_ref, buf, sem = rest
    else:
        op_ref, os_ref, buf, sem = rest
    i = pl.program_id(0)
    n = pl.num_programs(0)
    slot = i % 2
    q = GM2_TN // 2
    nj = D_MODEL // GM2_TN

    def issue8(blk, to_slot, r0):
        for k in range(TOP_K):
            rows8 = buf.at[to_slot, k, pl.ds(r0, 8)]
            for r in range(8):
                src = dest_ref[(blk * bt + r0 + r) * TOP_K + k]
                pltpu.make_async_copy(y_ref.at[pl.ds(src, 1)], rows8.at[pl.ds(r, 1)], sem.at[to_slot]).start(
                    priority=k % 2)

    def issue_block(blk, to_slot):
        lax.fori_loop(0, bt // 8, lambda g, c: (issue8(blk, to_slot, pl.multiple_of(g * 8, 8)), c)[1], 0)

    def combine_rows(out_ref, r0):
        rows = pl.ds(r0, FIN_ROWS)
        rg = rg_ref[rows, :]
        parts = [None] * (2 * nj)
        for k in range(TOP_K):
            gate = rg[:, k:k + 1]
            for j in range(nj):
                lo, hi = _unpack_halves(buf[slot, k, rows, j * q:(j + 1) * q])
                for idx, val in ((2 * j, lo), (2 * j + 1, hi)):
                    parts[idx] = gate * val if k == 0 else parts[idx] + gate * val
        f = jnp.concatenate(parts, axis=-1)
        fn = f * lax.rsqrt(jnp.mean(f * f, axis=-1, keepdims=True) + EPS) * g_ref[...]
        out_ref[rows, :] = x_ref[rows, :] + mod_ref[0, 5:6, :] * fn

    def combine_all(out_ref):
        def chunk(c, carry):
            combine_rows(out_ref, pl.multiple_of(c * FIN_ROWS, FIN_ROWS))
            return carry

        lax.fori_loop(0, bt // FIN_ROWS, chunk, 0, unroll=FIN_UNROLL)

    @pl.when(i == 0)
    def _():
        issue_block(0, 0)

    @pl.when(i + 1 < n)
    def _():
        issue_block(i + 1, (i + 1) % 2)

    for k in range(TOP_K):
        pltpu.make_async_copy(y_ref.at[pl.ds(0, bt)], buf.at[slot, k], sem.at[slot]).wait()

    if nbp is None:
        combine_all(o_ref)
    else:
        @pl.when(i < nbp)
        def _():
            combine_all(op_ref)

        @pl.when(i >= nbp)
        def _():
            combine_all(os_ref)


def _finish(dest, yu, rg, x1, mod, g, tp, ls, split):
    t = x1.shape[0]
    bt = FIN_BM
    row = lambda w: pl.BlockSpec((bt, w), lambda i, dst: (i, 0))
    if split:
        nbp = tp // bt
        out_specs = (pl.BlockSpec((bt, D_MODEL), lambda i, dst: (jnp.minimum(i, nbp - 1), 0)),
                     pl.BlockSpec((bt, D_MODEL), lambda i, dst: (jnp.maximum(i - nbp, 0), 0)))
        out_shape = (jax.ShapeDtypeStruct((tp, D_MODEL), F32), jax.ShapeDtypeStruct((t - tp, D_MODEL), F32))
    else:
        nbp = None
        out_specs = row(D_MODEL)
        out_shape = jax.ShapeDtypeStruct((t, D_MODEL), F32)
    grid_spec = pltpu.PrefetchScalarGridSpec(
        num_scalar_prefetch=1,
        grid=(t // bt,),
        in_specs=[pl.BlockSpec(memory_space=pl.ANY),
                  row(ROUTER_PAD),
                  row(D_MODEL),
                  pl.BlockSpec((1, 6, D_MODEL), lambda i, dst: (_mod_row(i, bt, tp, ls), 0, 0)),
                  pl.BlockSpec((1, D_MODEL), lambda i, dst: (0, 0))],
        out_specs=out_specs,
        scratch_shapes=[pltpu.VMEM((2, TOP_K, bt, D_MODEL // 2), jnp.uint32), pltpu.SemaphoreType.DMA((2,))],
    )
    return pl.pallas_call(
        functools.partial(_finish_kernel, bt=bt, nbp=nbp),
        grid_spec=grid_spec,
        out_shape=out_shape,
        compiler_params=_cp(("arbitrary",), 48),
        name="moe_finish",
    )(dest, yu, rg, x1, mod, g)


def _moe(hu, ri, rg, cnt, x1, mod, g, w_gu, b_gu, w_dn, b_dn, l, tp, ls, split=False):
    t = hu.shape[0]
    n_blocks = -(-t * TOP_K // MOE_BM) + N_EXPERTS
    counts = cnt[0, :N_EXPERTS].astype(jnp.int32)
    pad_start, layout = _block_layout(counts, n_blocks)
    e_flat = ri[:, :TOP_K].reshape(-1)
    p_flat = ri[:, TOP_K:2 * TOP_K].reshape(-1)
    first = jnp.where(e_flat[:, None] == jnp.arange(N_EXPERTS, dtype=jnp.int32)[None, :], pad_start[None, :], 0)
    dest = jnp.sum(first, axis=1) + p_flat
    xs = _dispatch(dest, pad_start, counts, hu, n_blocks * MOE_BM)
    act = _gm1(layout, xs, w_gu, b_gu, l)
    yu = _gm2(layout, act, w_dn, b_dn, l)
    return _finish(dest, yu, rg, x1, mod, g, tp, ls, split)


def kernel(x_prompt, x_sample, cache_attn_k, cache_attn_v, cache_diff_k, cache_diff_v, c, c_ctx, w_mod, b_mod, norm_g, w_in, w_out, hy_conv_w, hy_conv_b, hy_ffn_w1, hy_ffn_b1, hy_ffn_w2, hy_ffn_b2, hy_ffn_w3, hy_sin_freq, hy_skip, sgu_norm_g, sgu_norm_b, sgu_w, sgu_b, attn_q_g, attn_k_g, diff_lambda, diff_norm_g, w_router, b_router, w_gate_up, b_gate_up, w_down, b_down):
    bp, lp, d = x_prompt.shape
    bs, ls, _ = x_sample.shape
    tp = bp * lp
    past = cache_attn_k.shape[2]
    x = jnp.concatenate([x_prompt.reshape(tp, d), x_sample.reshape(bs * ls, d)], axis=0)

    cond = jnp.concatenate([c_ctx[None, :], c, jnp.zeros((COND_ROWS - 1 - bs, d), F32)], axis=0)
    mod = _modulation(cond, w_mod, b_mod).reshape(DEPTH, COND_ROWS, 6, d)

    w_in_b = w_in.astype(BF16)
    w_out_b = w_out.astype(BF16)
    wr_f = jnp.pad(w_router, ((0, 0), (0, 0), (0, ROUTER_PAD - N_EXPERTS)))
    wr_hi = wr_f.astype(BF16)
    wr = jnp.concatenate([wr_hi, (wr_f - wr_hi.astype(F32)).astype(BF16)], axis=-1)
    br = jnp.pad(b_router, ((0, 0), (0, ROUTER_PAD - N_EXPERTS))).reshape(DEPTH, 1, ROUTER_PAD)

    pk = cache_attn_k.reshape(bs, DEPTH, past, -1)
    pv = cache_attn_v.reshape(bs, DEPTH, past, -1)
    pdk = cache_diff_k.reshape(bs, DEPTH, past, -1)
    pdv = cache_diff_v.reshape(bs, DEPTH, past, -1)

    groups = []
    for L in (lp, ls):
        cm, sm, alt8, fwd, inv = _dft_tables(L)
        z, window = _filter_consts(L)
        groups.append((L, cm, sm, alt8, fwd, inv, z, window))
    qw = ATTN_HEADS * ATTN_HEAD_DIM
    kw = ATTN_KV_HEADS * ATTN_HEAD_DIM
    attn_tables = _rope_tables(ls, ATTN_HEAD_DIM, qw) + _rope_tables(ls, ATTN_HEAD_DIM, kw)
    diff_tables = _rope_tables(ls, DIFF_QK_DIM, GROUP_WIDTH)

    new_k, new_v, new_dk, new_dv = [], [], [], []
    for l in range(DEPTH):
        ng = norm_g[l]
        proj = _in_proj(x, mod[l], ng[0:1], w_in_b[l], tp, ls)

        ys = []
        for gi, (B, row_off) in enumerate(((bp, 0), (bs, tp // ls))):
            L, cm, sm, alt8, fwd, inv, z, window = groups[gi]
            kre, kim = _hyena_filter(L, z, window, cm, sm, alt8, hy_ffn_w1[l], hy_ffn_b1[l], hy_ffn_w2[l],
                                     hy_ffn_b2[l], hy_ffn_w3[l], hy_sin_freq[l])
            ya = _hyena(proj, row_off, B, L, hy_conv_w[l], hy_conv_b[l], kre, kim, hy_skip[l], fwd, inv)
            yb = _sgu(proj, row_off, B, L, sgu_norm_g[l], sgu_norm_b[l], sgu_w[l], sgu_b[l])
            if gi == 0:
                yc, k_ctx, v_ctx = _gqa(proj, row_off, B, L, attn_q_g[l], attn_k_g[l], l)
                yd, dk_ctx, dv_ctx = _diff(proj, row_off, B, L, diff_lambda[l], diff_norm_g[l], l)
                new_k.append(k_ctx.reshape(bp, lp, ATTN_KV_HEADS, ATTN_HEAD_DIM))
                new_v.append(v_ctx.reshape(bp, lp, ATTN_KV_HEADS, ATTN_HEAD_DIM))
                new_dk.append(dk_ctx.reshape(bp, lp, DIFF_HEADS, 2, DIFF_QK_DIM))
                new_dv.append(dv_ctx.reshape(bp, lp, DIFF_HEADS, DIFF_V_DIM))
            else:
                yc = _gqa(proj, row_off, B, L, attn_q_g[l], attn_k_g[l], l, past=(pk, pv), tables=attn_tables)
                yd = _diff(proj, row_off, B, L, diff_lambda[l], diff_norm_g[l], l, past=(pdk, pdv),
                           tables=diff_tables)
            ys.append((ya, yb, yc, yd))
        x1, hu, ri, rg, cnt = _out_proj(ys[0], ys[1], w_out_b[l], x, mod[l], ng, wr[l], br[l], tp, ls)
        x = _moe(hu, ri, rg, cnt, x1, mod[l], ng[3:4], w_gate_up, b_gate_up, w_down, b_down, l, tp, ls,
                 split=(l == DEPTH - 1))

    y_prompt = x[0].reshape(bp, lp, d)
    y_sample = x[1].reshape(bs, ls, d)
    return (y_prompt, y_sample, jnp.stack(new_k, axis=1), jnp.stack(new_v, axis=1),
            jnp.stack(new_dk, axis=1), jnp.stack(new_dv, axis=1))
```

```python
import functools
import math

import numpy as np
import jax
import jax.numpy as jnp
from jax import lax
from jax.experimental import pallas as pl
from jax.experimental.pallas import tpu as pltpu

F32 = jnp.float32
BF16 = jnp.bfloat16
HIGHEST = lax.Precision.HIGHEST

D_MODEL = 2048
DEPTH = 2
GRID_W = 64
GROUP_WIDTH = 512
EPS = 1e-6
IN_COLS = 5120
HYENA_BANDS = 16
HYENA_POS_DIM = 1 + 2 * HYENA_BANDS
HYENA_FFN = 64
HYENA_FAST_DECAY = 0.3
HYENA_SLOW_DECAY = 1.5
HYENA_TARGET = 0.01
POS_PAD = 128
CHUNK = 128
SGU_HEADS = 4
ATTN_HEADS = 4
ATTN_KV_HEADS = 2
ATTN_HEAD_DIM = 128
ROPE_THETA = 10000.0
DIFF_HEADS = 4
DIFF_QK_DIM = 64
DIFF_V_DIM = 128
N_EXPERTS = 32
TOP_K = 4
D_FF = 2048
SWIGLU_LIMIT = 7.0
SWIGLU_ALPHA = 1.702
ROUTER_PAD = 128

IN_BM, IN_BN = 1024, 1024
OUT_BM = 256
MOE_BM = 512
GM1_TN = 1024
GM1_SUB = 256
GM2_TN = 2048
DISPATCH_BT = 512
FIN_BM = 256
FIN_ROWS = 32
FIN_UNROLL = 2
COND_ROWS = 16
MOD_TN = 1024
HY_CC = 256
Q_CHUNK = 512
MIB = 2 ** 20


def _cp(sem, vmem_mib):
    return pltpu.CompilerParams(dimension_semantics=sem, vmem_limit_bytes=vmem_mib * MIB)


def _mod_kernel(c_ref, w_ref, b_ref, o_ref):
    c = c_ref[...]
    s = c * jax.nn.sigmoid(c)
    o_ref[...] = jnp.dot(s.astype(BF16), w_ref[...].astype(BF16), preferred_element_type=F32) + b_ref[...]


def _modulation(cond, w_mod, b_mod):
    n = w_mod.shape[-1]
    return pl.pallas_call(
        _mod_kernel,
        grid=(DEPTH, n // MOD_TN),
        in_specs=[
            pl.BlockSpec((COND_ROWS, D_MODEL), lambda l, j: (0, 0)),
            pl.BlockSpec((None, D_MODEL, MOD_TN), lambda l, j: (l, 0, j)),
            pl.BlockSpec((None, 1, MOD_TN), lambda l, j: (l, 0, j)),
        ],
        out_specs=pl.BlockSpec((None, COND_ROWS, MOD_TN), lambda l, j: (l, 0, j)),
        out_shape=jax.ShapeDtypeStruct((DEPTH, COND_ROWS, n), F32),
        compiler_params=_cp(("arbitrary", "arbitrary"), 40),
        name="modulation",
    )(cond, w_mod, b_mod.reshape(DEPTH, 1, n))


def _mod_row(i, bm, tp, ls):
    nbp = tp // bm
    return jnp.where(i < nbp, 0, 1 + (i - nbp) // (ls // bm))


def _in_proj_kernel(x_ref, mod_ref, g_ref, w_ref, o_ref, h_ref):
    @pl.when(pl.program_id(1) == 0)
    def _():
        x = x_ref[...]
        r = lax.rsqrt(jnp.mean(x * x, axis=-1, keepdims=True) + EPS)
        h = (x * r) * g_ref[...] * (1.0 + mod_ref[0, 1:2, :]) + mod_ref[0, 0:1, :]
        h_ref[...] = h.astype(BF16)

    o_ref[...] = jnp.dot(h_ref[...], w_ref[...], preferred_element_type=F32)


def _in_proj(x, mod, g, w, tp, ls):
    t = x.shape[0]
    bm = min(IN_BM, ls, tp)
    return pl.pallas_call(
        _in_proj_kernel,
        grid=(t // bm, IN_COLS // IN_BN),
        in_specs=[
            pl.BlockSpec((bm, D_MODEL), lambda i, j: (i, 0)),
            pl.BlockSpec((1, 6, D_MODEL), lambda i, j: (_mod_row(i, bm, tp, ls), 0, 0)),
            pl.BlockSpec((1, D_MODEL), lambda i, j: (0, 0)),
            pl.BlockSpec((D_MODEL, IN_BN), lambda i, j: (0, j)),
        ],
        out_specs=pl.BlockSpec((bm, IN_BN), lambda i, j: (i, j)),
        out_shape=jax.ShapeDtypeStruct((t, IN_COLS), F32),
        scratch_shapes=[pltpu.VMEM((bm, D_MODEL), BF16)],
        compiler_params=_cp(("arbitrary", "arbitrary"), 56),
        name="in_proj",
    )(x, mod, g, w)


def _dft_tables(L):
    k = np.arange(L, dtype=np.int64)
    ang = ((k[:, None] * k[None, :]) % (2 * L)).astype(np.float64) * (math.pi / L)
    cm = np.cos(ang)
    sm = np.sin(ang)
    alt = np.where(k % 2 == 0, 1.0, -1.0)
    sf = -sm
    sf[0] = alt
    fwd = np.concatenate([cm, sf], axis=0)
    wk = np.where(k == 0, 1.0, 2.0) / (2 * L)
    si = -sm / L
    si[:, 0] = alt / (2 * L)
    inv = np.concatenate([cm * wk[None, :], si], axis=1)
    alt8 = np.zeros((8, L))
    alt8[0] = alt
    return (jnp.asarray(cm, F32), jnp.asarray(sm, F32), jnp.asarray(alt8, F32),
            jnp.asarray(fwd.astype(BF16)), jnp.asarray(inv.astype(BF16)))


def _filter_consts(L):
    t = np.linspace(0.0, 1.0, L, dtype=np.float32)[:, None]
    bands = np.linspace(1e-4, HYENA_BANDS - 1, HYENA_BANDS, dtype=np.float32)[None, :]
    z = np.concatenate([t, np.cos(2 * math.pi * bands * t), np.sin(2 * math.pi * bands * t)], axis=-1)
    z = np.pad(z, ((0, 0), (0, POS_PAD - HYENA_POS_DIM)))
    max_decay = math.log(HYENA_TARGET) / HYENA_FAST_DECAY
    min_decay = math.log(HYENA_TARGET) / HYENA_SLOW_DECAY
    deltas = np.abs(np.linspace(min_decay, max_decay, GROUP_WIDTH, dtype=np.float32))
    window = np.exp(-t * deltas[None, :])
    return jnp.asarray(z, F32), jnp.asarray(window, F32)


def _filter_kernel(z_ref, w1_ref, b1_ref, w2_ref, b2_ref, w3_ref, fr_ref, win_ref, cm_ref, sm_ref, alt_ref,
                   kre_ref, kim_ref):
    c = GROUP_WIDTH
    h = jnp.dot(z_ref[...], w1_ref[...], precision=HIGHEST, preferred_element_type=F32) + b1_ref[...]
    h = jnp.sin(fr_ref[0:1, :] * h)
    h = jnp.dot(h, w2_ref[...], precision=HIGHEST, preferred_element_type=F32) + b2_ref[...]
    h = jnp.sin(fr_ref[1:2, :] * h)
    hfb = jnp.dot(h, w3_ref[...], precision=HIGHEST, preferred_element_type=F32)
    win = win_ref[...]
    hf = hfb[:, :c] * win
    hb = hfb[:, c:] * win
    row = lax.broadcasted_iota(jnp.int32, hf.shape, 0)
    g = hf + jnp.where(row == 0, 0.0, hb)
    kre = jnp.dot(cm_ref[...], g, precision=HIGHEST, preferred_element_type=F32)
    kim = jnp.dot(sm_ref[...], hb - hf, precision=HIGHEST, preferred_element_type=F32)
    kn = jnp.dot(alt_ref[...], g, precision=HIGHEST, preferred_element_type=F32)[0:1, :]
    kre_ref[...] = kre
    kim_ref[...] = jnp.where(row == 0, kn, kim)


def _hyena_filter(L, z, window, cm, sm, alt8, w1, b1, w2, b2, w3, freq):
    w1p = jnp.pad(w1, ((0, POS_PAD - HYENA_POS_DIM), (0, 0)))
    return pl.pallas_call(
        _filter_kernel,
        out_shape=(jax.ShapeDtypeStruct((L, GROUP_WIDTH), F32), jax.ShapeDtypeStruct((L, GROUP_WIDTH), F32)),
        compiler_params=pltpu.CompilerParams(vmem_limit_bytes=56 * MIB),
        name="hyena_filter",
    )(z, w1p, b1.reshape(1, -1), w2, b2.reshape(1, -1), w3, freq, window, cm, sm, alt8)


def _short_conv(a, w_ref, b_ref, row, L):
    prev = jnp.where(row == 0, 0.0, pltpu.roll(a, 1, 0))
    nxt = jnp.where(row == L - 1, 0.0, pltpu.roll(a, L - 1, 0))
    return b_ref[...] + prev * w_ref[0:1, :] + a * w_ref[1:2, :] + nxt * w_ref[2:3, :]


def _hyena_kernel(a0_ref, a1_ref, a2_ref, w0_ref, w1_ref, w2_ref, b0_ref, b1_ref, b2_ref, kre_ref, kim_ref,
                  skip_ref, fwd_ref, inv_ref, o_ref, *, L):
    row = lax.broadcasted_iota(jnp.int32, (L, HY_CC), 0)
    x1 = _short_conv(a1_ref[...], w1_ref, b1_ref, row, L)
    hv = _short_conv(a2_ref[...], w2_ref, b2_ref, row, L)
    u = hv * x1
    spec = jnp.dot(fwd_ref[...], u.astype(BF16), preferred_element_type=F32)
    ure, uim = spec[:L], spec[L:]
    kre, kim = kre_ref[...], kim_ref[...]
    first = row == 0
    yre = ure * kre - jnp.where(first, 0.0, uim * kim)
    yim = jnp.where(first, uim * kim, ure * kim + uim * kre)
    yspec = jnp.concatenate([yre, yim], axis=0).astype(BF16)
    y = jnp.dot(inv_ref[...], yspec, preferred_element_type=F32)
    x0 = _short_conv(a0_ref[...], w0_ref, b0_ref, row, L)
    o_ref[...] = (x0 * (y + u * skip_ref[...])).astype(BF16)


def _hyena(proj, row_off, B, L, conv_w, conv_b, kre, kim, skip, fwd, inv):
    ncc = GROUP_WIDTH // HY_CC

    def a_spec(part):
        return pl.BlockSpec((L, HY_CC), lambda b, cc: (row_off + b, part * ncc + cc))

    def w_spec(part, rows):
        return pl.BlockSpec((rows, HY_CC), lambda b, cc: (0, part * ncc + cc))

    chan = pl.BlockSpec((L, HY_CC), lambda b, cc: (0, cc))
    conv_b = conv_b.reshape(1, -1)
    return pl.pallas_call(
        functools.partial(_hyena_kernel, L=L),
        grid=(B, ncc),
        in_specs=[a_spec(0), a_spec(1), a_spec(2), w_spec(0, 3), w_spec(1, 3), w_spec(2, 3),
                  w_spec(0, 1), w_spec(1, 1), w_spec(2, 1), chan, chan,
                  pl.BlockSpec((1, HY_CC), lambda b, cc: (0, cc)),
                  pl.BlockSpec((2 * L, L), lambda b, cc: (0, 0)),
                  pl.BlockSpec((L, 2 * L), lambda b, cc: (0, 0))],
        out_specs=pl.BlockSpec((L, HY_CC), lambda b, cc: (b, cc)),
        out_shape=jax.ShapeDtypeStruct((B * L, GROUP_WIDTH), BF16),
        compiler_params=_cp(("arbitrary", "arbitrary"), 56),
        name="hyena_conv",
    )(proj, proj, proj, conv_w, conv_w, conv_w, conv_b, conv_b, conv_b, kre, kim, skip.reshape(1, -1), fwd, inv)


def _sgu_kernel(u_ref, v_ref, g_ref, b_ref, w_ref, bs_ref, o_ref, *, L):
    v = v_ref[...]
    mu = jnp.mean(v, axis=-1, keepdims=True)
    xc = v - mu
    vn = xc * lax.rsqrt(jnp.mean(xc * xc, axis=-1, keepdims=True) + EPS) * g_ref[...] + b_ref[...]
    vnb = vn.astype(BF16)
    hc = GROUP_WIDTH // SGU_HEADS
    for h in range(SGU_HEADS):
        wh = w_ref[h].astype(BF16)
        bias = bs_ref[h]
        cols = slice(h * hc, (h + 1) * hc)
        for n in range(L // CHUNK):
            rows = slice(n * CHUNK, (n + 1) * CHUNK)
            mixed = jnp.dot(wh, vnb[rows, cols], preferred_element_type=F32) + bias
            o_ref[rows, cols] = (u_ref[rows, cols] * mixed).astype(BF16)


def _sgu(proj, row_off, B, L, g, b, w_s, b_s):
    hc = GROUP_WIDTH // SGU_HEADS
    bias = jnp.broadcast_to(b_s[:, :, None], (SGU_HEADS, CHUNK, hc))
    return pl.pallas_call(
        functools.partial(_sgu_kernel, L=L),
        grid=(B,),
        in_specs=[pl.BlockSpec((L, GROUP_WIDTH), lambda i: (row_off + i, 3)),
                  pl.BlockSpec((L, GROUP_WIDTH), lambda i: (row_off + i, 4)),
                  pl.BlockSpec((1, GROUP_WIDTH), lambda i: (0, 0)),
                  pl.BlockSpec((1, GROUP_WIDTH), lambda i: (0, 0)),
                  pl.BlockSpec((SGU_HEADS, CHUNK, CHUNK), lambda i: (0, 0, 0)),
                  pl.BlockSpec((SGU_HEADS, CHUNK, hc), lambda i: (0, 0, 0))],
        out_specs=pl.BlockSpec((L, GROUP_WIDTH), lambda i: (i, 0)),
        out_shape=jax.ShapeDtypeStruct((B * L, GROUP_WIDTH), BF16),
        compiler_params=_cp(("arbitrary",), 40),
        name="sgu",
    )(proj, proj, g.reshape(1, -1), b.reshape(1, -1), w_s, bias)


def _rope_tables(L, dh, width):
    quarter = dh // 4
    pos = np.arange(L)
    rowp = (pos // GRID_W).astype(np.float32)
    colp = (pos % GRID_W).astype(np.float32)
    inv_freq = (ROPE_THETA ** (-np.arange(quarter, dtype=np.float32) / quarter)).astype(np.float32)
    ar = (rowp[:, None] * inv_freq[None, :]).astype(np.float64)
    ac = (colp[:, None] * inv_freq[None, :]).astype(np.float64)
    cos = np.concatenate([np.cos(ar), np.cos(ar), np.cos(ac), np.cos(ac)], axis=-1)
    sin = np.concatenate([-np.sin(ar), np.sin(ar), -np.sin(ac), np.sin(ac)], axis=-1)
    reps = width // dh
    return jnp.asarray(np.tile(cos, (1, reps)), F32), jnp.asarray(np.tile(sin, (1, reps)), F32)


def _rope(x, cos, sin, quarter):
    w = x.shape[-1]
    lane = lax.broadcasted_iota(jnp.int32, x.shape, 1)
    partner = jnp.where((lane & quarter) != 0, pltpu.roll(x, quarter, 1), pltpu.roll(x, w - quarter, 1))
    return x * cos + partner * sin


def _head_rms(x, g, heads, dh):
    outs = []
    for h in range(heads):
        xh = x[:, h * dh:(h + 1) * dh]
        outs.append(xh * lax.rsqrt(jnp.mean(xh * xh, axis=-1, keepdims=True) + EPS) * g)
    return jnp.concatenate(outs, axis=-1)


def _softmax_pv(q, k, v):
    s = lax.dot_general(q, k, (((1,), (1,)), ((), ())), preferred_element_type=F32)
    e = jnp.exp(s - jnp.max(s, axis=-1, keepdims=True))
    inv_l = 1.0 / jnp.sum(e, axis=-1, keepdims=True)
    return jnp.dot(e.astype(BF16), v, preferred_element_type=F32), inv_l


def _gqa_body(q, k_ref, v_ref, o_ref, L):
    dh = ATTN_HEAD_DIM
    group = ATTN_HEADS // ATTN_KV_HEADS
    qs = (q * dh ** -0.5).astype(BF16)
    qc = min(Q_CHUNK, L)
    for h in range(ATTN_HEADS):
        kv = h // group
        kh = k_ref[:, kv * dh:(kv + 1) * dh]
        vh = v_ref[:, kv * dh:(kv + 1) * dh]
        for c in range(L // qc):
            rows = slice(c * qc, (c + 1) * qc)
            pv, inv_l = _softmax_pv(qs[rows, h * dh:(h + 1) * dh], kh, vh)
            o_ref[rows, h * dh:(h + 1) * dh] = (pv * inv_l).astype(BF16)


def _gqa_ctx_kernel(q_ref, k_ref, v_ref, gq_ref, gk_ref, o_ref, ko_ref, vo_ref, kb_ref, vb_ref, *, L):
    q = _head_rms(q_ref[...], gq_ref[...], ATTN_HEADS, ATTN_HEAD_DIM)
    k = _head_rms(k_ref[...], gk_ref[...], ATTN_KV_HEADS, ATTN_HEAD_DIM)
    v = v_ref[...]
    ko_ref[...] = k
    vo_ref[...] = v
    kb_ref[...] = k.astype(BF16)
    vb_ref[...] = v.astype(BF16)
    _gqa_body(q, kb_ref, vb_ref, o_ref, L)


def _gqa_lat_kernel(q_ref, k_ref, v_ref, gq_ref, gk_ref, cq_ref, sq_ref, ck_ref, sk_ref, pk_ref, pv_ref, o_ref,
                    kb_ref, vb_ref, *, L, P):
    quarter = ATTN_HEAD_DIM // 4
    q = _rope(_head_rms(q_ref[...], gq_ref[...], ATTN_HEADS, ATTN_HEAD_DIM), cq_ref[...], sq_ref[...], quarter)
    k = _rope(_head_rms(k_ref[...], gk_ref[...], ATTN_KV_HEADS, ATTN_HEAD_DIM), ck_ref[...], sk_ref[...], quarter)
    kb_ref[0:P, :] = pk_ref[...].astype(BF16)
    kb_ref[P:P + L, :] = k.astype(BF16)
    vb_ref[0:P, :] = pv_ref[...].astype(BF16)
    vb_ref[P:P + L, :] = v_ref[...].astype(BF16)
    _gqa_body(q, kb_ref, vb_ref, o_ref, L)


def _gqa(proj, row_off, B, L, gq, gk, l, past=None, tables=None):
    qw = ATTN_HEADS * ATTN_HEAD_DIM
    kw = ATTN_KV_HEADS * ATTN_HEAD_DIM
    qoff = 5
    koff = (2560 + qw) // kw
    in_specs = [pl.BlockSpec((L, qw), lambda i: (row_off + i, qoff)),
                pl.BlockSpec((L, kw), lambda i: (row_off + i, koff)),
                pl.BlockSpec((L, kw), lambda i: (row_off + i, koff + 1)),
                pl.BlockSpec((1, ATTN_HEAD_DIM), lambda i: (0, 0)),
                pl.BlockSpec((1, ATTN_HEAD_DIM), lambda i: (0, 0))]
    args = [proj, proj, proj, gq.reshape(1, -1), gk.reshape(1, -1)]
    y_spec = pl.BlockSpec((L, qw), lambda i: (i, 0))
    y_shape = jax.ShapeDtypeStruct((B * L, qw), BF16)
    if past is None:
        return pl.pallas_call(
            functools.partial(_gqa_ctx_kernel, L=L),
            grid=(B,),
            in_specs=in_specs,
            out_specs=(y_spec, pl.BlockSpec((L, kw), lambda i: (i, 0)), pl.BlockSpec((L, kw), lambda i: (i, 0))),
            out_shape=(y_shape, jax.ShapeDtypeStruct((B * L, kw), F32), jax.ShapeDtypeStruct((B * L, kw), F32)),
            scratch_shapes=[pltpu.VMEM((L, kw), BF16), pltpu.VMEM((L, kw), BF16)],
            compiler_params=_cp(("arbitrary",), 40),
            name="gqa_ctx",
        )(*args)
    pk, pv = past
    P = pk.shape[2]
    cq, sq, ck, sk = tables
    const = lambda w: pl.BlockSpec((L, w), lambda i: (0, 0))
    cache = pl.BlockSpec((None, None, P, kw), lambda i: (i, l, 0, 0))
    return pl.pallas_call(
        functools.partial(_gqa_lat_kernel, L=L, P=P),
        grid=(B,),
        in_specs=in_specs + [const(qw), const(qw), const(kw), const(kw), cache, cache],
        out_specs=y_spec,
        out_shape=y_shape,
        scratch_shapes=[pltpu.VMEM((P + L, kw), BF16), pltpu.VMEM((P + L, kw), BF16)],
        compiler_params=_cp(("arbitrary",), 56),
        name="gqa_lat",
    )(*args, cq, sq, ck, sk, pk, pv)


def _diff_body(dq, k_ref, v_ref, dl_ref, g_ref, o_ref, L, lambda_init):
    dv = DIFF_V_DIM
    dl = dl_ref[...]
    lam = (jnp.exp(jnp.sum(dl[0:1] * dl[1:2], axis=-1, keepdims=True))
           - jnp.exp(jnp.sum(dl[2:3] * dl[3:4], axis=-1, keepdims=True)) + lambda_init)
    qs = dq * DIFF_QK_DIM ** -0.5
    first_map = lax.broadcasted_iota(jnp.int32, (1, dv), 1) < DIFF_QK_DIM
    qc = min(Q_CHUNK, L)
    for h in range(DIFF_HEADS):
        cols = slice(h * dv, (h + 1) * dv)
        kh = k_ref[:, cols]
        vh = v_ref[:, cols]
        for c in range(L // qc):
            rows = slice(c * qc, (c + 1) * qc)
            qh = qs[rows, cols]
            pv0, inv0 = _softmax_pv(jnp.where(first_map, qh, 0.0).astype(BF16), kh, vh)
            pv1, inv1 = _softmax_pv(jnp.where(first_map, 0.0, qh).astype(BF16), kh, vh)
            o = pv0 * inv0 - pv1 * (lam * inv1)
            r = lax.rsqrt(jnp.mean(o * o, axis=-1, keepdims=True) + EPS)
            o_ref[rows, cols] = ((o * r) * g_ref[...] * (1.0 - lambda_init)).astype(BF16)


def _diff_ctx_kernel(q_ref, k_ref, v_ref, dl_ref, g_ref, o_ref, ko_ref, vo_ref, kb_ref, vb_ref, *, L, lambda_init):
    k = k_ref[...]
    v = v_ref[...]
    ko_ref[...] = k
    vo_ref[...] = v
    kb_ref[...] = k.astype(BF16)
    vb_ref[...] = v.astype(BF16)
    _diff_body(q_ref[...], kb_ref, vb_ref, dl_ref, g_ref, o_ref, L, lambda_init)


def _diff_lat_kernel(q_ref, k_ref, v_ref, dl_ref, g_ref, cos_ref, sin_ref, pk_ref, pv_ref, o_ref, kb_ref, vb_ref,
                     *, L, P, lambda_init):
    quarter = DIFF_QK_DIM // 4
    cos, sin = cos_ref[...], sin_ref[...]
    q = _rope(q_ref[...], cos, sin, quarter)
    k = _rope(k_ref[...], cos, sin, quarter)
    kb_ref[0:P, :] = pk_ref[...].astype(BF16)
    kb_ref[P:P + L, :] = k.astype(BF16)
    vb_ref[0:P, :] = pv_ref[...].astype(BF16)
    vb_ref[P:P + L, :] = v_ref[...].astype(BF16)
    _diff_body(q, kb_ref, vb_ref, dl_ref, g_ref, o_ref, L, lambda_init)


def _diff(proj, row_off, B, L, dl, g, l, past=None, tables=None):
    w = GROUP_WIDTH
    lambda_init = 0.8 - 0.6 * math.exp(-0.3 * l)
    in_specs = [pl.BlockSpec((L, w), lambda i: (row_off + i, 7)),
                pl.BlockSpec((L, w), lambda i: (row_off + i, 8)),
                pl.BlockSpec((L, w), lambda i: (row_off + i, 9)),
                pl.BlockSpec((4, DIFF_QK_DIM), lambda i: (0, 0)),
                pl.BlockSpec((1, DIFF_V_DIM), lambda i: (0, 0))]
    args = [proj, proj, proj, dl, g.reshape(1, -1)]
    y_spec = pl.BlockSpec((L, w), lambda i: (i, 0))
    y_shape = jax.ShapeDtypeStruct((B * L, w), BF16)
    if past is None:
        return pl.pallas_call(
            functools.partial(_diff_ctx_kernel, L=L, lambda_init=lambda_init),
            grid=(B,),
            in_specs=in_specs,
            out_specs=(y_spec, pl.BlockSpec((L, w), lambda i: (i, 0)), pl.BlockSpec((L, w), lambda i: (i, 0))),
            out_shape=(y_shape, jax.ShapeDtypeStruct((B * L, w), F32), jax.ShapeDtypeStruct((B * L, w), F32)),
            scratch_shapes=[pltpu.VMEM((L, w), BF16), pltpu.VMEM((L, w), BF16)],
            compiler_params=_cp(("arbitrary",), 40),
            name="diff_ctx",
        )(*args)
    pk, pv = past
    P = pk.shape[2]
    cos, sin = tables
    const = pl.BlockSpec((L, w), lambda i: (0, 0))
    cache = pl.BlockSpec((None, None, P, w), lambda i: (i, l, 0, 0))
    return pl.pallas_call(
        functools.partial(_diff_lat_kernel, L=L, P=P, lambda_init=lambda_init),
        grid=(B,),
        in_specs=in_specs + [const, const, cache, cache],
        out_specs=y_spec,
        out_shape=y_shape,
        scratch_shapes=[pltpu.VMEM((P + L, w), BF16), pltpu.VMEM((P + L, w), BF16)],
        compiler_params=_cp(("arbitrary",), 56),
        name="diff_lat",
    )(*args, cos, sin, pk, pv)


def _pack_halves(lo, hi):
    lo_bits = pltpu.bitcast(lo.astype(BF16).astype(F32), jnp.uint32) >> 16
    hi_bits = pltpu.bitcast(hi.astype(BF16).astype(F32), jnp.uint32) & jnp.uint32(0xFFFF0000)
    return lo_bits | hi_bits


def _unpack_halves(words):
    lo = pltpu.bitcast(words << 16, F32)
    hi = pltpu.bitcast(words & jnp.uint32(0xFFFF0000), F32)
    return lo, hi


def _route_rows(logits, prefix_base, tri_ref):
    lane = lax.broadcasted_iota(jnp.int32, logits.shape, 1)
    neg = jnp.float32(-jnp.inf)
    work = jnp.where(lane < N_EXPERTS, logits, neg)
    sels, vals, idxs = [], [], []
    for _ in range(TOP_K):
        mk = jnp.max(work, axis=-1, keepdims=True)
        ik = jnp.min(jnp.where(work == mk, lane, ROUTER_PAD), axis=-1, keepdims=True)
        sel = lane == ik
        work = jnp.where(sel, neg, work)
        sels.append(sel)
        vals.append(mk)
        idxs.append(ik)
    es = [jnp.exp(v - vals[0]) for v in vals]
    inv = 1.0 / (es[0] + es[1] + es[2] + es[3])
    member = jnp.zeros(logits.shape, F32)
    for sel in sels:
        member = member + jnp.where(sel, 1.0, 0.0)
    prefix = jnp.dot(tri_ref[...], member.astype(BF16), preferred_element_type=F32) + prefix_base
    ri = jnp.zeros(logits.shape, jnp.int32)
    rg = jnp.zeros(logits.shape, F32)
    for k in range(TOP_K):
        rank = jnp.sum(jnp.where(sels[k], prefix, 0.0), axis=-1, keepdims=True).astype(jnp.int32)
        ri = jnp.where(lane == k, idxs[k], ri)
        ri = jnp.where(lane == TOP_K + k, rank, ri)
        rg = jnp.where(lane == k, es[k] * inv, rg)
    return ri, rg, jnp.sum(member, axis=0, keepdims=True)


def _out_proj_kernel(pa_ref, pb_ref, pc_ref, pd_ref, sa_ref, sb_ref, sc_ref, sd_ref, w_ref, x_ref, mod_ref, ng_ref,
                     wr_ref, br_ref, tri_ref, x1_ref, hu_ref, rt_ref, rg_ref, cnt_ref, cat_ref, carry_ref, *, nbp):
    i = pl.program_id(0)

    @pl.when(i == 0)
    def _():
        carry_ref[...] = jnp.zeros_like(carry_ref)

    gw = GROUP_WIDTH

    @pl.when(i < nbp)
    def _():
        for m_, ref in enumerate((pa_ref, pb_ref, pc_ref, pd_ref)):
            cat_ref[:, m_ * gw:(m_ + 1) * gw] = ref[...]

    @pl.when(i >= nbp)
    def _():
        for m_, ref in enumerate((sa_ref, sb_ref, sc_ref, sd_ref)):
            cat_ref[:, m_ * gw:(m_ + 1) * gw] = ref[...]

    m = jnp.dot(cat_ref[...], w_ref[...], preferred_element_type=F32)
    mn = m * lax.rsqrt(jnp.mean(m * m, axis=-1, keepdims=True) + EPS) * ng_ref[1:2, :]
    x1 = x_ref[...] + mod_ref[0, 2:3, :] * mn
    x1_ref[...] = x1
    hn = x1 * lax.rsqrt(jnp.mean(x1 * x1, axis=-1, keepdims=True) + EPS) * ng_ref[2:3, :]
    h = hn * (1.0 + mod_ref[0, 4:5, :]) + mod_ref[0, 3:4, :]
    half = D_MODEL // 2
    hu_ref[...] = _pack_halves(h[:, :half], h[:, half:])
    hh = h.astype(BF16)
    hl = (h - hh.astype(F32)).astype(BF16)
    both = jnp.dot(hh, wr_ref[...], preferred_element_type=F32)
    low = jnp.dot(hl, wr_ref[:, :ROUTER_PAD], preferred_element_type=F32)
    logits = both[:, :ROUTER_PAD] + both[:, ROUTER_PAD:] + low + br_ref[...]
    ri, rg, counts = _route_rows(logits, carry_ref[0:1, :], tri_ref)
    rt_ref[...] = ri.astype(F32).T[:8, :].astype(jnp.int32)
    rg_ref[...] = rg
    carry_ref[...] = carry_ref[...] + counts
    cnt_ref[...] = carry_ref[...]


def _out_proj(ys_prompt, ys_latent, w, x, mod, ng, wr, br, tp, ls):
    t = x.shape[0]
    bm = OUT_BM
    nbp = tp // bm
    pspec = pl.BlockSpec((bm, GROUP_WIDTH), lambda i: (jnp.minimum(i, nbp - 1), 0))
    sspec = pl.BlockSpec((bm, GROUP_WIDTH), lambda i: (jnp.maximum(i - nbp, 0), 0))
    row = pl.BlockSpec((bm, D_MODEL), lambda i: (i, 0))
    info = pl.BlockSpec((bm, ROUTER_PAD), lambda i: (i, 0))
    tri = jnp.asarray(np.tril(np.ones((bm, bm), np.float32), -1).astype(BF16))
    return pl.pallas_call(
        functools.partial(_out_proj_kernel, nbp=nbp),
        grid=(t // bm,),
        in_specs=[pspec, pspec, pspec, pspec, sspec, sspec, sspec, sspec,
                  pl.BlockSpec((D_MODEL, D_MODEL), lambda i: (0, 0)),
                  row,
                  pl.BlockSpec((1, 6, D_MODEL), lambda i: (_mod_row(i, bm, tp, ls), 0, 0)),
                  pl.BlockSpec((4, D_MODEL), lambda i: (0, 0)),
                  pl.BlockSpec((D_MODEL, 2 * ROUTER_PAD), lambda i: (0, 0)),
                  pl.BlockSpec((1, ROUTER_PAD), lambda i: (0, 0)),
                  pl.BlockSpec((bm, bm), lambda i: (0, 0))],
        out_specs=(row, pl.BlockSpec((bm, D_MODEL // 2), lambda i: (i, 0)), pl.BlockSpec((8, bm), lambda i: (0, i)),
                   info, pl.BlockSpec((8, ROUTER_PAD), lambda i: (0, 0))),
        out_shape=(jax.ShapeDtypeStruct((t, D_MODEL), F32), jax.ShapeDtypeStruct((t, D_MODEL // 2), jnp.uint32),
                   jax.ShapeDtypeStruct((8, t), jnp.int32), jax.ShapeDtypeStruct((t, ROUTER_PAD), F32),
                   jax.ShapeDtypeStruct((8, ROUTER_PAD), F32)),
        scratch_shapes=[pltpu.VMEM((bm, D_MODEL), BF16), pltpu.VMEM((8, ROUTER_PAD), F32)],
        compiler_params=_cp(("arbitrary",), 56),
        name="out_proj",
    )(*ys_prompt, *ys_latent, w, x, mod, ng, wr, br, tri)


def _new_expert(be_ref, i):
    return jnp.logical_or(i == 0, be_ref[jnp.maximum(i - 1, 0)] != be_ref[i])


def _next_tile(nx_ref, be_ref, i, j, nj):
    nxt = nx_ref[i]
    has_run = nxt >= 0
    return jnp.logical_or(has_run, j + 1 < nj), jnp.where(has_run, j, j + 1), jnp.where(has_run, nxt, be_ref[0])


def _gm1_kernel(be_ref, nu_ref, hf_ref, nx_ref, x_ref, w_ref, bg_ref, bu_ref, o_ref, wst_ref, wbf_ref, xb_ref, sem,
                *, l, nj):
    j = pl.program_id(0)
    i = pl.program_id(1)
    used = i < nu_ref[0]
    tn = GM1_TN
    half = D_MODEL // 2

    def tile_copies(jt, e):
        gcol = pl.multiple_of(jt * tn, tn)
        ucol = pl.multiple_of(D_FF + jt * tn, tn)
        return (pltpu.make_async_copy(w_ref.at[l, e, :, pl.ds(gcol, tn)], wst_ref.at[:, pl.ds(0, tn)], sem.at[0]),
                pltpu.make_async_copy(w_ref.at[l, e, :, pl.ds(ucol, tn)], wst_ref.at[:, pl.ds(tn, tn)], sem.at[1]))

    first = jnp.logical_and(used, _new_expert(be_ref, i))

    @pl.when(first)
    def _():
        @pl.when(jnp.logical_and(j == 0, i == 0))
        def _():
            for cp in tile_copies(0, be_ref[i]):
                cp.start()

        for cp in tile_copies(j, be_ref[i]):
            cp.wait()

    def compute(rows, cast):
        lo, hi = _unpack_halves(x_ref[:rows, :])
        xb_ref[:rows, :half] = lo.astype(BF16)
        xb_ref[:rows, half:] = hi.astype(BF16)
        for c in range(tn // GM1_SUB):
            cols = slice(c * GM1_SUB, (c + 1) * GM1_SUB)
            ucols = slice(tn + c * GM1_SUB, tn + (c + 1) * GM1_SUB)
            if cast:
                wbf_ref[:, cols] = wst_ref[:, cols].astype(BF16)
                wbf_ref[:, ucols] = wst_ref[:, ucols].astype(BF16)
            g = jnp.dot(xb_ref[:rows, :], wbf_ref[:, cols], preferred_element_type=F32)
            u = jnp.dot(xb_ref[:rows, :], wbf_ref[:, ucols], preferred_element_type=F32)
            gate = jnp.minimum(g + bg_ref[:, cols], SWIGLU_LIMIT)
            up = jnp.clip(u + bu_ref[:, cols], -SWIGLU_LIMIT, SWIGLU_LIMIT)
            glu = gate * jax.nn.sigmoid(gate * SWIGLU_ALPHA)
            o_ref[:rows, cols] = ((up + 1.0) * glu).astype(BF16)
        if rows < MOE_BM:
            o_ref[rows:, :] = jnp.zeros((MOE_BM - rows, tn), BF16)

    small = hf_ref[i] != 0
    for cast in (False, True):
        for rows in (MOE_BM, MOE_BM // 2):
            pick = jnp.logical_and(first if cast else jnp.logical_and(used, jnp.logical_not(first)),
                                   small if rows < MOE_BM else jnp.logical_not(small))

            @pl.when(pick)
            def _(rows=rows, cast=cast):
                compute(rows, cast)

    @pl.when(first)
    def _():
        exists, jn, en = _next_tile(nx_ref, be_ref, i, j, nj)

        @pl.when(exists)
        def _():
            for cp in tile_copies(jn, en):
                cp.start()

    @pl.when(jnp.logical_not(used))
    def _():
        o_ref[...] = jnp.zeros_like(o_ref)


def _gm1(layout, xs, w_gu, b_gu, l):
    rows = xs.shape[0]
    nb = rows // MOE_BM
    nj = D_FF // GM1_TN
    b_gu = b_gu.reshape(DEPTH, N_EXPERTS, 1, 2 * D_FF)
    last = lambda i, nu: jnp.minimum(i, nu[0] - 1)
    grid_spec = pltpu.PrefetchScalarGridSpec(
        num_scalar_prefetch=4,
        grid=(nj, nb),
        in_specs=[
            pl.BlockSpec((MOE_BM, D_MODEL // 2), lambda j, i, be, nu, hf, nx: (last(i, nu), 0)),
            pl.BlockSpec(memory_space=pl.ANY),
            pl.BlockSpec((None, None, 1, GM1_TN), lambda j, i, be, nu, hf, nx: (l, be[i], 0, j)),
            pl.BlockSpec((None, None, 1, GM1_TN), lambda j, i, be, nu, hf, nx: (l, be[i], 0, nj + j)),
        ],
        out_specs=pl.BlockSpec((MOE_BM, GM1_TN), lambda j, i, be, nu, hf, nx: (i, j)),
        scratch_shapes=[pltpu.VMEM((D_MODEL, 2 * GM1_TN), F32), pltpu.VMEM((D_MODEL, 2 * GM1_TN), BF16),
                        pltpu.VMEM((MOE_BM, D_MODEL), BF16), pltpu.SemaphoreType.DMA((2,))],
    )
    return pl.pallas_call(
        functools.partial(_gm1_kernel, l=l, nj=nj),
        grid_spec=grid_spec,
        out_shape=jax.ShapeDtypeStruct((rows, D_FF), BF16),
        compiler_params=_cp(("arbitrary", "arbitrary"), 56),
        name="moe_gate_up",
    )(*layout, xs, w_gu, b_gu, b_gu)


def _gm2_kernel(be_ref, nu_ref, hf_ref, nx_ref, a_ref, w_ref, b_ref, o_ref, wst_ref, wbf_ref, sem, *, l, nj):
    j = pl.program_id(0)
    i = pl.program_id(1)
    used = i < nu_ref[0]
    tn = GM2_TN
    half = tn // 2

    def tile_copy(jt, e):
        col = pl.multiple_of(jt * tn, tn)
        return pltpu.make_async_copy(w_ref.at[l, e, :, pl.ds(col, tn)], wst_ref, sem)

    first = jnp.logical_and(used, _new_expert(be_ref, i))

    @pl.when(first)
    def _():
        @pl.when(jnp.logical_and(j == 0, i == 0))
        def _():
            tile_copy(0, be_ref[i]).start()

        tile_copy(j, be_ref[i]).wait()

    def compute(rows, cast):
        if cast:
            wbf_ref[:, :half] = wst_ref[:, :half].astype(BF16)
        lo = jnp.dot(a_ref[:rows, :], wbf_ref[:, :half], preferred_element_type=F32) + b_ref[:, :half]
        if cast:
            wbf_ref[:, half:] = wst_ref[:, half:].astype(BF16)
        hi = jnp.dot(a_ref[:rows, :], wbf_ref[:, half:], preferred_element_type=F32) + b_ref[:, half:]
        o_ref[:rows, :] = _pack_halves(lo, hi)
        if rows < MOE_BM:
            o_ref[rows:, :] = jnp.zeros((MOE_BM - rows, half), jnp.uint32)

    small = hf_ref[i] != 0
    for cast in (False, True):
        for rows in (MOE_BM, MOE_BM // 2):
            pick = jnp.logical_and(first if cast else jnp.logical_and(used, jnp.logical_not(first)),
                                   small if rows < MOE_BM else jnp.logical_not(small))

            @pl.when(pick)
            def _(rows=rows, cast=cast):
                compute(rows, cast)

    @pl.when(first)
    def _():
        exists, jn, en = _next_tile(nx_ref, be_ref, i, j, nj)

        @pl.when(exists)
        def _():
            tile_copy(jn, en).start()

    @pl.when(jnp.logical_not(used))
    def _():
        o_ref[...] = jnp.zeros_like(o_ref)


def _gm2(layout, act, w_dn, b_dn, l):
    rows = act.shape[0]
    nb = rows // MOE_BM
    nj = D_MODEL // GM2_TN
    b_dn = b_dn.reshape(DEPTH, N_EXPERTS, 1, D_MODEL)
    last = lambda i, nu: jnp.minimum(i, nu[0] - 1)
    grid_spec = pltpu.PrefetchScalarGridSpec(
        num_scalar_prefetch=4,
        grid=(nj, nb),
        in_specs=[
            pl.BlockSpec((MOE_BM, D_FF), lambda j, i, be, nu, hf, nx: (last(i, nu), 0)),
            pl.BlockSpec(memory_space=pl.ANY),
            pl.BlockSpec((None, None, 1, GM2_TN), lambda j, i, be, nu, hf, nx: (l, be[i], 0, j)),
        ],
        out_specs=pl.BlockSpec((MOE_BM, GM2_TN // 2), lambda j, i, be, nu, hf, nx: (i, j)),
        scratch_shapes=[pltpu.VMEM((D_FF, GM2_TN), F32), pltpu.VMEM((D_FF, GM2_TN), BF16),
                        pltpu.SemaphoreType.DMA(())],
    )
    return pl.pallas_call(
        functools.partial(_gm2_kernel, l=l, nj=nj),
        grid_spec=grid_spec,
        out_shape=jax.ShapeDtypeStruct((rows, D_MODEL // 2), jnp.uint32),
        compiler_params=_cp(("arbitrary", "arbitrary"), 56),
        name="moe_down",
    )(*layout, act, w_dn, b_dn)


def _block_layout(counts, n_blocks):
    padded = (counts + MOE_BM - 1) // MOE_BM * MOE_BM
    pad_end = jnp.cumsum(padded)
    pad_start = (pad_end - padded).astype(jnp.int32)
    starts = jnp.arange(n_blocks, dtype=jnp.int32) * MOE_BM
    block_e = jnp.minimum(jnp.sum(pad_end[None, :] <= starts[:, None], axis=1), N_EXPERTS - 1).astype(jnp.int32)
    n_used = (pad_end[-1] // MOE_BM).astype(jnp.int32).reshape(1)
    valid = jnp.clip(counts[block_e] - (starts - pad_start[block_e]), 0, MOE_BM)
    block_half = (valid <= MOE_BM // 2).astype(jnp.int32)
    idx = jnp.arange(N_EXPERTS, dtype=jnp.int32)
    later = jnp.where((counts > 0)[None, :] & (idx[None, :] > idx[:, None]), idx[None, :], N_EXPERTS)
    next_e = jnp.min(later, axis=1)
    block_next = jnp.where(next_e == N_EXPERTS, -1, next_e)[block_e].astype(jnp.int32)
    return pad_start, (block_e, n_used, block_half, block_next)


def _dispatch_kernel(dest_ref, ps_ref, cn_ref, h_ref, xs_ref, zero_ref, sem, zsem, *, bt):
    i = pl.program_id(0)
    base = i * bt
    nt = pl.num_programs(0) * bt

    def zero_rows(issue):
        def per_expert(e, carry):
            start = ps_ref[e] + cn_ref[e]
            end = (start + MOE_BM - 1) // MOE_BM * MOE_BM
            head_end = jnp.minimum((start + 7) // 8 * 8, end)

            def single(r, c):
                cp = pltpu.make_async_copy(zero_ref.at[pl.ds(0, 1)], xs_ref.at[pl.ds(r, 1)], zsem)
                cp.start() if issue else cp.wait()
                return c

            def chunk(q, c):
                r = pl.multiple_of(head_end + q * 8, 8)
                cp = pltpu.make_async_copy(zero_ref.at[pl.ds(0, 8)], xs_ref.at[pl.ds(r, 8)], zsem)
                cp.start() if issue else cp.wait()
                return c

            lax.fori_loop(start, head_end, single, 0)
            lax.fori_loop(0, (end - head_end) // 8, chunk, 0)
            return carry

        lax.fori_loop(0, N_EXPERTS, per_expert, 0)

        last = N_EXPERTS - 1
        used_rows = ps_ref[last] + (cn_ref[last] + MOE_BM - 1) // MOE_BM * MOE_BM

        def block(b, c):
            r = pl.multiple_of(b * MOE_BM, MOE_BM)
            cp = pltpu.make_async_copy(zero_ref, xs_ref.at[pl.ds(r, MOE_BM)], zsem)
            cp.start() if issue else cp.wait()
            return c

        lax.fori_loop(used_rows // MOE_BM, xs_ref.shape[0] // MOE_BM, block, 0)

    @pl.when(i == 0)
    def _():
        zero_ref[...] = jnp.zeros_like(zero_ref)
        zero_rows(True)

    def group(g, carry):
        r0 = pl.multiple_of(g * 8, 8)
        rows8 = h_ref.at[pl.ds(r0, 8)]
        for r in range(8):
            for k in range(TOP_K):
                dst = dest_ref[k * nt + base + r0 + r]
                pltpu.make_async_copy(rows8.at[pl.ds(r, 1)], xs_ref.at[pl.ds(dst, 1)], sem).start(priority=k % 2)
        return carry

    lax.fori_loop(0, bt // 8, group, 0)
    for _ in range(TOP_K):
        pltpu.make_async_copy(h_ref, xs_ref.at[pl.ds(0, bt)], sem).wait()

    @pl.when(i == pl.num_programs(0) - 1)
    def _():
        zero_rows(False)


def _dispatch(dest, pad_start, counts, hu, rows):
    t, w = hu.shape
    bt = DISPATCH_BT
    grid_spec = pltpu.PrefetchScalarGridSpec(
        num_scalar_prefetch=3,
        grid=(t // bt,),
        in_specs=[pl.BlockSpec((bt, w), lambda i, dst, ps, cn: (i, 0))],
        out_specs=pl.BlockSpec(memory_space=pl.ANY),
        scratch_shapes=[pltpu.VMEM((MOE_BM, w), jnp.uint32), pltpu.SemaphoreType.DMA(()),
                        pltpu.SemaphoreType.DMA(())],
    )
    return pl.pallas_call(
        functools.partial(_dispatch_kernel, bt=bt),
        grid_spec=grid_spec,
        out_shape=jax.ShapeDtypeStruct((rows, w), jnp.uint32),
        compiler_params=pltpu.CompilerParams(dimension_semantics=("arbitrary",), has_side_effects=True),
        name="moe_dispatch",
    )(dest, pad_start, counts, hu)


def _finish_kernel(dest_ref, y_ref, rg_ref, x_ref, mod_ref, g_ref, *rest, bt, nbp):
    if nbp is None:
        o_ref, buf, sem = rest
    else:
        op_ref, os_ref, buf, sem = rest
    i = pl.program_id(0)
    n = pl.num_programs(0)
    slot = i % 2
    q = GM2_TN // 2
    nj = D_MODEL // GM2_TN

    def issue8(blk, to_slot, r0):
        for k in range(TOP_K):
            rows8 = buf.at[to_slot, k, pl.ds(r0, 8)]
            for r in range(8):
                src = dest_ref[k * (n * bt) + blk * bt + r0 + r]
                pltpu.make_async_copy(y_ref.at[pl.ds(src, 1)], rows8.at[pl.ds(r, 1)], sem.at[to_slot]).start(
                    priority=k % 2)

    def issue_block(blk, to_slot):
        lax.fori_loop(0, bt // 8, lambda g, c: (issue8(blk, to_slot, pl.multiple_of(g * 8, 8)), c)[1], 0)

    def combine_rows(out_ref, r0):
        rows = pl.ds(r0, FIN_ROWS)
        rg = rg_ref[rows, :]
        parts = [None] * (2 * nj)
        for k in range(TOP_K):
            gate = rg[:, k:k + 1]
            for j in range(nj):
                lo, hi = _unpack_halves(buf[slot, k, rows, j * q:(j + 1) * q])
                for idx, val in ((2 * j, lo), (2 * j + 1, hi)):
                    parts[idx] = gate * val if k == 0 else parts[idx] + gate * val
        f = jnp.concatenate(parts, axis=-1)
        fn = f * lax.rsqrt(jnp.mean(f * f, axis=-1, keepdims=True) + EPS) * g_ref[...]
        out_ref[rows, :] = x_ref[rows, :] + mod_ref[0, 5:6, :] * fn

    def combine_all(out_ref):
        def chunk(c, carry):
            combine_rows(out_ref, pl.multiple_of(c * FIN_ROWS, FIN_ROWS))
            return carry

        lax.fori_loop(0, bt // FIN_ROWS, chunk, 0, unroll=FIN_UNROLL)

    @pl.when(i == 0)
    def _():
        issue_block(0, 0)

    @pl.when(i + 1 < n)
    def _():
        issue_block(i + 1, (i + 1) % 2)

    for k in range(TOP_K):
        pltpu.make_async_copy(y_ref.at[pl.ds(0, bt)], buf.at[slot, k], sem.at[slot]).wait()

    if nbp is None:
        combine_all(o_ref)
    else:
        @pl.when(i < nbp)
        def _():
            combine_all(op_ref)

        @pl.when(i >= nbp)
        def _():
            combine_all(os_ref)


def _finish(dest, yu, rg, x1, mod, g, tp, ls, split):
    t = x1.shape[0]
    bt = FIN_BM
    row = lambda w: pl.BlockSpec((bt, w), lambda i, dst: (i, 0))
    if split:
        nbp = tp // bt
        out_specs = (pl.BlockSpec((bt, D_MODEL), lambda i, dst: (jnp.minimum(i, nbp - 1), 0)),
                     pl.BlockSpec((bt, D_MODEL), lambda i, dst: (jnp.maximum(i - nbp, 0), 0)))
        out_shape = (jax.ShapeDtypeStruct((tp, D_MODEL), F32), jax.ShapeDtypeStruct((t - tp, D_MODEL), F32))
    else:
        nbp = None
        out_specs = row(D_MODEL)
        out_shape = jax.ShapeDtypeStruct((t, D_MODEL), F32)
    grid_spec = pltpu.PrefetchScalarGridSpec(
        num_scalar_prefetch=1,
        grid=(t // bt,),
        in_specs=[pl.BlockSpec(memory_space=pl.ANY),
                  row(ROUTER_PAD),
                  row(D_MODEL),
                  pl.BlockSpec((1, 6, D_MODEL), lambda i, dst: (_mod_row(i, bt, tp, ls), 0, 0)),
                  pl.BlockSpec((1, D_MODEL), lambda i, dst: (0, 0))],
        out_specs=out_specs,
        scratch_shapes=[pltpu.VMEM((2, TOP_K, bt, D_MODEL // 2), jnp.uint32), pltpu.SemaphoreType.DMA((2,))],
    )
    return pl.pallas_call(
        functools.partial(_finish_kernel, bt=bt, nbp=nbp),
        grid_spec=grid_spec,
        out_shape=out_shape,
        compiler_params=_cp(("arbitrary",), 48),
        name="moe_finish",
    )(dest, yu, rg, x1, mod, g)


def _moe(hu, rt, rg, cnt, x1, mod, g, w_gu, b_gu, w_dn, b_dn, l, tp, ls, split=False):
    t = hu.shape[0]
    n_blocks = -(-t * TOP_K // MOE_BM) + N_EXPERTS
    counts = cnt[0, :N_EXPERTS].astype(jnp.int32)
    pad_start, layout = _block_layout(counts, n_blocks)
    first = jnp.where(rt[:TOP_K, :, None] == jnp.arange(N_EXPERTS, dtype=jnp.int32), pad_start, 0)
    dest = (jnp.sum(first, axis=-1) + rt[TOP_K:2 * TOP_K]).reshape(-1)
    xs = _dispatch(dest, pad_start, counts, hu, n_blocks * MOE_BM)
    act = _gm1(layout, xs, w_gu, b_gu, l)
    yu = _gm2(layout, act, w_dn, b_dn, l)
    return _finish(dest, yu, rg, x1, mod, g, tp, ls, split)


def kernel(x_prompt, x_sample, cache_attn_k, cache_attn_v, cache_diff_k, cache_diff_v, c, c_ctx, w_mod, b_mod, norm_g, w_in, w_out, hy_conv_w, hy_conv_b, hy_ffn_w1, hy_ffn_b1, hy_ffn_w2, hy_ffn_b2, hy_ffn_w3, hy_sin_freq, hy_skip, sgu_norm_g, sgu_norm_b, sgu_w, sgu_b, attn_q_g, attn_k_g, diff_lambda, diff_norm_g, w_router, b_router, w_gate_up, b_gate_up, w_down, b_down):
    bp, lp, d = x_prompt.shape
    bs, ls, _ = x_sample.shape
    tp = bp * lp
    past = cache_attn_k.shape[2]
    x = jnp.concatenate([x_prompt.reshape(tp, d), x_sample.reshape(bs * ls, d)], axis=0)

    cond = jnp.concatenate([c_ctx[None, :], c, jnp.zeros((COND_ROWS - 1 - bs, d), F32)], axis=0)
    mod = _modulation(cond, w_mod, b_mod).reshape(DEPTH, COND_ROWS, 6, d)

    w_in_b = w_in.astype(BF16)
    w_out_b = w_out.astype(BF16)
    wr_f = jnp.pad(w_router, ((0, 0), (0, 0), (0, ROUTER_PAD - N_EXPERTS)))
    wr_hi = wr_f.astype(BF16)
    wr = jnp.concatenate([wr_hi, (wr_f - wr_hi.astype(F32)).astype(BF16)], axis=-1)
    br = jnp.pad(b_router, ((0, 0), (0, ROUTER_PAD - N_EXPERTS))).reshape(DEPTH, 1, ROUTER_PAD)

    pk = cache_attn_k.reshape(bs, DEPTH, past, -1)
    pv = cache_attn_v.reshape(bs, DEPTH, past, -1)
    pdk = cache_diff_k.reshape(bs, DEPTH, past, -1)
    pdv = cache_diff_v.reshape(bs, DEPTH, past, -1)

    groups = []
    for L in (lp, ls):
        cm, sm, alt8, fwd, inv = _dft_tables(L)
        z, window = _filter_consts(L)
        groups.append((L, cm, sm, alt8, fwd, inv, z, window))
    qw = ATTN_HEADS * ATTN_HEAD_DIM
    kw = ATTN_KV_HEADS * ATTN_HEAD_DIM
    attn_tables = _rope_tables(ls, ATTN_HEAD_DIM, qw) + _rope_tables(ls, ATTN_HEAD_DIM, kw)
    diff_tables = _rope_tables(ls, DIFF_QK_DIM, GROUP_WIDTH)

    new_k, new_v, new_dk, new_dv = [], [], [], []
    for l in range(DEPTH):
        ng = norm_g[l]
        proj = _in_proj(x, mod[l], ng[0:1], w_in_b[l], tp, ls)

        ys = []
        for gi, (B, row_off) in enumerate(((bp, 0), (bs, tp // ls))):
            L, cm, sm, alt8, fwd, inv, z, window = groups[gi]
            kre, kim = _hyena_filter(L, z, window, cm, sm, alt8, hy_ffn_w1[l], hy_ffn_b1[l], hy_ffn_w2[l],
                                     hy_ffn_b2[l], hy_ffn_w3[l], hy_sin_freq[l])
            ya = _hyena(proj, row_off, B, L, hy_conv_w[l], hy_conv_b[l], kre, kim, hy_skip[l], fwd, inv)
            yb = _sgu(proj, row_off, B, L, sgu_norm_g[l], sgu_norm_b[l], sgu_w[l], sgu_b[l])
            if gi == 0:
                yc, k_ctx, v_ctx = _gqa(proj, row_off, B, L, attn_q_g[l], attn_k_g[l], l)
                yd, dk_ctx, dv_ctx = _diff(proj, row_off, B, L, diff_lambda[l], diff_norm_g[l], l)
                new_k.append(k_ctx.reshape(bp, lp, ATTN_KV_HEADS, ATTN_HEAD_DIM))
                new_v.append(v_ctx.reshape(bp, lp, ATTN_KV_HEADS, ATTN_HEAD_DIM))
                new_dk.append(dk_ctx.reshape(bp, lp, DIFF_HEADS, 2, DIFF_QK_DIM))
                new_dv.append(dv_ctx.reshape(bp, lp, DIFF_HEADS, DIFF_V_DIM))
            else:
                yc = _gqa(proj, row_off, B, L, attn_q_g[l], attn_k_g[l], l, past=(pk, pv), tables=attn_tables)
                yd = _diff(proj, row_off, B, L, diff_lambda[l], diff_norm_g[l], l, past=(pdk, pdv),
                           tables=diff_tables)
            ys.append((ya, yb, yc, yd))
        x1, hu, rt, rg, cnt = _out_proj(ys[0], ys[1], w_out_b[l], x, mod[l], ng, wr[l], br[l], tp, ls)
        x = _moe(hu, rt, rg, cnt, x1, mod[l], ng[3:4], w_gate_up, b_gate_up, w_down, b_down, l, tp, ls,
                 split=(l == DEPTH - 1))

    y_prompt = x[0].reshape(bp, lp, d)
    y_sample = x[1].reshape(bs, ls, d)
    return (y_prompt, y_sample, jnp.stack(new_k, axis=1), jnp.stack(new_v, axis=1),
            jnp.stack(new_dk, axis=1), jnp.stack(new_dv, axis=1))
```

```python
import functools
import math

import numpy as np
import jax
import jax.numpy as jnp
from jax import lax
from jax.experimental import pallas as pl
from jax.experimental.pallas import tpu as pltpu

F32 = jnp.float32
BF16 = jnp.bfloat16
HIGHEST = lax.Precision.HIGHEST

D_MODEL = 2048
DEPTH = 2
GRID_W = 64
GROUP_WIDTH = 512
EPS = 1e-6
IN_COLS = 5120
HYENA_BANDS = 16
HYENA_POS_DIM = 1 + 2 * HYENA_BANDS
HYENA_FFN = 64
HYENA_FAST_DECAY = 0.3
HYENA_SLOW_DECAY = 1.5
HYENA_TARGET = 0.01
POS_PAD = 128
CHUNK = 128
SGU_HEADS = 4
ATTN_HEADS = 4
ATTN_KV_HEADS = 2
ATTN_HEAD_DIM = 128
ROPE_THETA = 10000.0
DIFF_HEADS = 4
DIFF_QK_DIM = 64
DIFF_V_DIM = 128
N_EXPERTS = 32
TOP_K = 4
D_FF = 2048
SWIGLU_LIMIT = 7.0
SWIGLU_ALPHA = 1.702
ROUTER_PAD = 128

IN_BM, IN_BN = 1024, 1024
OUT_BM = 256
MOE_BM = 512
GM1_TN = 1024
GM1_SUB = 256
GM2_TN = 2048
DISPATCH_BT = 512
FIN_BM = 256
FIN_ROWS = 32
FIN_UNROLL = 2
COND_ROWS = 16
MOD_TN = 1024
HY_CC = 256
HY_ROWS = 1024
CTX_ROWS = 1024
Q_CHUNK = 512
MIB = 2 ** 20


def _cp(sem, vmem_mib):
    return pltpu.CompilerParams(dimension_semantics=sem, vmem_limit_bytes=vmem_mib * MIB)


def _mod_kernel(c_ref, w_ref, b_ref, o_ref):
    c = c_ref[...]
    s = c * jax.nn.sigmoid(c)
    o_ref[...] = jnp.dot(s.astype(BF16), w_ref[...].astype(BF16), preferred_element_type=F32) + b_ref[...]


def _modulation(cond, w_mod, b_mod):
    n = w_mod.shape[-1]
    return pl.pallas_call(
        _mod_kernel,
        grid=(DEPTH, n // MOD_TN),
        in_specs=[
            pl.BlockSpec((COND_ROWS, D_MODEL), lambda l, j: (0, 0)),
            pl.BlockSpec((None, D_MODEL, MOD_TN), lambda l, j: (l, 0, j)),
            pl.BlockSpec((None, 1, MOD_TN), lambda l, j: (l, 0, j)),
        ],
        out_specs=pl.BlockSpec((None, COND_ROWS, MOD_TN), lambda l, j: (l, 0, j)),
        out_shape=jax.ShapeDtypeStruct((DEPTH, COND_ROWS, n), F32),
        compiler_params=_cp(("arbitrary", "arbitrary"), 40),
        name="modulation",
    )(cond, w_mod, b_mod.reshape(DEPTH, 1, n))


def _mod_row(i, bm, tp, ls):
    nbp = tp // bm
    return jnp.where(i < nbp, 0, 1 + (i - nbp) // (ls // bm))


def _in_proj_kernel(x_ref, mod_ref, g_ref, w_ref, o_ref, h_ref):
    @pl.when(pl.program_id(1) == 0)
    def _():
        x = x_ref[...]
        r = lax.rsqrt(jnp.mean(x * x, axis=-1, keepdims=True) + EPS)
        h = (x * r) * g_ref[...] * (1.0 + mod_ref[0, 1:2, :]) + mod_ref[0, 0:1, :]
        h_ref[...] = h.astype(BF16)

    o_ref[...] = jnp.dot(h_ref[...], w_ref[...], preferred_element_type=F32)


def _in_proj(x, mod, g, w, tp, ls):
    t = x.shape[0]
    bm = min(IN_BM, ls, tp)
    return pl.pallas_call(
        _in_proj_kernel,
        grid=(t // bm, IN_COLS // IN_BN),
        in_specs=[
            pl.BlockSpec((bm, D_MODEL), lambda i, j: (i, 0)),
            pl.BlockSpec((1, 6, D_MODEL), lambda i, j: (_mod_row(i, bm, tp, ls), 0, 0)),
            pl.BlockSpec((1, D_MODEL), lambda i, j: (0, 0)),
            pl.BlockSpec((D_MODEL, IN_BN), lambda i, j: (0, j)),
        ],
        out_specs=pl.BlockSpec((bm, IN_BN), lambda i, j: (i, j)),
        out_shape=jax.ShapeDtypeStruct((t, IN_COLS), F32),
        scratch_shapes=[pltpu.VMEM((bm, D_MODEL), BF16)],
        compiler_params=_cp(("arbitrary", "arbitrary"), 56),
        name="in_proj",
    )(x, mod, g, w)


def _dft_tables(L):
    k = np.arange(L, dtype=np.int64)
    ang = ((k[:, None] * k[None, :]) % (2 * L)).astype(np.float64) * (math.pi / L)
    cm = np.cos(ang)
    sm = np.sin(ang)
    alt = np.where(k % 2 == 0, 1.0, -1.0)
    sf = -sm
    sf[0] = alt
    fwd = np.concatenate([cm, sf], axis=0)
    wk = np.where(k == 0, 1.0, 2.0) / (2 * L)
    si = -sm / L
    si[:, 0] = alt / (2 * L)
    inv = np.concatenate([cm * wk[None, :], si], axis=1)
    alt8 = np.zeros((8, L))
    alt8[0] = alt
    return (jnp.asarray(cm, F32), jnp.asarray(sm, F32), jnp.asarray(alt8, F32),
            jnp.asarray(fwd.astype(BF16)), jnp.asarray(inv.astype(BF16)))


def _filter_consts(L):
    t = np.linspace(0.0, 1.0, L, dtype=np.float32)[:, None]
    bands = np.linspace(1e-4, HYENA_BANDS - 1, HYENA_BANDS, dtype=np.float32)[None, :]
    z = np.concatenate([t, np.cos(2 * math.pi * bands * t), np.sin(2 * math.pi * bands * t)], axis=-1)
    z = np.pad(z, ((0, 0), (0, POS_PAD - HYENA_POS_DIM)))
    max_decay = math.log(HYENA_TARGET) / HYENA_FAST_DECAY
    min_decay = math.log(HYENA_TARGET) / HYENA_SLOW_DECAY
    deltas = np.abs(np.linspace(min_decay, max_decay, GROUP_WIDTH, dtype=np.float32))
    window = np.exp(-t * deltas[None, :])
    return jnp.asarray(z, F32), jnp.asarray(window, F32)


def _filter_kernel(z_ref, w1_ref, b1_ref, w2_ref, b2_ref, w3_ref, fr_ref, win_ref, cm_ref, sm_ref, alt_ref,
                   kre_ref, kim_ref):
    c = GROUP_WIDTH
    h = jnp.dot(z_ref[...], w1_ref[...], precision=HIGHEST, preferred_element_type=F32) + b1_ref[...]
    h = jnp.sin(fr_ref[0:1, :] * h)
    h = jnp.dot(h, w2_ref[...], precision=HIGHEST, preferred_element_type=F32) + b2_ref[...]
    h = jnp.sin(fr_ref[1:2, :] * h)
    hfb = jnp.dot(h, w3_ref[...], precision=HIGHEST, preferred_element_type=F32)
    win = win_ref[...]
    hf = hfb[:, :c] * win
    hb = hfb[:, c:] * win
    row = lax.broadcasted_iota(jnp.int32, hf.shape, 0)
    g = hf + jnp.where(row == 0, 0.0, hb)
    kre = jnp.dot(cm_ref[...], g, precision=HIGHEST, preferred_element_type=F32)
    kim = jnp.dot(sm_ref[...], hb - hf, precision=HIGHEST, preferred_element_type=F32)
    kn = jnp.dot(alt_ref[...], g, precision=HIGHEST, preferred_element_type=F32)[0:1, :]
    kre_ref[...] = kre
    kim_ref[...] = jnp.where(row == 0, kn, kim)


def _hyena_filter(L, z, window, cm, sm, alt8, w1, b1, w2, b2, w3, freq):
    w1p = jnp.pad(w1, ((0, POS_PAD - HYENA_POS_DIM), (0, 0)))
    return pl.pallas_call(
        _filter_kernel,
        out_shape=(jax.ShapeDtypeStruct((L, GROUP_WIDTH), F32), jax.ShapeDtypeStruct((L, GROUP_WIDTH), F32)),
        compiler_params=pltpu.CompilerParams(vmem_limit_bytes=56 * MIB),
        name="hyena_filter",
    )(z, w1p, b1.reshape(1, -1), w2, b2.reshape(1, -1), w3, freq, window, cm, sm, alt8)


def _short_conv(a, w, b, row, L):
    prev = jnp.where(row == 0, 0.0, pltpu.roll(a, 1, 0))
    nxt = jnp.where(row == L - 1, 0.0, pltpu.roll(a, L - 1, 0))
    return b + prev * w[0:1, :] + a * w[1:2, :] + nxt * w[2:3, :]


def _hyena_kernel(a0_ref, a1_ref, a2_ref, w0_ref, w1_ref, w2_ref, b0_ref, b1_ref, b2_ref, kre_ref, kim_ref,
                  skip_ref, fwd_ref, inv_ref, o_ref, *, L, G):
    cc = HY_CC

    def lanes(ref):
        return ref[...] if G == 1 else jnp.concatenate([ref[g * L:(g + 1) * L, :] for g in range(G)], axis=1)

    def tiled(v):
        return v if G == 1 else jnp.concatenate([v] * G, axis=1)

    row = lax.broadcasted_iota(jnp.int32, (L, G * cc), 0)
    x1 = _short_conv(lanes(a1_ref), tiled(w1_ref[...]), tiled(b1_ref[...]), row, L)
    hv = _short_conv(lanes(a2_ref), tiled(w2_ref[...]), tiled(b2_ref[...]), row, L)
    u = hv * x1
    spec = jnp.dot(fwd_ref[...], u.astype(BF16), preferred_element_type=F32)
    ure, uim = spec[:L], spec[L:]
    kre, kim = tiled(kre_ref[...]), tiled(kim_ref[...])
    first = row == 0
    yre = ure * kre - jnp.where(first, 0.0, uim * kim)
    yim = jnp.where(first, uim * kim, ure * kim + uim * kre)
    yspec = jnp.concatenate([yre, yim], axis=0).astype(BF16)
    y = jnp.dot(inv_ref[...], yspec, preferred_element_type=F32)
    x0 = _short_conv(lanes(a0_ref), tiled(w0_ref[...]), tiled(b0_ref[...]), row, L)
    out = (x0 * (y + u * tiled(skip_ref[...]))).astype(BF16)
    for g in range(G):
        o_ref[g * L:(g + 1) * L, :] = out[:, g * cc:(g + 1) * cc]


def _hyena(proj, row_off, B, L, conv_w, conv_b, kre, kim, skip, fwd, inv):
    ncc = GROUP_WIDTH // HY_CC
    G = max(1, min(B, HY_ROWS // L))
    assert row_off % G == 0 and B % G == 0

    def a_spec(part):
        return pl.BlockSpec((G * L, HY_CC), lambda b, cc: (row_off // G + b, part * ncc + cc))

    def w_spec(part, rows):
        return pl.BlockSpec((rows, HY_CC), lambda b, cc: (0, part * ncc + cc))

    chan = pl.BlockSpec((L, HY_CC), lambda b, cc: (0, cc))
    conv_b = conv_b.reshape(1, -1)
    return pl.pallas_call(
        functools.partial(_hyena_kernel, L=L, G=G),
        grid=(B // G, ncc),
        in_specs=[a_spec(0), a_spec(1), a_spec(2), w_spec(0, 3), w_spec(1, 3), w_spec(2, 3),
                  w_spec(0, 1), w_spec(1, 1), w_spec(2, 1), chan, chan,
                  pl.BlockSpec((1, HY_CC), lambda b, cc: (0, cc)),
                  pl.BlockSpec((2 * L, L), lambda b, cc: (0, 0)),
                  pl.BlockSpec((L, 2 * L), lambda b, cc: (0, 0))],
        out_specs=pl.BlockSpec((G * L, HY_CC), lambda b, cc: (b, cc)),
        out_shape=jax.ShapeDtypeStruct((B * L, GROUP_WIDTH), BF16),
        compiler_params=_cp(("arbitrary", "arbitrary"), 56),
        name="hyena_conv",
    )(proj, proj, proj, conv_w, conv_w, conv_w, conv_b, conv_b, conv_b, kre, kim, skip.reshape(1, -1), fwd, inv)


def _sgu_kernel(u_ref, v_ref, g_ref, b_ref, w_ref, bs_ref, o_ref, *, L):
    v = v_ref[...]
    mu = jnp.mean(v, axis=-1, keepdims=True)
    xc = v - mu
    vn = xc * lax.rsqrt(jnp.mean(xc * xc, axis=-1, keepdims=True) + EPS) * g_ref[...] + b_ref[...]
    vnb = vn.astype(BF16)
    hc = GROUP_WIDTH // SGU_HEADS
    for h in range(SGU_HEADS):
        wh = w_ref[h].astype(BF16)
        bias = bs_ref[h]
        cols = slice(h * hc, (h + 1) * hc)
        for n in range(L // CHUNK):
            rows = slice(n * CHUNK, (n + 1) * CHUNK)
            mixed = jnp.dot(wh, vnb[rows, cols], preferred_element_type=F32) + bias
            o_ref[rows, cols] = (u_ref[rows, cols] * mixed).astype(BF16)


def _sgu(proj, row_off, B, L, g, b, w_s, b_s):
    hc = GROUP_WIDTH // SGU_HEADS
    bias = jnp.broadcast_to(b_s[:, :, None], (SGU_HEADS, CHUNK, hc))
    return pl.pallas_call(
        functools.partial(_sgu_kernel, L=L),
        grid=(B,),
        in_specs=[pl.BlockSpec((L, GROUP_WIDTH), lambda i: (row_off + i, 3)),
                  pl.BlockSpec((L, GROUP_WIDTH), lambda i: (row_off + i, 4)),
                  pl.BlockSpec((1, GROUP_WIDTH), lambda i: (0, 0)),
                  pl.BlockSpec((1, GROUP_WIDTH), lambda i: (0, 0)),
                  pl.BlockSpec((SGU_HEADS, CHUNK, CHUNK), lambda i: (0, 0, 0)),
                  pl.BlockSpec((SGU_HEADS, CHUNK, hc), lambda i: (0, 0, 0))],
        out_specs=pl.BlockSpec((L, GROUP_WIDTH), lambda i: (i, 0)),
        out_shape=jax.ShapeDtypeStruct((B * L, GROUP_WIDTH), BF16),
        compiler_params=_cp(("arbitrary",), 40),
        name="sgu",
    )(proj, proj, g.reshape(1, -1), b.reshape(1, -1), w_s, bias)


def _rope_tables(L, dh, width):
    quarter = dh // 4
    pos = np.arange(L)
    rowp = (pos // GRID_W).astype(np.float32)
    colp = (pos % GRID_W).astype(np.float32)
    inv_freq = (ROPE_THETA ** (-np.arange(quarter, dtype=np.float32) / quarter)).astype(np.float32)
    ar = (rowp[:, None] * inv_freq[None, :]).astype(np.float64)
    ac = (colp[:, None] * inv_freq[None, :]).astype(np.float64)
    cos = np.concatenate([np.cos(ar), np.cos(ar), np.cos(ac), np.cos(ac)], axis=-1)
    sin = np.concatenate([-np.sin(ar), np.sin(ar), -np.sin(ac), np.sin(ac)], axis=-1)
    reps = width // dh
    return jnp.asarray(np.tile(cos, (1, reps)), F32), jnp.asarray(np.tile(sin, (1, reps)), F32)


def _rope(x, cos, sin, quarter):
    w = x.shape[-1]
    lane = lax.broadcasted_iota(jnp.int32, x.shape, 1)
    partner = jnp.where((lane & quarter) != 0, pltpu.roll(x, quarter, 1), pltpu.roll(x, w - quarter, 1))
    return x * cos + partner * sin


def _head_rms(x, g, heads, dh):
    outs = []
    for h in range(heads):
        xh = x[:, h * dh:(h + 1) * dh]
        outs.append(xh * lax.rsqrt(jnp.mean(xh * xh, axis=-1, keepdims=True) + EPS) * g)
    return jnp.concatenate(outs, axis=-1)


def _softmax_pv(q, k, v):
    s = lax.dot_general(q, k, (((1,), (1,)), ((), ())), preferred_element_type=F32)
    e = jnp.exp(s - jnp.max(s, axis=-1, keepdims=True))
    inv_l = 1.0 / jnp.sum(e, axis=-1, keepdims=True)
    return jnp.dot(e.astype(BF16), v, preferred_element_type=F32), inv_l


def _gqa_body(q, k_ref, v_ref, o_ref, L):
    dh = ATTN_HEAD_DIM
    group = ATTN_HEADS // ATTN_KV_HEADS
    qs = (q * dh ** -0.5).astype(BF16)
    qc = min(Q_CHUNK, L)
    for h in range(ATTN_HEADS):
        kv = h // group
        kh = k_ref[:, kv * dh:(kv + 1) * dh]
        vh = v_ref[:, kv * dh:(kv + 1) * dh]
        for c in range(L // qc):
            rows = slice(c * qc, (c + 1) * qc)
            pv, inv_l = _softmax_pv(qs[rows, h * dh:(h + 1) * dh], kh, vh)
            o_ref[rows, h * dh:(h + 1) * dh] = (pv * inv_l).astype(BF16)


def _gqa_ctx_kernel(q_ref, k_ref, v_ref, gq_ref, gk_ref, o_ref, ko_ref, vo_ref, kb_ref, vb_ref, *, L, G):
    for g in range(G):
        rows = slice(g * L, (g + 1) * L)
        q = _head_rms(q_ref[rows, :], gq_ref[...], ATTN_HEADS, ATTN_HEAD_DIM)
        k = _head_rms(k_ref[rows, :], gk_ref[...], ATTN_KV_HEADS, ATTN_HEAD_DIM)
        v = v_ref[rows, :]
        ko_ref[rows, :] = k
        vo_ref[rows, :] = v
        kb_ref[rows, :] = k.astype(BF16)
        vb_ref[rows, :] = v.astype(BF16)
        seq = pl.ds(g * L, L)
        _gqa_body(q, kb_ref.at[seq], vb_ref.at[seq], o_ref.at[seq], L)


def _gqa_lat_kernel(q_ref, k_ref, v_ref, gq_ref, gk_ref, cq_ref, sq_ref, ck_ref, sk_ref, pk_ref, pv_ref, o_ref,
                    kb_ref, vb_ref, *, L, P):
    quarter = ATTN_HEAD_DIM // 4
    q = _rope(_head_rms(q_ref[...], gq_ref[...], ATTN_HEADS, ATTN_HEAD_DIM), cq_ref[...], sq_ref[...], quarter)
    k = _rope(_head_rms(k_ref[...], gk_ref[...], ATTN_KV_HEADS, ATTN_HEAD_DIM), ck_ref[...], sk_ref[...], quarter)
    kb_ref[0:P, :] = pk_ref[...].astype(BF16)
    kb_ref[P:P + L, :] = k.astype(BF16)
    vb_ref[0:P, :] = pv_ref[...].astype(BF16)
    vb_ref[P:P + L, :] = v_ref[...].astype(BF16)
    _gqa_body(q, kb_ref, vb_ref, o_ref, L)


def _gqa(proj, row_off, B, L, gq, gk, l, past=None, tables=None):
    qw = ATTN_HEADS * ATTN_HEAD_DIM
    kw = ATTN_KV_HEADS * ATTN_HEAD_DIM
    qoff = 5
    koff = (2560 + qw) // kw
    in_specs = [pl.BlockSpec((L, qw), lambda i: (row_off + i, qoff)),
                pl.BlockSpec((L, kw), lambda i: (row_off + i, koff)),
                pl.BlockSpec((L, kw), lambda i: (row_off + i, koff + 1)),
                pl.BlockSpec((1, ATTN_HEAD_DIM), lambda i: (0, 0)),
                pl.BlockSpec((1, ATTN_HEAD_DIM), lambda i: (0, 0))]
    args = [proj, proj, proj, gq.reshape(1, -1), gk.reshape(1, -1)]
    y_spec = pl.BlockSpec((L, qw), lambda i: (i, 0))
    y_shape = jax.ShapeDtypeStruct((B * L, qw), BF16)
    if past is None:
        G = max(1, min(B, CTX_ROWS // L))
        assert row_off % G == 0 and B % G == 0
        blk = lambda w, off: pl.BlockSpec((G * L, w), lambda i: (row_off // G + i, off))
        out = lambda w: pl.BlockSpec((G * L, w), lambda i: (i, 0))
        return pl.pallas_call(
            functools.partial(_gqa_ctx_kernel, L=L, G=G),
            grid=(B // G,),
            in_specs=[blk(qw, qoff), blk(kw, koff), blk(kw, koff + 1)] + in_specs[3:],
            out_specs=(out(qw), out(kw), out(kw)),
            out_shape=(y_shape, jax.ShapeDtypeStruct((B * L, kw), F32), jax.ShapeDtypeStruct((B * L, kw), F32)),
            scratch_shapes=[pltpu.VMEM((G * L, kw), BF16), pltpu.VMEM((G * L, kw), BF16)],
            compiler_params=_cp(("arbitrary",), 40),
            name="gqa_ctx",
        )(*args)
    pk, pv = past
    P = pk.shape[2]
    cq, sq, ck, sk = tables
    const = lambda w: pl.BlockSpec((L, w), lambda i: (0, 0))
    cache = pl.BlockSpec((None, None, P, kw), lambda i: (i, l, 0, 0))
    return pl.pallas_call(
        functools.partial(_gqa_lat_kernel, L=L, P=P),
        grid=(B,),
        in_specs=in_specs + [const(qw), const(qw), const(kw), const(kw), cache, cache],
        out_specs=y_spec,
        out_shape=y_shape,
        scratch_shapes=[pltpu.VMEM((P + L, kw), BF16), pltpu.VMEM((P + L, kw), BF16)],
        compiler_params=_cp(("arbitrary",), 56),
        name="gqa_lat",
    )(*args, cq, sq, ck, sk, pk, pv)


def _diff_body(dq, k_ref, v_ref, dl_ref, g_ref, o_ref, L, lambda_init):
    dv = DIFF_V_DIM
    dl = dl_ref[...]
    lam = (jnp.exp(jnp.sum(dl[0:1] * dl[1:2], axis=-1, keepdims=True))
           - jnp.exp(jnp.sum(dl[2:3] * dl[3:4], axis=-1, keepdims=True)) + lambda_init)
    qs = dq * DIFF_QK_DIM ** -0.5
    first_map = lax.broadcasted_iota(jnp.int32, (1, dv), 1) < DIFF_QK_DIM
    qc = min(Q_CHUNK, L)
    for h in range(DIFF_HEADS):
        cols = slice(h * dv, (h + 1) * dv)
        kh = k_ref[:, cols]
        vh = v_ref[:, cols]
        for c in range(L // qc):
            rows = slice(c * qc, (c + 1) * qc)
            qh = qs[rows, cols]
            pv0, inv0 = _softmax_pv(jnp.where(first_map, qh, 0.0).astype(BF16), kh, vh)
            pv1, inv1 = _softmax_pv(jnp.where(first_map, 0.0, qh).astype(BF16), kh, vh)
            o = pv0 * inv0 - pv1 * (lam * inv1)
            r = lax.rsqrt(jnp.mean(o * o, axis=-1, keepdims=True) + EPS)
            o_ref[rows, cols] = ((o * r) * g_ref[...] * (1.0 - lambda_init)).astype(BF16)


def _diff_ctx_kernel(q_ref, k_ref, v_ref, dl_ref, g_ref, o_ref, ko_ref, vo_ref, kb_ref, vb_ref, *, L, G, lambda_init):
    for g in range(G):
        rows = slice(g * L, (g + 1) * L)
        k = k_ref[rows, :]
        v = v_ref[rows, :]
        ko_ref[rows, :] = k
        vo_ref[rows, :] = v
        kb_ref[rows, :] = k.astype(BF16)
        vb_ref[rows, :] = v.astype(BF16)
        seq = pl.ds(g * L, L)
        _diff_body(q_ref[rows, :], kb_ref.at[seq], vb_ref.at[seq], dl_ref, g_ref, o_ref.at[seq], L, lambda_init)


def _diff_lat_kernel(q_ref, k_ref, v_ref, dl_ref, g_ref, cos_ref, sin_ref, pk_ref, pv_ref, o_ref, kb_ref, vb_ref,
                     *, L, P, lambda_init):
    quarter = DIFF_QK_DIM // 4
    cos, sin = cos_ref[...], sin_ref[...]
    q = _rope(q_ref[...], cos, sin, quarter)
    k = _rope(k_ref[...], cos, sin, quarter)
    kb_ref[0:P, :] = pk_ref[...].astype(BF16)
    kb_ref[P:P + L, :] = k.astype(BF16)
    vb_ref[0:P, :] = pv_ref[...].astype(BF16)
    vb_ref[P:P + L, :] = v_ref[...].astype(BF16)
    _diff_body(q, kb_ref, vb_ref, dl_ref, g_ref, o_ref, L, lambda_init)


def _diff(proj, row_off, B, L, dl, g, l, past=None, tables=None):
    w = GROUP_WIDTH
    lambda_init = 0.8 - 0.6 * math.exp(-0.3 * l)
    in_specs = [pl.BlockSpec((L, w), lambda i: (row_off + i, 7)),
                pl.BlockSpec((L, w), lambda i: (row_off + i, 8)),
                pl.BlockSpec((L, w), lambda i: (row_off + i, 9)),
                pl.BlockSpec((4, DIFF_QK_DIM), lambda i: (0, 0)),
                pl.BlockSpec((1, DIFF_V_DIM), lambda i: (0, 0))]
    args = [proj, proj, proj, dl, g.reshape(1, -1)]
    y_spec = pl.BlockSpec((L, w), lambda i: (i, 0))
    y_shape = jax.ShapeDtypeStruct((B * L, w), BF16)
    if past is None:
        G = max(1, min(B, CTX_ROWS // L))
        assert row_off % G == 0 and B % G == 0
        blk = lambda off: pl.BlockSpec((G * L, w), lambda i: (row_off // G + i, off))
        out = pl.BlockSpec((G * L, w), lambda i: (i, 0))
        return pl.pallas_call(
            functools.partial(_diff_ctx_kernel, L=L, G=G, lambda_init=lambda_init),
            grid=(B // G,),
            in_specs=[blk(7), blk(8), blk(9)] + in_specs[3:],
            out_specs=(out, out, out),
            out_shape=(y_shape, jax.ShapeDtypeStruct((B * L, w), F32), jax.ShapeDtypeStruct((B * L, w), F32)),
            scratch_shapes=[pltpu.VMEM((G * L, w), BF16), pltpu.VMEM((G * L, w), BF16)],
            compiler_params=_cp(("arbitrary",), 48),
            name="diff_ctx",
        )(*args)
    pk, pv = past
    P = pk.shape[2]
    cos, sin = tables
    const = pl.BlockSpec((L, w), lambda i: (0, 0))
    cache = pl.BlockSpec((None, None, P, w), lambda i: (i, l, 0, 0))
    return pl.pallas_call(
        functools.partial(_diff_lat_kernel, L=L, P=P, lambda_init=lambda_init),
        grid=(B,),
        in_specs=in_specs + [const, const, cache, cache],
        out_specs=y_spec,
        out_shape=y_shape,
        scratch_shapes=[pltpu.VMEM((P + L, w), BF16), pltpu.VMEM((P + L, w), BF16)],
        compiler_params=_cp(("arbitrary",), 56),
        name="diff_lat",
    )(*args, cos, sin, pk, pv)


def _pack_halves(lo, hi):
    lo_bits = pltpu.bitcast(lo.astype(BF16).astype(F32), jnp.uint32) >> 16
    hi_bits = pltpu.bitcast(hi.astype(BF16).astype(F32), jnp.uint32) & jnp.uint32(0xFFFF0000)
    return lo_bits | hi_bits


def _unpack_halves(words):
    lo = pltpu.bitcast(words << 16, F32)
    hi = pltpu.bitcast(words & jnp.uint32(0xFFFF0000), F32)
    return lo, hi


def _route_rows(logits, prefix_base, tri_ref):
    lane = lax.broadcasted_iota(jnp.int32, logits.shape, 1)
    neg = jnp.float32(-jnp.inf)
    work = jnp.where(lane < N_EXPERTS, logits, neg)
    sels, vals, idxs = [], [], []
    for _ in range(TOP_K):
        mk = jnp.max(work, axis=-1, keepdims=True)
        ik = jnp.min(jnp.where(work == mk, lane, ROUTER_PAD), axis=-1, keepdims=True)
        sel = lane == ik
        work = jnp.where(sel, neg, work)
        sels.append(sel)
        vals.append(mk)
        idxs.append(ik)
    es = [jnp.exp(v - vals[0]) for v in vals]
    inv = 1.0 / (es[0] + es[1] + es[2] + es[3])
    member = jnp.zeros(logits.shape, F32)
    for sel in sels:
        member = member + jnp.where(sel, 1.0, 0.0)
    prefix = jnp.dot(tri_ref[...], member.astype(BF16), preferred_element_type=F32) + prefix_base
    ri = jnp.zeros(logits.shape, jnp.int32)
    rg = jnp.zeros(logits.shape, F32)
    for k in range(TOP_K):
        rank = jnp.sum(jnp.where(sels[k], prefix, 0.0), axis=-1, keepdims=True).astype(jnp.int32)
        ri = jnp.where(lane == k, idxs[k], ri)
        ri = jnp.where(lane == TOP_K + k, rank, ri)
        rg = jnp.where(lane == k, es[k] * inv, rg)
    return ri, rg, jnp.sum(member, axis=0, keepdims=True)


def _out_proj_kernel(pa_ref, pb_ref, pc_ref, pd_ref, sa_ref, sb_ref, sc_ref, sd_ref, w_ref, x_ref, mod_ref, ng_ref,
                     wr_ref, br_ref, tri_ref, x1_ref, hu_ref, rt_ref, rg_ref, cnt_ref, cat_ref, carry_ref, *, nbp):
    i = pl.program_id(0)

    @pl.when(i == 0)
    def _():
        carry_ref[...] = jnp.zeros_like(carry_ref)

    gw = GROUP_WIDTH

    @pl.when(i < nbp)
    def _():
        for m_, ref in enumerate((pa_ref, pb_ref, pc_ref, pd_ref)):
            cat_ref[:, m_ * gw:(m_ + 1) * gw] = ref[...]

    @pl.when(i >= nbp)
    def _():
        for m_, ref in enumerate((sa_ref, sb_ref, sc_ref, sd_ref)):
            cat_ref[:, m_ * gw:(m_ + 1) * gw] = ref[...]

    m = jnp.dot(cat_ref[...], w_ref[...], preferred_element_type=F32)
    mn = m * lax.rsqrt(jnp.mean(m * m, axis=-1, keepdims=True) + EPS) * ng_ref[1:2, :]
    x1 = x_ref[...] + mod_ref[0, 2:3, :] * mn
    x1_ref[...] = x1
    hn = x1 * lax.rsqrt(jnp.mean(x1 * x1, axis=-1, keepdims=True) + EPS) * ng_ref[2:3, :]
    h = hn * (1.0 + mod_ref[0, 4:5, :]) + mod_ref[0, 3:4, :]
    half = D_MODEL // 2
    hu_ref[...] = _pack_halves(h[:, :half], h[:, half:])
    hh = h.astype(BF16)
    hl = (h - hh.astype(F32)).astype(BF16)
    both = jnp.dot(hh, wr_ref[...], preferred_element_type=F32)
    low = jnp.dot(hl, wr_ref[:, :ROUTER_PAD], preferred_element_type=F32)
    logits = both[:, :ROUTER_PAD] + both[:, ROUTER_PAD:] + low + br_ref[...]
    ri, rg, counts = _route_rows(logits, carry_ref[0:1, :], tri_ref)
    rt_ref[...] = ri.astype(F32).T[:8, :].astype(jnp.int32)
    rg_ref[...] = rg
    carry_ref[...] = carry_ref[...] + counts
    cnt_ref[...] = carry_ref[...]


def _out_proj(ys_prompt, ys_latent, w, x, mod, ng, wr, br, tp, ls):
    t = x.shape[0]
    bm = OUT_BM
    nbp = tp // bm
    pspec = pl.BlockSpec((bm, GROUP_WIDTH), lambda i: (jnp.minimum(i, nbp - 1), 0))
    sspec = pl.BlockSpec((bm, GROUP_WIDTH), lambda i: (jnp.maximum(i - nbp, 0), 0))
    row = pl.BlockSpec((bm, D_MODEL), lambda i: (i, 0))
    info = pl.BlockSpec((bm, ROUTER_PAD), lambda i: (i, 0))
    tri = jnp.asarray(np.tril(np.ones((bm, bm), np.float32), -1).astype(BF16))
    return pl.pallas_call(
        functools.partial(_out_proj_kernel, nbp=nbp),
        grid=(t // bm,),
        in_specs=[pspec, pspec, pspec, pspec, sspec, sspec, sspec, sspec,
                  pl.BlockSpec((D_MODEL, D_MODEL), lambda i: (0, 0)),
                  row,
                  pl.BlockSpec((1, 6, D_MODEL), lambda i: (_mod_row(i, bm, tp, ls), 0, 0)),
                  pl.BlockSpec((4, D_MODEL), lambda i: (0, 0)),
                  pl.BlockSpec((D_MODEL, 2 * ROUTER_PAD), lambda i: (0, 0)),
                  pl.BlockSpec((1, ROUTER_PAD), lambda i: (0, 0)),
                  pl.BlockSpec((bm, bm), lambda i: (0, 0))],
        out_specs=(row, pl.BlockSpec((bm, D_MODEL // 2), lambda i: (i, 0)), pl.BlockSpec((8, bm), lambda i: (0, i)),
                   info, pl.BlockSpec((8, ROUTER_PAD), lambda i: (0, 0))),
        out_shape=(jax.ShapeDtypeStruct((t, D_MODEL), F32), jax.ShapeDtypeStruct((t, D_MODEL // 2), jnp.uint32),
                   jax.ShapeDtypeStruct((8, t), jnp.int32), jax.ShapeDtypeStruct((t, ROUTER_PAD), F32),
                   jax.ShapeDtypeStruct((8, ROUTER_PAD), F32)),
        scratch_shapes=[pltpu.VMEM((bm, D_MODEL), BF16), pltpu.VMEM((8, ROUTER_PAD), F32)],
        compiler_params=_cp(("arbitrary",), 56),
        name="out_proj",
    )(*ys_prompt, *ys_latent, w, x, mod, ng, wr, br, tri)


def _new_expert(be_ref, i):
    return jnp.logical_or(i == 0, be_ref[jnp.maximum(i - 1, 0)] != be_ref[i])


def _next_tile(nx_ref, be_ref, i, j, nj):
    nxt = nx_ref[i]
    has_run = nxt >= 0
    return jnp.logical_or(has_run, j + 1 < nj), jnp.where(has_run, j, j + 1), jnp.where(has_run, nxt, be_ref[0])


def _gm1_kernel(be_ref, nu_ref, hf_ref, nx_ref, x_ref, w_ref, bg_ref, bu_ref, o_ref, wst_ref, wbf_ref, xb_ref, sem,
                *, l, nj):
    j = pl.program_id(0)
    i = pl.program_id(1)
    used = i < nu_ref[0]
    tn = GM1_TN
    half = D_MODEL // 2

    def tile_copies(jt, e):
        gcol = pl.multiple_of(jt * tn, tn)
        ucol = pl.multiple_of(D_FF + jt * tn, tn)
        return (pltpu.make_async_copy(w_ref.at[l, e, :, pl.ds(gcol, tn)], wst_ref.at[:, pl.ds(0, tn)], sem.at[0]),
                pltpu.make_async_copy(w_ref.at[l, e, :, pl.ds(ucol, tn)], wst_ref.at[:, pl.ds(tn, tn)], sem.at[1]))

    @pl.when(jnp.logical_and(used, _new_expert(be_ref, i)))
    def _():
        e = be_ref[i]

        @pl.when(jnp.logical_and(j == 0, i == 0))
        def _():
            for cp in tile_copies(0, e):
                cp.start()

        for cp in tile_copies(j, e):
            cp.wait()
        wbf_ref[...] = wst_ref[...].astype(BF16)
        exists, jn, en = _next_tile(nx_ref, be_ref, i, j, nj)

        @pl.when(exists)
        def _():
            for cp in tile_copies(jn, en):
                cp.start()

    def compute(rows):
        lo, hi = _unpack_halves(x_ref[:rows, :])
        xb_ref[:rows, :half] = lo.astype(BF16)
        xb_ref[:rows, half:] = hi.astype(BF16)
        for c in range(tn // GM1_SUB):
            cols = slice(c * GM1_SUB, (c + 1) * GM1_SUB)
            ucols = slice(tn + c * GM1_SUB, tn + (c + 1) * GM1_SUB)
            g = jnp.dot(xb_ref[:rows, :], wbf_ref[:, cols], preferred_element_type=F32)
            u = jnp.dot(xb_ref[:rows, :], wbf_ref[:, ucols], preferred_element_type=F32)
            gate = jnp.minimum(g + bg_ref[:, cols], SWIGLU_LIMIT)
            up = jnp.clip(u + bu_ref[:, cols], -SWIGLU_LIMIT, SWIGLU_LIMIT)
            glu = gate * jax.nn.sigmoid(gate * SWIGLU_ALPHA)
            o_ref[:rows, cols] = ((up + 1.0) * glu).astype(BF16)
        if rows < MOE_BM:
            o_ref[rows:, :] = jnp.zeros((MOE_BM - rows, tn), BF16)

    small = hf_ref[i] != 0

    @pl.when(jnp.logical_and(used, jnp.logical_not(small)))
    def _():
        compute(MOE_BM)

    @pl.when(jnp.logical_and(used, small))
    def _():
        compute(MOE_BM // 2)

    @pl.when(jnp.logical_not(used))
    def _():
        o_ref[...] = jnp.zeros_like(o_ref)


def _gm1(layout, xs, w_gu, b_gu, l):
    rows = xs.shape[0]
    nb = rows // MOE_BM
    nj = D_FF // GM1_TN
    b_gu = b_gu.reshape(DEPTH, N_EXPERTS, 1, 2 * D_FF)
    last = lambda i, nu: jnp.minimum(i, nu[0] - 1)
    grid_spec = pltpu.PrefetchScalarGridSpec(
        num_scalar_prefetch=4,
        grid=(nj, nb),
        in_specs=[
            pl.BlockSpec((MOE_BM, D_MODEL // 2), lambda j, i, be, nu, hf, nx: (last(i, nu), 0)),
            pl.BlockSpec(memory_space=pl.ANY),
            pl.BlockSpec((None, None, 1, GM1_TN), lambda j, i, be, nu, hf, nx: (l, be[i], 0, j)),
            pl.BlockSpec((None, None, 1, GM1_TN), lambda j, i, be, nu, hf, nx: (l, be[i], 0, nj + j)),
        ],
        out_specs=pl.BlockSpec((MOE_BM, GM1_TN), lambda j, i, be, nu, hf, nx: (i, j)),
        scratch_shapes=[pltpu.VMEM((D_MODEL, 2 * GM1_TN), F32), pltpu.VMEM((D_MODEL, 2 * GM1_TN), BF16),
                        pltpu.VMEM((MOE_BM, D_MODEL), BF16), pltpu.SemaphoreType.DMA((2,))],
    )
    return pl.pallas_call(
        functools.partial(_gm1_kernel, l=l, nj=nj),
        grid_spec=grid_spec,
        out_shape=jax.ShapeDtypeStruct((rows, D_FF), BF16),
        compiler_params=_cp(("arbitrary", "arbitrary"), 56),
        name="moe_gate_up",
    )(*layout, xs, w_gu, b_gu, b_gu)


def _gm2_kernel(be_ref, nu_ref, hf_ref, nx_ref, a_ref, w_ref, b_ref, o_ref, wst_ref, wbf_ref, sem, *, l, nj):
    j = pl.program_id(0)
    i = pl.program_id(1)
    used = i < nu_ref[0]
    tn = GM2_TN
    half = tn // 2

    def tile_copy(jt, e):
        col = pl.multiple_of(jt * tn, tn)
        return pltpu.make_async_copy(w_ref.at[l, e, :, pl.ds(col, tn)], wst_ref, sem)

    @pl.when(jnp.logical_and(used, _new_expert(be_ref, i)))
    def _():
        e = be_ref[i]

        @pl.when(jnp.logical_and(j == 0, i == 0))
        def _():
            tile_copy(0, e).start()

        tile_copy(j, e).wait()
        wbf_ref[...] = wst_ref[...].astype(BF16)
        exists, jn, en = _next_tile(nx_ref, be_ref, i, j, nj)

        @pl.when(exists)
        def _():
            tile_copy(jn, en).start()

    def compute(rows):
        lo = jnp.dot(a_ref[:rows, :], wbf_ref[:, :half], preferred_element_type=F32) + b_ref[:, :half]
        hi = jnp.dot(a_ref[:rows, :], wbf_ref[:, half:], preferred_element_type=F32) + b_ref[:, half:]
        o_ref[:rows, :] = _pack_halves(lo, hi)
        if rows < MOE_BM:
            o_ref[rows:, :] = jnp.zeros((MOE_BM - rows, half), jnp.uint32)

    small = hf_ref[i] != 0

    @pl.when(jnp.logical_and(used, jnp.logical_not(small)))
    def _():
        compute(MOE_BM)

    @pl.when(jnp.logical_and(used, small))
    def _():
        compute(MOE_BM // 2)

    @pl.when(jnp.logical_not(used))
    def _():
        o_ref[...] = jnp.zeros_like(o_ref)


def _gm2(layout, act, w_dn, b_dn, l):
    rows = act.shape[0]
    nb = rows // MOE_BM
    nj = D_MODEL // GM2_TN
    b_dn = b_dn.reshape(DEPTH, N_EXPERTS, 1, D_MODEL)
    last = lambda i, nu: jnp.minimum(i, nu[0] - 1)
    grid_spec = pltpu.PrefetchScalarGridSpec(
        num_scalar_prefetch=4,
        grid=(nj, nb),
        in_specs=[
            pl.BlockSpec((MOE_BM, D_FF), lambda j, i, be, nu, hf, nx: (last(i, nu), 0)),
            pl.BlockSpec(memory_space=pl.ANY),
            pl.BlockSpec((None, None, 1, GM2_TN), lambda j, i, be, nu, hf, nx: (l, be[i], 0, j)),
        ],
        out_specs=pl.BlockSpec((MOE_BM, GM2_TN // 2), lambda j, i, be, nu, hf, nx: (i, j)),
        scratch_shapes=[pltpu.VMEM((D_FF, GM2_TN), F32), pltpu.VMEM((D_FF, GM2_TN), BF16),
                        pltpu.SemaphoreType.DMA(())],
    )
    return pl.pallas_call(
        functools.partial(_gm2_kernel, l=l, nj=nj),
        grid_spec=grid_spec,
        out_shape=jax.ShapeDtypeStruct((rows, D_MODEL // 2), jnp.uint32),
        compiler_params=_cp(("arbitrary", "arbitrary"), 56),
        name="moe_down",
    )(*layout, act, w_dn, b_dn)


def _block_layout(counts, n_blocks):
    padded = (counts + MOE_BM - 1) // MOE_BM * MOE_BM
    pad_end = jnp.cumsum(padded)
    pad_start = (pad_end - padded).astype(jnp.int32)
    starts = jnp.arange(n_blocks, dtype=jnp.int32) * MOE_BM
    block_e = jnp.minimum(jnp.sum(pad_end[None, :] <= starts[:, None], axis=1), N_EXPERTS - 1).astype(jnp.int32)
    n_used = (pad_end[-1] // MOE_BM).astype(jnp.int32).reshape(1)
    valid = jnp.clip(counts[block_e] - (starts - pad_start[block_e]), 0, MOE_BM)
    block_half = (valid <= MOE_BM // 2).astype(jnp.int32)
    idx = jnp.arange(N_EXPERTS, dtype=jnp.int32)
    later = jnp.where((counts > 0)[None, :] & (idx[None, :] > idx[:, None]), idx[None, :], N_EXPERTS)
    next_e = jnp.min(later, axis=1)
    block_next = jnp.where(next_e == N_EXPERTS, -1, next_e)[block_e].astype(jnp.int32)
    return pad_start, (block_e, n_used, block_half, block_next)


def _dispatch_kernel(dest_ref, ps_ref, cn_ref, h_ref, xs_ref, zero_ref, sem, zsem, *, bt):
    i = pl.program_id(0)
    base = i * bt
    nt = pl.num_programs(0) * bt

    def zero_rows(issue):
        def per_expert(e, carry):
            start = ps_ref[e] + cn_ref[e]
            end = (start + MOE_BM - 1) // MOE_BM * MOE_BM
            head_end = jnp.minimum((start + 7) // 8 * 8, end)

            def single(r, c):
                cp = pltpu.make_async_copy(zero_ref.at[pl.ds(0, 1)], xs_ref.at[pl.ds(r, 1)], zsem)
                cp.start() if issue else cp.wait()
                return c

            def chunk(q, c):
                r = pl.multiple_of(head_end + q * 8, 8)
                cp = pltpu.make_async_copy(zero_ref.at[pl.ds(0, 8)], xs_ref.at[pl.ds(r, 8)], zsem)
                cp.start() if issue else cp.wait()
                return c

            lax.fori_loop(start, head_end, single, 0)
            lax.fori_loop(0, (end - head_end) // 8, chunk, 0)
            return carry

        lax.fori_loop(0, N_EXPERTS, per_expert, 0)

        last = N_EXPERTS - 1
        used_rows = ps_ref[last] + (cn_ref[last] + MOE_BM - 1) // MOE_BM * MOE_BM

        def block(b, c):
            r = pl.multiple_of(b * MOE_BM, MOE_BM)
            cp = pltpu.make_async_copy(zero_ref, xs_ref.at[pl.ds(r, MOE_BM)], zsem)
            cp.start() if issue else cp.wait()
            return c

        lax.fori_loop(used_rows // MOE_BM, xs_ref.shape[0] // MOE_BM, block, 0)

    @pl.when(i == 0)
    def _():
        zero_ref[...] = jnp.zeros_like(zero_ref)
        zero_rows(True)

    def group(g, carry):
        r0 = pl.multiple_of(g * 8, 8)
        rows8 = h_ref.at[pl.ds(r0, 8)]
        for r in range(8):
            for k in range(TOP_K):
                dst = dest_ref[k * nt + base + r0 + r]
                pltpu.make_async_copy(rows8.at[pl.ds(r, 1)], xs_ref.at[pl.ds(dst, 1)], sem).start(priority=k % 2)
        return carry

    lax.fori_loop(0, bt // 8, group, 0)
    for _ in range(TOP_K):
        pltpu.make_async_copy(h_ref, xs_ref.at[pl.ds(0, bt)], sem).wait()

    @pl.when(i == pl.num_programs(0) - 1)
    def _():
        zero_rows(False)


def _dispatch(dest, pad_start, counts, hu, rows):
    t, w = hu.shape
    bt = DISPATCH_BT
    grid_spec = pltpu.PrefetchScalarGridSpec(
        num_scalar_prefetch=3,
        grid=(t // bt,),
        in_specs=[pl.BlockSpec((bt, w), lambda i, dst, ps, cn: (i, 0))],
        out_specs=pl.BlockSpec(memory_space=pl.ANY),
        scratch_shapes=[pltpu.VMEM((MOE_BM, w), jnp.uint32), pltpu.SemaphoreType.DMA(()),
                        pltpu.SemaphoreType.DMA(())],
    )
    return pl.pallas_call(
        functools.partial(_dispatch_kernel, bt=bt),
        grid_spec=grid_spec,
        out_shape=jax.ShapeDtypeStruct((rows, w), jnp.uint32),
        compiler_params=pltpu.CompilerParams(dimension_semantics=("arbitrary",), has_side_effects=True),
        name="moe_dispatch",
    )(dest, pad_start, counts, hu)


def _finish_kernel(dest_ref, y_ref, rg_ref, x_ref, mod_ref, g_ref, *rest, bt, nbp):
    if nbp is None:
        o_ref, buf, sem = rest
    else:
        op_ref, os_ref, buf, sem = rest
    i = pl.program_id(0)
    n = pl.num_programs(0)
    slot = i % 2
    q = GM2_TN // 2
    nj = D_MODEL // GM2_TN

    def issue8(blk, to_slot, r0):
        for k in range(TOP_K):
            rows8 = buf.at[to_slot, k, pl.ds(r0, 8)]
            for r in range(8):
                src = dest_ref[k * (n * bt) + blk * bt + r0 + r]
                pltpu.make_async_copy(y_ref.at[pl.ds(src, 1)], rows8.at[pl.ds(r, 1)], sem.at[to_slot]).start(
                    priority=k % 2)

    def issue_block(blk, to_slot):
        lax.fori_loop(0, bt // 8, lambda g, c: (issue8(blk, to_slot, pl.multiple_of(g * 8, 8)), c)[1], 0)

    def combine_rows(out_ref, r0):
        rows = pl.ds(r0, FIN_ROWS)
        rg = rg_ref[rows, :]
        parts = [None] * (2 * nj)
        for k in range(TOP_K):
            gate = rg[:, k:k + 1]
            for j in range(nj):
                lo, hi = _unpack_halves(buf[slot, k, rows, j * q:(j + 1) * q])
                for idx, val in ((2 * j, lo), (2 * j + 1, hi)):
                    parts[idx] = gate * val if k == 0 else parts[idx] + gate * val
        f = jnp.concatenate(parts, axis=-1)
        fn = f * lax.rsqrt(jnp.mean(f * f, axis=-1, keepdims=True) + EPS) * g_ref[...]
        out_ref[rows, :] = x_ref[rows, :] + mod_ref[0, 5:6, :] * fn

    def combine_all(out_ref):
        def chunk(c, carry):
            combine_rows(out_ref, pl.multiple_of(c * FIN_ROWS, FIN_ROWS))
            return carry

        lax.fori_loop(0, bt // FIN_ROWS, chunk, 0, unroll=FIN_UNROLL)

    @pl.when(i == 0)
    def _():
        issue_block(0, 0)

    @pl.when(i + 1 < n)
    def _():
        issue_block(i + 1, (i + 1) % 2)

    for k in range(TOP_K):
        pltpu.make_async_copy(y_ref.at[pl.ds(0, bt)], buf.at[slot, k], sem.at[slot]).wait()

    if nbp is None:
        combine_all(o_ref)
    else:
        @pl.when(i < nbp)
        def _():
            combine_all(op_ref)

        @pl.when(i >= nbp)
        def _():
            combine_all(os_ref)


def _finish(dest, yu, rg, x1, mod, g, tp, ls, split):
    t = x1.shape[0]
    bt = FIN_BM
    row = lambda w: pl.BlockSpec((bt, w), lambda i, dst: (i, 0))
    if split:
        nbp = tp // bt
        out_specs = (pl.BlockSpec((bt, D_MODEL), lambda i, dst: (jnp.minimum(i, nbp - 1), 0)),
                     pl.BlockSpec((bt, D_MODEL), lambda i, dst: (jnp.maximum(i - nbp, 0), 0)))
        out_shape = (jax.ShapeDtypeStruct((tp, D_MODEL), F32), jax.ShapeDtypeStruct((t - tp, D_MODEL), F32))
    else:
        nbp = None
        out_specs = row(D_MODEL)
        out_shape = jax.ShapeDtypeStruct((t, D_MODEL), F32)
    grid_spec = pltpu.PrefetchScalarGridSpec(
        num_scalar_prefetch=1,
        grid=(t // bt,),
        in_specs=[pl.BlockSpec(memory_space=pl.ANY),
                  row(ROUTER_PAD),
                  row(D_MODEL),
                  pl.BlockSpec((1, 6, D_MODEL), lambda i, dst: (_mod_row(i, bt, tp, ls), 0, 0)),
                  pl.BlockSpec((1, D_MODEL), lambda i, dst: (0, 0))],
        out_specs=out_specs,
        scratch_shapes=[pltpu.VMEM((2, TOP_K, bt, D_MODEL // 2), jnp.uint32), pltpu.SemaphoreType.DMA((2,))],
    )
    return pl.pallas_call(
        functools.partial(_finish_kernel, bt=bt, nbp=nbp),
        grid_spec=grid_spec,
        out_shape=out_shape,
        compiler_params=_cp(("arbitrary",), 48),
        name="moe_finish",
    )(dest, yu, rg, x1, mod, g)


def _moe(hu, rt, rg, cnt, x1, mod, g, w_gu, b_gu, w_dn, b_dn, l, tp, ls, split=False):
    t = hu.shape[0]
    n_blocks = -(-t * TOP_K // MOE_BM) + N_EXPERTS
    counts = cnt[0, :N_EXPERTS].astype(jnp.int32)
    pad_start, layout = _block_layout(counts, n_blocks)
    first = jnp.where(rt[:TOP_K, :, None] == jnp.arange(N_EXPERTS, dtype=jnp.int32), pad_start, 0)
    dest = (jnp.sum(first, axis=-1) + rt[TOP_K:2 * TOP_K]).reshape(-1)
    xs = _dispatch(dest, pad_start, counts, hu, n_blocks * MOE_BM)
    act = _gm1(layout, xs, w_gu, b_gu, l)
    yu = _gm2(layout, act, w_dn, b_dn, l)
    return _finish(dest, yu, rg, x1, mod, g, tp, ls, split)


def kernel(x_prompt, x_sample, cache_attn_k, cache_attn_v, cache_diff_k, cache_diff_v, c, c_ctx, w_mod, b_mod, norm_g, w_in, w_out, hy_conv_w, hy_conv_b, hy_ffn_w1, hy_ffn_b1, hy_ffn_w2, hy_ffn_b2, hy_ffn_w3, hy_sin_freq, hy_skip, sgu_norm_g, sgu_norm_b, sgu_w, sgu_b, attn_q_g, attn_k_g, diff_lambda, diff_norm_g, w_router, b_router, w_gate_up, b_gate_up, w_down, b_down):
    bp, lp, d = x_prompt.shape
    bs, ls, _ = x_sample.shape
    tp = bp * lp
    past = cache_attn_k.shape[2]
    x = jnp.concatenate([x_prompt.reshape(tp, d), x_sample.reshape(bs * ls, d)], axis=0)

    cond = jnp.concatenate([c_ctx[None, :], c, jnp.zeros((COND_ROWS - 1 - bs, d), F32)], axis=0)
    mod = _modulation(cond, w_mod, b_mod).reshape(DEPTH, COND_ROWS, 6, d)

    w_in_b = w_in.astype(BF16)
    w_out_b = w_out.astype(BF16)
    wr_f = jnp.pad(w_router, ((0, 0), (0, 0), (0, ROUTER_PAD - N_EXPERTS)))
    wr_hi = wr_f.astype(BF16)
    wr = jnp.concatenate([wr_hi, (wr_f - wr_hi.astype(F32)).astype(BF16)], axis=-1)
    br = jnp.pad(b_router, ((0, 0), (0, ROUTER_PAD - N_EXPERTS))).reshape(DEPTH, 1, ROUTER_PAD)

    pk = cache_attn_k.reshape(bs, DEPTH, past, -1)
    pv = cache_attn_v.reshape(bs, DEPTH, past, -1)
    pdk = cache_diff_k.reshape(bs, DEPTH, past, -1)
    pdv = cache_diff_v.reshape(bs, DEPTH, past, -1)

    groups = []
    for L in (lp, ls):
        cm, sm, alt8, fwd, inv = _dft_tables(L)
        z, window = _filter_consts(L)
        groups.append((L, cm, sm, alt8, fwd, inv, z, window))
    qw = ATTN_HEADS * ATTN_HEAD_DIM
    kw = ATTN_KV_HEADS * ATTN_HEAD_DIM
    attn_tables = _rope_tables(ls, ATTN_HEAD_DIM, qw) + _rope_tables(ls, ATTN_HEAD_DIM, kw)
    diff_tables = _rope_tables(ls, DIFF_QK_DIM, GROUP_WIDTH)

    new_k, new_v, new_dk, new_dv = [], [], [], []
    for l in range(DEPTH):
        ng = norm_g[l]
        proj = _in_proj(x, mod[l], ng[0:1], w_in_b[l], tp, ls)

        ys = []
        for gi, (B, row_off) in enumerate(((bp, 0), (bs, tp // ls))):
            L, cm, sm, alt8, fwd, inv, z, window = groups[gi]
            kre, kim = _hyena_filter(L, z, window, cm, sm, alt8, hy_ffn_w1[l], hy_ffn_b1[l], hy_ffn_w2[l],
                                     hy_ffn_b2[l], hy_ffn_w3[l], hy_sin_freq[l])
            ya = _hyena(proj, row_off, B, L, hy_conv_w[l], hy_conv_b[l], kre, kim, hy_skip[l], fwd, inv)
            yb = _sgu(proj, row_off, B, L, sgu_norm_g[l], sgu_norm_b[l], sgu_w[l], sgu_b[l])
            if gi == 0:
                yc, k_ctx, v_ctx = _gqa(proj, row_off, B, L, attn_q_g[l], attn_k_g[l], l)
                yd, dk_ctx, dv_ctx = _diff(proj, row_off, B, L, diff_lambda[l], diff_norm_g[l], l)
                new_k.append(k_ctx.reshape(bp, lp, ATTN_KV_HEADS, ATTN_HEAD_DIM))
                new_v.append(v_ctx.reshape(bp, lp, ATTN_KV_HEADS, ATTN_HEAD_DIM))
                new_dk.append(dk_ctx.reshape(bp, lp, DIFF_HEADS, 2, DIFF_QK_DIM))
                new_dv.append(dv_ctx.reshape(bp, lp, DIFF_HEADS, DIFF_V_DIM))
            else:
                yc = _gqa(proj, row_off, B, L, attn_q_g[l], attn_k_g[l], l, past=(pk, pv), tables=attn_tables)
                yd = _diff(proj, row_off, B, L, diff_lambda[l], diff_norm_g[l], l, past=(pdk, pdv),
                           tables=diff_tables)
            ys.append((ya, yb, yc, yd))
        x1, hu, rt, rg, cnt = _out_proj(ys[0], ys[1], w_out_b[l], x, mod[l], ng, wr[l], br[l], tp, ls)
        x = _moe(hu, rt, rg, cnt, x1, mod[l], ng[3:4], w_gate_up, b_gate_up, w_down, b_down, l, tp, ls,
                 split=(l == DEPTH - 1))

    y_prompt = x[0].reshape(bp, lp, d)
    y_sample = x[1].reshape(bs, ls, d)
    return (y_prompt, y_sample, jnp.stack(new_k, axis=1), jnp.stack(new_v, axis=1),
            jnp.stack(new_dk, axis=1), jnp.stack(new_dv, axis=1))
```

```python
import functools
import math

import numpy as np
import jax
import jax.numpy as jnp
from jax import lax
from jax.experimental import pallas as pl
from jax.experimental.pallas import tpu as pltpu

F32 = jnp.float32
BF16 = jnp.bfloat16
HIGHEST = lax.Precision.HIGHEST

D_MODEL = 2048
DEPTH = 2
GRID_W = 64
GROUP_WIDTH = 512
EPS = 1e-6
IN_COLS = 5120
HYENA_BANDS = 16
HYENA_POS_DIM = 1 + 2 * HYENA_BANDS
HYENA_FFN = 64
HYENA_FAST_DECAY = 0.3
HYENA_SLOW_DECAY = 1.5
HYENA_TARGET = 0.01
POS_PAD = 128
CHUNK = 128
SGU_HEADS = 4
ATTN_HEADS = 4
ATTN_KV_HEADS = 2
ATTN_HEAD_DIM = 128
ROPE_THETA = 10000.0
DIFF_HEADS = 4
DIFF_QK_DIM = 64
DIFF_V_DIM = 128
N_EXPERTS = 32
TOP_K = 4
D_FF = 2048
SWIGLU_LIMIT = 7.0
SWIGLU_ALPHA = 1.702
ROUTER_PAD = 128

IN_BM, IN_BN = 1024, 1024
OUT_BM = 256
MOE_BM = 512
GM1_TN = 1024
GM1_SUB = 256
GM2_TN = 2048
DISPATCH_BT = 512
FIN_BM = 256
FIN_ROWS = 32
FIN_UNROLL = 2
COND_ROWS = 16
MOD_TN = 1024
HY_CC = 256
HY_ROWS = 1024
CTX_ROWS = 1024
Q_CHUNK = 512
MIB = 2 ** 20


def _cp(sem, vmem_mib):
    return pltpu.CompilerParams(dimension_semantics=sem, vmem_limit_bytes=vmem_mib * MIB)


def _mod_kernel(c_ref, w_ref, b_ref, o_ref):
    c = c_ref[...]
    s = c * jax.nn.sigmoid(c)
    o_ref[...] = jnp.dot(s.astype(BF16), w_ref[...].astype(BF16), preferred_element_type=F32) + b_ref[...]


def _modulation(cond, w_mod, b_mod):
    n = w_mod.shape[-1]
    return pl.pallas_call(
        _mod_kernel,
        grid=(DEPTH, n // MOD_TN),
        in_specs=[
            pl.BlockSpec((COND_ROWS, D_MODEL), lambda l, j: (0, 0)),
            pl.BlockSpec((None, D_MODEL, MOD_TN), lambda l, j: (l, 0, j)),
            pl.BlockSpec((None, 1, MOD_TN), lambda l, j: (l, 0, j)),
        ],
        out_specs=pl.BlockSpec((None, COND_ROWS, MOD_TN), lambda l, j: (l, 0, j)),
        out_shape=jax.ShapeDtypeStruct((DEPTH, COND_ROWS, n), F32),
        compiler_params=_cp(("arbitrary", "arbitrary"), 40),
        name="modulation",
    )(cond, w_mod, b_mod.reshape(DEPTH, 1, n))


def _mod_row(i, bm, tp, ls):
    nbp = tp // bm
    return jnp.where(i < nbp, 0, 1 + (i - nbp) // (ls // bm))


def _in_proj_kernel(x_ref, mod_ref, g_ref, w_ref, o_ref, h_ref):
    @pl.when(pl.program_id(1) == 0)
    def _():
        x = x_ref[...]
        r = lax.rsqrt(jnp.mean(x * x, axis=-1, keepdims=True) + EPS)
        h = (x * r) * g_ref[...] * (1.0 + mod_ref[0, 1:2, :]) + mod_ref[0, 0:1, :]
        h_ref[...] = h.astype(BF16)

    o_ref[...] = jnp.dot(h_ref[...], w_ref[...], preferred_element_type=F32)


def _in_proj(x, mod, g, w, tp, ls):
    t = x.shape[0]
    bm = min(IN_BM, ls, tp)
    return pl.pallas_call(
        _in_proj_kernel,
        grid=(t // bm, IN_COLS // IN_BN),
        in_specs=[
            pl.BlockSpec((bm, D_MODEL), lambda i, j: (i, 0)),
            pl.BlockSpec((1, 6, D_MODEL), lambda i, j: (_mod_row(i, bm, tp, ls), 0, 0)),
            pl.BlockSpec((1, D_MODEL), lambda i, j: (0, 0)),
            pl.BlockSpec((D_MODEL, IN_BN), lambda i, j: (0, j)),
        ],
        out_specs=pl.BlockSpec((bm, IN_BN), lambda i, j: (i, j)),
        out_shape=jax.ShapeDtypeStruct((t, IN_COLS), F32),
        scratch_shapes=[pltpu.VMEM((bm, D_MODEL), BF16)],
        compiler_params=_cp(("arbitrary", "arbitrary"), 56),
        name="in_proj",
    )(x, mod, g, w)


def _dft_tables(L):
    k = np.arange(L, dtype=np.int64)
    ang = ((k[:, None] * k[None, :]) % (2 * L)).astype(np.float64) * (math.pi / L)
    cm = np.cos(ang)
    sm = np.sin(ang)
    alt = np.where(k % 2 == 0, 1.0, -1.0)
    sf = -sm
    sf[0] = alt
    fwd = np.concatenate([cm, sf], axis=0)
    wk = np.where(k == 0, 1.0, 2.0) / (2 * L)
    si = -sm / L
    si[:, 0] = alt / (2 * L)
    inv = np.concatenate([cm * wk[None, :], si], axis=1)
    alt8 = np.zeros((8, L))
    alt8[0] = alt
    return (jnp.asarray(cm, F32), jnp.asarray(sm, F32), jnp.asarray(alt8, F32),
            jnp.asarray(fwd.astype(BF16)), jnp.asarray(inv.astype(BF16)))


def _filter_consts(L):
    t = np.linspace(0.0, 1.0, L, dtype=np.float32)[:, None]
    bands = np.linspace(1e-4, HYENA_BANDS - 1, HYENA_BANDS, dtype=np.float32)[None, :]
    z = np.concatenate([t, np.cos(2 * math.pi * bands * t), np.sin(2 * math.pi * bands * t)], axis=-1)
    z = np.pad(z, ((0, 0), (0, POS_PAD - HYENA_POS_DIM)))
    max_decay = math.log(HYENA_TARGET) / HYENA_FAST_DECAY
    min_decay = math.log(HYENA_TARGET) / HYENA_SLOW_DECAY
    deltas = np.abs(np.linspace(min_decay, max_decay, GROUP_WIDTH, dtype=np.float32))
    window = np.exp(-t * deltas[None, :])
    return jnp.asarray(z, F32), jnp.asarray(window, F32)


def _filter_kernel(z_ref, w1_ref, b1_ref, w2_ref, b2_ref, w3_ref, fr_ref, win_ref, cm_ref, sm_ref, alt_ref,
                   kre_ref, kim_ref):
    c = GROUP_WIDTH
    h = jnp.dot(z_ref[...], w1_ref[...], precision=HIGHEST, preferred_element_type=F32) + b1_ref[...]
    h = jnp.sin(fr_ref[0:1, :] * h)
    h = jnp.dot(h, w2_ref[...], precision=HIGHEST, preferred_element_type=F32) + b2_ref[...]
    h = jnp.sin(fr_ref[1:2, :] * h)
    hfb = jnp.dot(h, w3_ref[...], precision=HIGHEST, preferred_element_type=F32)
    win = win_ref[...]
    hf = hfb[:, :c] * win
    hb = hfb[:, c:] * win
    row = lax.broadcasted_iota(jnp.int32, hf.shape, 0)
    g = hf + jnp.where(row == 0, 0.0, hb)
    kre = jnp.dot(cm_ref[...], g, precision=HIGHEST, preferred_element_type=F32)
    kim = jnp.dot(sm_ref[...], hb - hf, precision=HIGHEST, preferred_element_type=F32)
    kn = jnp.dot(alt_ref[...], g, precision=HIGHEST, preferred_element_type=F32)[0:1, :]
    kre_ref[...] = kre
    kim_ref[...] = jnp.where(row == 0, kn, kim)


def _hyena_filter(L, z, window, cm, sm, alt8, w1, b1, w2, b2, w3, freq):
    w1p = jnp.pad(w1, ((0, POS_PAD - HYENA_POS_DIM), (0, 0)))
    return pl.pallas_call(
        _filter_kernel,
        out_shape=(jax.ShapeDtypeStruct((L, GROUP_WIDTH), F32), jax.ShapeDtypeStruct((L, GROUP_WIDTH), F32)),
        compiler_params=pltpu.CompilerParams(vmem_limit_bytes=56 * MIB),
        name="hyena_filter",
    )(z, w1p, b1.reshape(1, -1), w2, b2.reshape(1, -1), w3, freq, window, cm, sm, alt8)


def _short_conv(a, w, b, row, L):
    prev = jnp.where(row == 0, 0.0, pltpu.roll(a, 1, 0))
    nxt = jnp.where(row == L - 1, 0.0, pltpu.roll(a, L - 1, 0))
    return b + prev * w[0:1, :] + a * w[1:2, :] + nxt * w[2:3, :]


def _hyena_kernel(a0_ref, a1_ref, a2_ref, w0_ref, w1_ref, w2_ref, b0_ref, b1_ref, b2_ref, kre_ref, kim_ref,
                  skip_ref, fwd_ref, inv_ref, o_ref, *, L, G):
    cc = HY_CC

    def lanes(ref):
        return ref[...] if G == 1 else jnp.concatenate([ref[g * L:(g + 1) * L, :] for g in range(G)], axis=1)

    def tiled(v):
        return v if G == 1 else jnp.concatenate([v] * G, axis=1)

    row = lax.broadcasted_iota(jnp.int32, (L, G * cc), 0)
    x1 = _short_conv(lanes(a1_ref), tiled(w1_ref[...]), tiled(b1_ref[...]), row, L)
    hv = _short_conv(lanes(a2_ref), tiled(w2_ref[...]), tiled(b2_ref[...]), row, L)
    u = hv * x1
    spec = jnp.dot(fwd_ref[...], u.astype(BF16), preferred_element_type=F32)
    ure, uim = spec[:L], spec[L:]
    kre, kim = tiled(kre_ref[...]), tiled(kim_ref[...])
    first = row == 0
    yre = ure * kre - jnp.where(first, 0.0, uim * kim)
    yim = jnp.where(first, uim * kim, ure * kim + uim * kre)
    yspec = jnp.concatenate([yre, yim], axis=0).astype(BF16)
    y = jnp.dot(inv_ref[...], yspec, preferred_element_type=F32)
    x0 = _short_conv(lanes(a0_ref), tiled(w0_ref[...]), tiled(b0_ref[...]), row, L)
    out = (x0 * (y + u * tiled(skip_ref[...]))).astype(BF16)
    for g in range(G):
        o_ref[g * L:(g + 1) * L, :] = out[:, g * cc:(g + 1) * cc]


def _hyena(proj, row_off, B, L, conv_w, conv_b, kre, kim, skip, fwd, inv):
    ncc = GROUP_WIDTH // HY_CC
    G = max(1, min(B, HY_ROWS // L))
    assert row_off % G == 0 and B % G == 0

    def a_spec(part):
        return pl.BlockSpec((G * L, HY_CC), lambda b, cc: (row_off // G + b, part * ncc + cc))

    def w_spec(part, rows):
        return pl.BlockSpec((rows, HY_CC), lambda b, cc: (0, part * ncc + cc))

    chan = pl.BlockSpec((L, HY_CC), lambda b, cc: (0, cc))
    conv_b = conv_b.reshape(1, -1)
    return pl.pallas_call(
        functools.partial(_hyena_kernel, L=L, G=G),
        grid=(B // G, ncc),
        in_specs=[a_spec(0), a_spec(1), a_spec(2), w_spec(0, 3), w_spec(1, 3), w_spec(2, 3),
                  w_spec(0, 1), w_spec(1, 1), w_spec(2, 1), chan, chan,
                  pl.BlockSpec((1, HY_CC), lambda b, cc: (0, cc)),
                  pl.BlockSpec((2 * L, L), lambda b, cc: (0, 0)),
                  pl.BlockSpec((L, 2 * L), lambda b, cc: (0, 0))],
        out_specs=pl.BlockSpec((G * L, HY_CC), lambda b, cc: (b, cc)),
        out_shape=jax.ShapeDtypeStruct((B * L, GROUP_WIDTH), BF16),
        compiler_params=_cp(("arbitrary", "arbitrary"), 56),
        name="hyena_conv",
    )(proj, proj, proj, conv_w, conv_w, conv_w, conv_b, conv_b, conv_b, kre, kim, skip.reshape(1, -1), fwd, inv)


def _sgu_kernel(u_ref, v_ref, g_ref, b_ref, w_ref, bs_ref, o_ref, *, L):
    v = v_ref[...]
    mu = jnp.mean(v, axis=-1, keepdims=True)
    xc = v - mu
    vn = xc * lax.rsqrt(jnp.mean(xc * xc, axis=-1, keepdims=True) + EPS) * g_ref[...] + b_ref[...]
    vnb = vn.astype(BF16)
    hc = GROUP_WIDTH // SGU_HEADS
    for h in range(SGU_HEADS):
        wh = w_ref[h].astype(BF16)
        bias = bs_ref[h]
        cols = slice(h * hc, (h + 1) * hc)
        for n in range(L // CHUNK):
            rows = slice(n * CHUNK, (n + 1) * CHUNK)
            mixed = jnp.dot(wh, vnb[rows, cols], preferred_element_type=F32) + bias
            o_ref[rows, cols] = (u_ref[rows, cols] * mixed).astype(BF16)


def _sgu(proj, row_off, B, L, g, b, w_s, b_s):
    hc = GROUP_WIDTH // SGU_HEADS
    bias = jnp.broadcast_to(b_s[:, :, None], (SGU_HEADS, CHUNK, hc))
    return pl.pallas_call(
        functools.partial(_sgu_kernel, L=L),
        grid=(B,),
        in_specs=[pl.BlockSpec((L, GROUP_WIDTH), lambda i: (row_off + i, 3)),
                  pl.BlockSpec((L, GROUP_WIDTH), lambda i: (row_off + i, 4)),
                  pl.BlockSpec((1, GROUP_WIDTH), lambda i: (0, 0)),
                  pl.BlockSpec((1, GROUP_WIDTH), lambda i: (0, 0)),
                  pl.BlockSpec((SGU_HEADS, CHUNK, CHUNK), lambda i: (0, 0, 0)),
                  pl.BlockSpec((SGU_HEADS, CHUNK, hc), lambda i: (0, 0, 0))],
        out_specs=pl.BlockSpec((L, GROUP_WIDTH), lambda i: (i, 0)),
        out_shape=jax.ShapeDtypeStruct((B * L, GROUP_WIDTH), BF16),
        compiler_params=_cp(("arbitrary",), 40),
        name="sgu",
    )(proj, proj, g.reshape(1, -1), b.reshape(1, -1), w_s, bias)


def _rope_tables(L, dh, width):
    quarter = dh // 4
    pos = np.arange(L)
    rowp = (pos // GRID_W).astype(np.float32)
    colp = (pos % GRID_W).astype(np.float32)
    inv_freq = (ROPE_THETA ** (-np.arange(quarter, dtype=np.float32) / quarter)).astype(np.float32)
    ar = (rowp[:, None] * inv_freq[None, :]).astype(np.float64)
    ac = (colp[:, None] * inv_freq[None, :]).astype(np.float64)
    cos = np.concatenate([np.cos(ar), np.cos(ar), np.cos(ac), np.cos(ac)], axis=-1)
    sin = np.concatenate([-np.sin(ar), np.sin(ar), -np.sin(ac), np.sin(ac)], axis=-1)
    reps = width // dh
    return jnp.asarray(np.tile(cos, (1, reps)), F32), jnp.asarray(np.tile(sin, (1, reps)), F32)


def _rope(x, cos, sin, quarter):
    w = x.shape[-1]
    lane = lax.broadcasted_iota(jnp.int32, x.shape, 1)
    partner = jnp.where((lane & quarter) != 0, pltpu.roll(x, quarter, 1), pltpu.roll(x, w - quarter, 1))
    return x * cos + partner * sin


def _head_rms(x, g, heads, dh):
    outs = []
    for h in range(heads):
        xh = x[:, h * dh:(h + 1) * dh]
        outs.append(xh * lax.rsqrt(jnp.mean(xh * xh, axis=-1, keepdims=True) + EPS) * g)
    return jnp.concatenate(outs, axis=-1)


def _softmax_pv(q, k, v1):
    dv = v1.shape[1] // 2
    s = lax.dot_general(q, k, (((1,), (1,)), ((), ())), preferred_element_type=F32)
    e = jnp.exp(s - jnp.max(s, axis=-1, keepdims=True))
    pv = jnp.dot(e.astype(BF16), v1, preferred_element_type=F32)
    return pv[:, :dv], 1.0 / pv[:, dv:]


def _store_values(vb_ref, rows, v, heads, dv):
    n = v.shape[0]
    for h in range(heads):
        vb_ref[rows, 2 * h * dv:(2 * h + 1) * dv] = v[:, h * dv:(h + 1) * dv].astype(BF16)
        vb_ref[rows, (2 * h + 1) * dv:(2 * h + 2) * dv] = jnp.ones((n, dv), BF16)


def _gqa_body(q, k_ref, v_ref, o_ref, L):
    dh = ATTN_HEAD_DIM
    group = ATTN_HEADS // ATTN_KV_HEADS
    qs = (q * dh ** -0.5).astype(BF16)
    qc = min(Q_CHUNK, L)
    for h in range(ATTN_HEADS):
        kv = h // group
        kh = k_ref[:, kv * dh:(kv + 1) * dh]
        vh = v_ref[:, 2 * kv * dh:(2 * kv + 2) * dh]
        for c in range(L // qc):
            rows = slice(c * qc, (c + 1) * qc)
            pv, inv_l = _softmax_pv(qs[rows, h * dh:(h + 1) * dh], kh, vh)
            o_ref[rows, h * dh:(h + 1) * dh] = (pv * inv_l).astype(BF16)


def _gqa_ctx_kernel(q_ref, k_ref, v_ref, gq_ref, gk_ref, o_ref, ko_ref, vo_ref, kb_ref, vb_ref, *, L, G):
    for g in range(G):
        rows = slice(g * L, (g + 1) * L)
        q = _head_rms(q_ref[rows, :], gq_ref[...], ATTN_HEADS, ATTN_HEAD_DIM)
        k = _head_rms(k_ref[rows, :], gk_ref[...], ATTN_KV_HEADS, ATTN_HEAD_DIM)
        v = v_ref[rows, :]
        ko_ref[rows, :] = k
        vo_ref[rows, :] = v
        kb_ref[rows, :] = k.astype(BF16)
        _store_values(vb_ref, rows, v, ATTN_KV_HEADS, ATTN_HEAD_DIM)
        seq = pl.ds(g * L, L)
        _gqa_body(q, kb_ref.at[seq], vb_ref.at[seq], o_ref.at[seq], L)


def _gqa_lat_kernel(q_ref, k_ref, v_ref, gq_ref, gk_ref, cq_ref, sq_ref, ck_ref, sk_ref, pk_ref, pv_ref, o_ref,
                    kb_ref, vb_ref, *, L, P):
    quarter = ATTN_HEAD_DIM // 4
    q = _rope(_head_rms(q_ref[...], gq_ref[...], ATTN_HEADS, ATTN_HEAD_DIM), cq_ref[...], sq_ref[...], quarter)
    k = _rope(_head_rms(k_ref[...], gk_ref[...], ATTN_KV_HEADS, ATTN_HEAD_DIM), ck_ref[...], sk_ref[...], quarter)
    kb_ref[0:P, :] = pk_ref[...].astype(BF16)
    kb_ref[P:P + L, :] = k.astype(BF16)
    _store_values(vb_ref, slice(0, P), pv_ref[...], ATTN_KV_HEADS, ATTN_HEAD_DIM)
    _store_values(vb_ref, slice(P, P + L), v_ref[...], ATTN_KV_HEADS, ATTN_HEAD_DIM)
    _gqa_body(q, kb_ref, vb_ref, o_ref, L)


def _gqa(proj, row_off, B, L, gq, gk, l, past=None, tables=None):
    qw = ATTN_HEADS * ATTN_HEAD_DIM
    kw = ATTN_KV_HEADS * ATTN_HEAD_DIM
    qoff = 5
    koff = (2560 + qw) // kw
    in_specs = [pl.BlockSpec((L, qw), lambda i: (row_off + i, qoff)),
                pl.BlockSpec((L, kw), lambda i: (row_off + i, koff)),
                pl.BlockSpec((L, kw), lambda i: (row_off + i, koff + 1)),
                pl.BlockSpec((1, ATTN_HEAD_DIM), lambda i: (0, 0)),
                pl.BlockSpec((1, ATTN_HEAD_DIM), lambda i: (0, 0))]
    args = [proj, proj, proj, gq.reshape(1, -1), gk.reshape(1, -1)]
    y_spec = pl.BlockSpec((L, qw), lambda i: (i, 0))
    y_shape = jax.ShapeDtypeStruct((B * L, qw), BF16)
    if past is None:
        G = max(1, min(B, CTX_ROWS // L))
        assert row_off % G == 0 and B % G == 0
        blk = lambda w, off: pl.BlockSpec((G * L, w), lambda i: (row_off // G + i, off))
        out = lambda w: pl.BlockSpec((G * L, w), lambda i: (i, 0))
        return pl.pallas_call(
            functools.partial(_gqa_ctx_kernel, L=L, G=G),
            grid=(B // G,),
            in_specs=[blk(qw, qoff), blk(kw, koff), blk(kw, koff + 1)] + in_specs[3:],
            out_specs=(out(qw), out(kw), out(kw)),
            out_shape=(y_shape, jax.ShapeDtypeStruct((B * L, kw), F32), jax.ShapeDtypeStruct((B * L, kw), F32)),
            scratch_shapes=[pltpu.VMEM((G * L, kw), BF16), pltpu.VMEM((G * L, 2 * kw), BF16)],
            compiler_params=_cp(("arbitrary",), 40),
            name="gqa_ctx",
        )(*args)
    pk, pv = past
    P = pk.shape[2]
    cq, sq, ck, sk = tables
    const = lambda w: pl.BlockSpec((L, w), lambda i: (0, 0))
    cache = pl.BlockSpec((None, None, P, kw), lambda i: (i, l, 0, 0))
    return pl.pallas_call(
        functools.partial(_gqa_lat_kernel, L=L, P=P),
        grid=(B,),
        in_specs=in_specs + [const(qw), const(qw), const(kw), const(kw), cache, cache],
        out_specs=y_spec,
        out_shape=y_shape,
        scratch_shapes=[pltpu.VMEM((P + L, kw), BF16), pltpu.VMEM((P + L, 2 * kw), BF16)],
        compiler_params=_cp(("arbitrary",), 56),
        name="gqa_lat",
    )(*args, cq, sq, ck, sk, pk, pv)


def _diff_body(dq, k_ref, v_ref, dl_ref, g_ref, o_ref, L, lambda_init):
    dv = DIFF_V_DIM
    dl = dl_ref[...]
    lam = (jnp.exp(jnp.sum(dl[0:1] * dl[1:2], axis=-1, keepdims=True))
           - jnp.exp(jnp.sum(dl[2:3] * dl[3:4], axis=-1, keepdims=True)) + lambda_init)
    qs = dq * DIFF_QK_DIM ** -0.5
    first_map = lax.broadcasted_iota(jnp.int32, (1, dv), 1) < DIFF_QK_DIM
    qc = min(Q_CHUNK, L)
    for h in range(DIFF_HEADS):
        cols = slice(h * dv, (h + 1) * dv)
        kh = k_ref[:, cols]
        vh = v_ref[:, 2 * h * dv:(2 * h + 2) * dv]
        for c in range(L // qc):
            rows = slice(c * qc, (c + 1) * qc)
            qh = qs[rows, cols]
            pv0, inv0 = _softmax_pv(jnp.where(first_map, qh, 0.0).astype(BF16), kh, vh)
            pv1, inv1 = _softmax_pv(jnp.where(first_map, 0.0, qh).astype(BF16), kh, vh)
            o = pv0 * inv0 - pv1 * (lam * inv1)
            r = lax.rsqrt(jnp.mean(o * o, axis=-1, keepdims=True) + EPS)
            o_ref[rows, cols] = ((o * r) * g_ref[...] * (1.0 - lambda_init)).astype(BF16)


def _diff_ctx_kernel(q_ref, k_ref, v_ref, dl_ref, g_ref, o_ref, ko_ref, vo_ref, kb_ref, vb_ref, *, L, G, lambda_init):
    for g in range(G):
        rows = slice(g * L, (g + 1) * L)
        k = k_ref[rows, :]
        v = v_ref[rows, :]
        ko_ref[rows, :] = k
        vo_ref[rows, :] = v
        kb_ref[rows, :] = k.astype(BF16)
        _store_values(vb_ref, rows, v, DIFF_HEADS, DIFF_V_DIM)
        seq = pl.ds(g * L, L)
        _diff_body(q_ref[rows, :], kb_ref.at[seq], vb_ref.at[seq], dl_ref, g_ref, o_ref.at[seq], L, lambda_init)


def _diff_lat_kernel(q_ref, k_ref, v_ref, dl_ref, g_ref, cos_ref, sin_ref, pk_ref, pv_ref, o_ref, kb_ref, vb_ref,
                     *, L, P, lambda_init):
    quarter = DIFF_QK_DIM // 4
    cos, sin = cos_ref[...], sin_ref[...]
    q = _rope(q_ref[...], cos, sin, quarter)
    k = _rope(k_ref[...], cos, sin, quarter)
    kb_ref[0:P, :] = pk_ref[...].astype(BF16)
    kb_ref[P:P + L, :] = k.astype(BF16)
    _store_values(vb_ref, slice(0, P), pv_ref[...], DIFF_HEADS, DIFF_V_DIM)
    _store_values(vb_ref, slice(P, P + L), v_ref[...], DIFF_HEADS, DIFF_V_DIM)
    _diff_body(q, kb_ref, vb_ref, dl_ref, g_ref, o_ref, L, lambda_init)


def _diff(proj, row_off, B, L, dl, g, l, past=None, tables=None):
    w = GROUP_WIDTH
    lambda_init = 0.8 - 0.6 * math.exp(-0.3 * l)
    in_specs = [pl.BlockSpec((L, w), lambda i: (row_off + i, 7)),
                pl.BlockSpec((L, w), lambda i: (row_off + i, 8)),
                pl.BlockSpec((L, w), lambda i: (row_off + i, 9)),
                pl.BlockSpec((4, DIFF_QK_DIM), lambda i: (0, 0)),
                pl.BlockSpec((1, DIFF_V_DIM), lambda i: (0, 0))]
    args = [proj, proj, proj, dl, g.reshape(1, -1)]
    y_spec = pl.BlockSpec((L, w), lambda i: (i, 0))
    y_shape = jax.ShapeDtypeStruct((B * L, w), BF16)
    if past is None:
        G = max(1, min(B, CTX_ROWS // L))
        assert row_off % G == 0 and B % G == 0
        blk = lambda off: pl.BlockSpec((G * L, w), lambda i: (row_off // G + i, off))
        out = pl.BlockSpec((G * L, w), lambda i: (i, 0))
        return pl.pallas_call(
            functools.partial(_diff_ctx_kernel, L=L, G=G, lambda_init=lambda_init),
            grid=(B // G,),
            in_specs=[blk(7), blk(8), blk(9)] + in_specs[3:],
            out_specs=(out, out, out),
            out_shape=(y_shape, jax.ShapeDtypeStruct((B * L, w), F32), jax.ShapeDtypeStruct((B * L, w), F32)),
            scratch_shapes=[pltpu.VMEM((G * L, w), BF16), pltpu.VMEM((G * L, 2 * w), BF16)],
            compiler_params=_cp(("arbitrary",), 48),
            name="diff_ctx",
        )(*args)
    pk, pv = past
    P = pk.shape[2]
    cos, sin = tables
    const = pl.BlockSpec((L, w), lambda i: (0, 0))
    cache = pl.BlockSpec((None, None, P, w), lambda i: (i, l, 0, 0))
    return pl.pallas_call(
        functools.partial(_diff_lat_kernel, L=L, P=P, lambda_init=lambda_init),
        grid=(B,),
        in_specs=in_specs + [const, const, cache, cache],
        out_specs=y_spec,
        out_shape=y_shape,
        scratch_shapes=[pltpu.VMEM((P + L, w), BF16), pltpu.VMEM((P + L, 2 * w), BF16)],
        compiler_params=_cp(("arbitrary",), 56),
        name="diff_lat",
    )(*args, cos, sin, pk, pv)


def _pack_halves(lo, hi):
    lo_bits = pltpu.bitcast(lo.astype(BF16).astype(F32), jnp.uint32) >> 16
    hi_bits = pltpu.bitcast(hi.astype(BF16).astype(F32), jnp.uint32) & jnp.uint32(0xFFFF0000)
    return lo_bits | hi_bits


def _unpack_halves(words):
    lo = pltpu.bitcast(words << 16, F32)
    hi = pltpu.bitcast(words & jnp.uint32(0xFFFF0000), F32)
    return lo, hi


def _route_rows(logits, prefix_base, tri_ref):
    lane = lax.broadcasted_iota(jnp.int32, logits.shape, 1)
    neg = jnp.float32(-jnp.inf)
    work = jnp.where(lane < N_EXPERTS, logits, neg)
    sels, vals, idxs = [], [], []
    for _ in range(TOP_K):
        mk = jnp.max(work, axis=-1, keepdims=True)
        ik = jnp.min(jnp.where(work == mk, lane, ROUTER_PAD), axis=-1, keepdims=True)
        sel = lane == ik
        work = jnp.where(sel, neg, work)
        sels.append(sel)
        vals.append(mk)
        idxs.append(ik)
    es = [jnp.exp(v - vals[0]) for v in vals]
    inv = 1.0 / (es[0] + es[1] + es[2] + es[3])
    member = jnp.zeros(logits.shape, F32)
    for sel in sels:
        member = member + jnp.where(sel, 1.0, 0.0)
    prefix = jnp.dot(tri_ref[...], member.astype(BF16), preferred_element_type=F32) + prefix_base
    ri = jnp.zeros(logits.shape, jnp.int32)
    rg = jnp.zeros(logits.shape, F32)
    for k in range(TOP_K):
        rank = jnp.sum(jnp.where(sels[k], prefix, 0.0), axis=-1, keepdims=True).astype(jnp.int32)
        ri = jnp.where(lane == k, idxs[k], ri)
        ri = jnp.where(lane == TOP_K + k, rank, ri)
        rg = jnp.where(lane == k, es[k] * inv, rg)
    return ri, rg, jnp.sum(member, axis=0, keepdims=True)


def _out_proj_kernel(pa_ref, pb_ref, pc_ref, pd_ref, sa_ref, sb_ref, sc_ref, sd_ref, w_ref, x_ref, mod_ref, ng_ref,
                     wr_ref, br_ref, tri_ref, x1_ref, hu_ref, rt_ref, rg_ref, cnt_ref, cat_ref, carry_ref, *, nbp):
    i = pl.program_id(0)

    @pl.when(i == 0)
    def _():
        carry_ref[...] = jnp.zeros_like(carry_ref)

    gw = GROUP_WIDTH

    @pl.when(i < nbp)
    def _():
        for m_, ref in enumerate((pa_ref, pb_ref, pc_ref, pd_ref)):
            cat_ref[:, m_ * gw:(m_ + 1) * gw] = ref[...]

    @pl.when(i >= nbp)
    def _():
        for m_, ref in enumerate((sa_ref, sb_ref, sc_ref, sd_ref)):
            cat_ref[:, m_ * gw:(m_ + 1) * gw] = ref[...]

    m = jnp.dot(cat_ref[...], w_ref[...], preferred_element_type=F32)
    mn = m * lax.rsqrt(jnp.mean(m * m, axis=-1, keepdims=True) + EPS) * ng_ref[1:2, :]
    x1 = x_ref[...] + mod_ref[0, 2:3, :] * mn
    x1_ref[...] = x1
    hn = x1 * lax.rsqrt(jnp.mean(x1 * x1, axis=-1, keepdims=True) + EPS) * ng_ref[2:3, :]
    h = hn * (1.0 + mod_ref[0, 4:5, :]) + mod_ref[0, 3:4, :]
    half = D_MODEL // 2
    hu_ref[...] = _pack_halves(h[:, :half], h[:, half:])
    hh = h.astype(BF16)
    hl = (h - hh.astype(F32)).astype(BF16)
    both = jnp.dot(hh, wr_ref[...], preferred_element_type=F32)
    low = jnp.dot(hl, wr_ref[:, :ROUTER_PAD], preferred_element_type=F32)
    logits = both[:, :ROUTER_PAD] + both[:, ROUTER_PAD:] + low + br_ref[...]
    ri, rg, counts = _route_rows(logits, carry_ref[0:1, :], tri_ref)
    rt_ref[...] = ri.astype(F32).T[:8, :].astype(jnp.int32)
    rg_ref[...] = rg
    carry_ref[...] = carry_ref[...] + counts
    cnt_ref[...] = carry_ref[...]


def _out_proj(ys_prompt, ys_latent, w, x, mod, ng, wr, br, tp, ls):
    t = x.shape[0]
    bm = OUT_BM
    nbp = tp // bm
    pspec = pl.BlockSpec((bm, GROUP_WIDTH), lambda i: (jnp.minimum(i, nbp - 1), 0))
    sspec = pl.BlockSpec((bm, GROUP_WIDTH), lambda i: (jnp.maximum(i - nbp, 0), 0))
    row = pl.BlockSpec((bm, D_MODEL), lambda i: (i, 0))
    info = pl.BlockSpec((bm, ROUTER_PAD), lambda i: (i, 0))
    tri = jnp.asarray(np.tril(np.ones((bm, bm), np.float32), -1).astype(BF16))
    return pl.pallas_call(
        functools.partial(_out_proj_kernel, nbp=nbp),
        grid=(t // bm,),
        in_specs=[pspec, pspec, pspec, pspec, sspec, sspec, sspec, sspec,
                  pl.BlockSpec((D_MODEL, D_MODEL), lambda i: (0, 0)),
                  row,
                  pl.BlockSpec((1, 6, D_MODEL), lambda i: (_mod_row(i, bm, tp, ls), 0, 0)),
                  pl.BlockSpec((4, D_MODEL), lambda i: (0, 0)),
                  pl.BlockSpec((D_MODEL, 2 * ROUTER_PAD), lambda i: (0, 0)),
                  pl.BlockSpec((1, ROUTER_PAD), lambda i: (0, 0)),
                  pl.BlockSpec((bm, bm), lambda i: (0, 0))],
        out_specs=(row, pl.BlockSpec((bm, D_MODEL // 2), lambda i: (i, 0)), pl.BlockSpec((8, bm), lambda i: (0, i)),
                   info, pl.BlockSpec((8, ROUTER_PAD), lambda i: (0, 0))),
        out_shape=(jax.ShapeDtypeStruct((t, D_MODEL), F32), jax.ShapeDtypeStruct((t, D_MODEL // 2), jnp.uint32),
                   jax.ShapeDtypeStruct((8, t), jnp.int32), jax.ShapeDtypeStruct((t, ROUTER_PAD), F32),
                   jax.ShapeDtypeStruct((8, ROUTER_PAD), F32)),
        scratch_shapes=[pltpu.VMEM((bm, D_MODEL), BF16), pltpu.VMEM((8, ROUTER_PAD), F32)],
        compiler_params=_cp(("arbitrary",), 56),
        name="out_proj",
    )(*ys_prompt, *ys_latent, w, x, mod, ng, wr, br, tri)


def _new_expert(be_ref, i):
    return jnp.logical_or(i == 0, be_ref[jnp.maximum(i - 1, 0)] != be_ref[i])


def _next_tile(nx_ref, be_ref, i, j, nj):
    nxt = nx_ref[i]
    has_run = nxt >= 0
    return jnp.logical_or(has_run, j + 1 < nj), jnp.where(has_run, j, j + 1), jnp.where(has_run, nxt, be_ref[0])


def _gm1_kernel(be_ref, nu_ref, hf_ref, nx_ref, x_ref, w_ref, bg_ref, bu_ref, o_ref, wst_ref, wbf_ref, xb_ref, sem,
                *, l, nj):
    j = pl.program_id(0)
    i = pl.program_id(1)
    used = i < nu_ref[0]
    tn = GM1_TN
    half = D_MODEL // 2

    def tile_copies(jt, e):
        gcol = pl.multiple_of(jt * tn, tn)
        ucol = pl.multiple_of(D_FF + jt * tn, tn)
        return (pltpu.make_async_copy(w_ref.at[l, e, :, pl.ds(gcol, tn)], wst_ref.at[:, pl.ds(0, tn)], sem.at[0]),
                pltpu.make_async_copy(w_ref.at[l, e, :, pl.ds(ucol, tn)], wst_ref.at[:, pl.ds(tn, tn)], sem.at[1]))

    @pl.when(jnp.logical_and(used, _new_expert(be_ref, i)))
    def _():
        e = be_ref[i]

        @pl.when(jnp.logical_and(j == 0, i == 0))
        def _():
            for cp in tile_copies(0, e):
                cp.start()

        for cp in tile_copies(j, e):
            cp.wait()
        wbf_ref[...] = wst_ref[...].astype(BF16)
        exists, jn, en = _next_tile(nx_ref, be_ref, i, j, nj)

        @pl.when(exists)
        def _():
            for cp in tile_copies(jn, en):
                cp.start()

    def compute(rows):
        lo, hi = _unpack_halves(x_ref[:rows, :])
        xb_ref[:rows, :half] = lo.astype(BF16)
        xb_ref[:rows, half:] = hi.astype(BF16)
        for c in range(tn // GM1_SUB):
            cols = slice(c * GM1_SUB, (c + 1) * GM1_SUB)
            ucols = slice(tn + c * GM1_SUB, tn + (c + 1) * GM1_SUB)
            g = jnp.dot(xb_ref[:rows, :], wbf_ref[:, cols], preferred_element_type=F32)
            u = jnp.dot(xb_ref[:rows, :], wbf_ref[:, ucols], preferred_element_type=F32)
            gate = jnp.minimum(g + bg_ref[:, cols], SWIGLU_LIMIT)
            up = jnp.clip(u + bu_ref[:, cols], -SWIGLU_LIMIT, SWIGLU_LIMIT)
            glu = gate * jax.nn.sigmoid(gate * SWIGLU_ALPHA)
            o_ref[:rows, cols] = ((up + 1.0) * glu).astype(BF16)
        if rows < MOE_BM:
            o_ref[rows:, :] = jnp.zeros((MOE_BM - rows, tn), BF16)

    small = hf_ref[i] != 0

    @pl.when(jnp.logical_and(used, jnp.logical_not(small)))
    def _():
        compute(MOE_BM)

    @pl.when(jnp.logical_and(used, small))
    def _():
        compute(MOE_BM // 2)

    @pl.when(jnp.logical_not(used))
    def _():
        o_ref[...] = jnp.zeros_like(o_ref)


def _gm1(layout, xs, w_gu, b_gu, l):
    rows = xs.shape[0]
    nb = rows // MOE_BM
    nj = D_FF // GM1_TN
    b_gu = b_gu.reshape(DEPTH, N_EXPERTS, 1, 2 * D_FF)
    last = lambda i, nu: jnp.minimum(i, nu[0] - 1)
    grid_spec = pltpu.PrefetchScalarGridSpec(
        num_scalar_prefetch=4,
        grid=(nj, nb),
        in_specs=[
            pl.BlockSpec((MOE_BM, D_MODEL // 2), lambda j, i, be, nu, hf, nx: (last(i, nu), 0)),
            pl.BlockSpec(memory_space=pl.ANY),
            pl.BlockSpec((None, None, 1, GM1_TN), lambda j, i, be, nu, hf, nx: (l, be[i], 0, j)),
            pl.BlockSpec((None, None, 1, GM1_TN), lambda j, i, be, nu, hf, nx: (l, be[i], 0, nj + j)),
        ],
        out_specs=pl.BlockSpec((MOE_BM, GM1_TN), lambda j, i, be, nu, hf, nx: (i, j)),
        scratch_shapes=[pltpu.VMEM((D_MODEL, 2 * GM1_TN), F32), pltpu.VMEM((D_MODEL, 2 * GM1_TN), BF16),
                        pltpu.VMEM((MOE_BM, D_MODEL), BF16), pltpu.SemaphoreType.DMA((2,))],
    )
    return pl.pallas_call(
        functools.partial(_gm1_kernel, l=l, nj=nj),
        grid_spec=grid_spec,
        out_shape=jax.ShapeDtypeStruct((rows, D_FF), BF16),
        compiler_params=_cp(("arbitrary", "arbitrary"), 56),
        name="moe_gate_up",
    )(*layout, xs, w_gu, b_gu, b_gu)


def _gm2_kernel(be_ref, nu_ref, hf_ref, nx_ref, a_ref, w_ref, b_ref, o_ref, wst_ref, wbf_ref, sem, *, l, nj):
    j = pl.program_id(0)
    i = pl.program_id(1)
    used = i < nu_ref[0]
    tn = GM2_TN
    half = tn // 2

    def tile_copy(jt, e):
        col = pl.multiple_of(jt * tn, tn)
        return pltpu.make_async_copy(w_ref.at[l, e, :, pl.ds(col, tn)], wst_ref, sem)

    @pl.when(jnp.logical_and(used, _new_expert(be_ref, i)))
    def _():
        e = be_ref[i]

        @pl.when(jnp.logical_and(j == 0, i == 0))
        def _():
            tile_copy(0, e).start()

        tile_copy(j, e).wait()
        wbf_ref[...] = wst_ref[...].astype(BF16)
        exists, jn, en = _next_tile(nx_ref, be_ref, i, j, nj)

        @pl.when(exists)
        def _():
            tile_copy(jn, en).start()

    def compute(rows):
        lo = jnp.dot(a_ref[:rows, :], wbf_ref[:, :half], preferred_element_type=F32) + b_ref[:, :half]
        hi = jnp.dot(a_ref[:rows, :], wbf_ref[:, half:], preferred_element_type=F32) + b_ref[:, half:]
        o_ref[:rows, :] = _pack_halves(lo, hi)
        if rows < MOE_BM:
            o_ref[rows:, :] = jnp.zeros((MOE_BM - rows, half), jnp.uint32)

    small = hf_ref[i] != 0

    @pl.when(jnp.logical_and(used, jnp.logical_not(small)))
    def _():
        compute(MOE_BM)

    @pl.when(jnp.logical_and(used, small))
    def _():
        compute(MOE_BM // 2)

    @pl.when(jnp.logical_not(used))
    def _():
        o_ref[...] = jnp.zeros_like(o_ref)


def _gm2(layout, act, w_dn, b_dn, l):
    rows = act.shape[0]
    nb = rows // MOE_BM
    nj = D_MODEL // GM2_TN
    b_dn = b_dn.reshape(DEPTH, N_EXPERTS, 1, D_MODEL)
    last = lambda i, nu: jnp.minimum(i, nu[0] - 1)
    grid_spec = pltpu.PrefetchScalarGridSpec(
        num_scalar_prefetch=4,
        grid=(nj, nb),
        in_specs=[
            pl.BlockSpec((MOE_BM, D_FF), lambda j, i, be, nu, hf, nx: (last(i, nu), 0)),
            pl.BlockSpec(memory_space=pl.ANY),
            pl.BlockSpec((None, None, 1, GM2_TN), lambda j, i, be, nu, hf, nx: (l, be[i], 0, j)),
        ],
        out_specs=pl.BlockSpec((MOE_BM, GM2_TN // 2), lambda j, i, be, nu, hf, nx: (i, j)),
        scratch_shapes=[pltpu.VMEM((D_FF, GM2_TN), F32), pltpu.VMEM((D_FF, GM2_TN), BF16),
                        pltpu.SemaphoreType.DMA(())],
    )
    return pl.pallas_call(
        functools.partial(_gm2_kernel, l=l, nj=nj),
        grid_spec=grid_spec,
        out_shape=jax.ShapeDtypeStruct((rows, D_MODEL // 2), jnp.uint32),
        compiler_params=_cp(("arbitrary", "arbitrary"), 56),
        name="moe_down",
    )(*layout, act, w_dn, b_dn)


def _block_layout(counts, n_blocks):
    padded = (counts + MOE_BM - 1) // MOE_BM * MOE_BM
    pad_end = jnp.cumsum(padded)
    pad_start = (pad_end - padded).astype(jnp.int32)
    starts = jnp.arange(n_blocks, dtype=jnp.int32) * MOE_BM
    block_e = jnp.minimum(jnp.sum(pad_end[None, :] <= starts[:, None], axis=1), N_EXPERTS - 1).astype(jnp.int32)
    n_used = (pad_end[-1] // MOE_BM).astype(jnp.int32).reshape(1)
    valid = jnp.clip(counts[block_e] - (starts - pad_start[block_e]), 0, MOE_BM)
    block_half = (valid <= MOE_BM // 2).astype(jnp.int32)
    idx = jnp.arange(N_EXPERTS, dtype=jnp.int32)
    later = jnp.where((counts > 0)[None, :] & (idx[None, :] > idx[:, None]), idx[None, :], N_EXPERTS)
    next_e = jnp.min(later, axis=1)
    block_next = jnp.where(next_e == N_EXPERTS, -1, next_e)[block_e].astype(jnp.int32)
    return pad_start, (block_e, n_used, block_half, block_next)


def _dispatch_kernel(dest_ref, ps_ref, cn_ref, h_ref, xs_ref, zero_ref, sem, zsem, *, bt):
    i = pl.program_id(0)
    base = i * bt
    nt = pl.num_programs(0) * bt

    def zero_rows(issue):
        def per_expert(e, carry):
            start = ps_ref[e] + cn_ref[e]
            end = (start + MOE_BM - 1) // MOE_BM * MOE_BM
            head_end = jnp.minimum((start + 7) // 8 * 8, end)

            def single(r, c):
                cp = pltpu.make_async_copy(zero_ref.at[pl.ds(0, 1)], xs_ref.at[pl.ds(r, 1)], zsem)
                cp.start() if issue else cp.wait()
                return c

            def chunk(q, c):
                r = pl.multiple_of(head_end + q * 8, 8)
                cp = pltpu.make_async_copy(zero_ref.at[pl.ds(0, 8)], xs_ref.at[pl.ds(r, 8)], zsem)
                cp.start() if issue else cp.wait()
                return c

            lax.fori_loop(start, head_end, single, 0)
            lax.fori_loop(0, (end - head_end) // 8, chunk, 0)
            return carry

        lax.fori_loop(0, N_EXPERTS, per_expert, 0)

        last = N_EXPERTS - 1
        used_rows = ps_ref[last] + (cn_ref[last] + MOE_BM - 1) // MOE_BM * MOE_BM

        def block(b, c):
            r = pl.multiple_of(b * MOE_BM, MOE_BM)
            cp = pltpu.make_async_copy(zero_ref, xs_ref.at[pl.ds(r, MOE_BM)], zsem)
            cp.start() if issue else cp.wait()
            return c

        lax.fori_loop(used_rows // MOE_BM, xs_ref.shape[0] // MOE_BM, block, 0)

    @pl.when(i == 0)
    def _():
        zero_ref[...] = jnp.zeros_like(zero_ref)
        zero_rows(True)

    def group(g, carry):
        r0 = pl.multiple_of(g * 8, 8)
        rows8 = h_ref.at[pl.ds(r0, 8)]
        for r in range(8):
            for k in range(TOP_K):
                dst = dest_ref[k * nt + base + r0 + r]
                pltpu.make_async_copy(rows8.at[pl.ds(r, 1)], xs_ref.at[pl.ds(dst, 1)], sem).start(priority=k % 2)
        return carry

    lax.fori_loop(0, bt // 8, group, 0)
    for _ in range(TOP_K):
        pltpu.make_async_copy(h_ref, xs_ref.at[pl.ds(0, bt)], sem).wait()

    @pl.when(i == pl.num_programs(0) - 1)
    def _():
        zero_rows(False)


def _dispatch(dest, pad_start, counts, hu, rows):
    t, w = hu.shape
    bt = DISPATCH_BT
    grid_spec = pltpu.PrefetchScalarGridSpec(
        num_scalar_prefetch=3,
        grid=(t // bt,),
        in_specs=[pl.BlockSpec((bt, w), lambda i, dst, ps, cn: (i, 0))],
        out_specs=pl.BlockSpec(memory_space=pl.ANY),
        scratch_shapes=[pltpu.VMEM((MOE_BM, w), jnp.uint32), pltpu.SemaphoreType.DMA(()),
                        pltpu.SemaphoreType.DMA(())],
    )
    return pl.pallas_call(
        functools.partial(_dispatch_kernel, bt=bt),
        grid_spec=grid_spec,
        out_shape=jax.ShapeDtypeStruct((rows, w), jnp.uint32),
        compiler_params=pltpu.CompilerParams(dimension_semantics=("arbitrary",), has_side_effects=True),
        name="moe_dispatch",
    )(dest, pad_start, counts, hu)


def _finish_kernel(dest_ref, y_ref, rg_ref, x_ref, mod_ref, g_ref, *rest, bt, nbp):
    if nbp is None:
        o_ref, buf, sem = rest
    else:
        op_ref, os_ref, buf, sem = rest
    i = pl.program_id(0)
    n = pl.num_programs(0)
    slot = i % 2
    q = GM2_TN // 2
    nj = D_MODEL // GM2_TN

    def issue8(blk, to_slot, r0):
        for k in range(TOP_K):
            rows8 = buf.at[to_slot, k, pl.ds(r0, 8)]
            for r in range(8):
                src = dest_ref[k * (n * bt) + blk * bt + r0 + r]
                pltpu.make_async_copy(y_ref.at[pl.ds(src, 1)], rows8.at[pl.ds(r, 1)], sem.at[to_slot]).start(
                    priority=k % 2)

    def issue_block(blk, to_slot):
        lax.fori_loop(0, bt // 8, lambda g, c: (issue8(blk, to_slot, pl.multiple_of(g * 8, 8)), c)[1], 0)

    def combine_rows(out_ref, r0):
        rows = pl.ds(r0, FIN_ROWS)
        rg = rg_ref[rows, :]
        parts = [None] * (2 * nj)
        for k in range(TOP_K):
            gate = rg[:, k:k + 1]
            for j in range(nj):
                lo, hi = _unpack_halves(buf[slot, k, rows, j * q:(j + 1) * q])
                for idx, val in ((2 * j, lo), (2 * j + 1, hi)):
                    parts[idx] = gate * val if k == 0 else parts[idx] + gate * val
        f = jnp.concatenate(parts, axis=-1)
        fn = f * lax.rsqrt(jnp.mean(f * f, axis=-1, keepdims=True) + EPS) * g_ref[...]
        out_ref[rows, :] = x_ref[rows, :] + mod_ref[0, 5:6, :] * fn

    def combine_all(out_ref):
        def chunk(c, carry):
            combine_rows(out_ref, pl.multiple_of(c * FIN_ROWS, FIN_ROWS))
            return carry

        lax.fori_loop(0, bt // FIN_ROWS, chunk, 0, unroll=FIN_UNROLL)

    @pl.when(i == 0)
    def _():
        issue_block(0, 0)

    @pl.when(i + 1 < n)
    def _():
        issue_block(i + 1, (i + 1) % 2)

    for k in range(TOP_K):
        pltpu.make_async_copy(y_ref.at[pl.ds(0, bt)], buf.at[slot, k], sem.at[slot]).wait()

    if nbp is None:
        combine_all(o_ref)
    else:
        @pl.when(i < nbp)
        def _():
            combine_all(op_ref)

        @pl.when(i >= nbp)
        def _():
            combine_all(os_ref)


def _finish(dest, yu, rg, x1, mod, g, tp, ls, split):
    t = x1.shape[0]
    bt = FIN_BM
    row = lambda w: pl.BlockSpec((bt, w), lambda i, dst: (i, 0))
    if split:
        nbp = tp // bt
        out_specs = (pl.BlockSpec((bt, D_MODEL), lambda i, dst: (jnp.minimum(i, nbp - 1), 0)),
                     pl.BlockSpec((bt, D_MODEL), lambda i, dst: (jnp.maximum(i - nbp, 0), 0)))
        out_shape = (jax.ShapeDtypeStruct((tp, D_MODEL), F32), jax.ShapeDtypeStruct((t - tp, D_MODEL), F32))
    else:
        nbp = None
        out_specs = row(D_MODEL)
        out_shape = jax.ShapeDtypeStruct((t, D_MODEL), F32)
    grid_spec = pltpu.PrefetchScalarGridSpec(
        num_scalar_prefetch=1,
        grid=(t // bt,),
        in_specs=[pl.BlockSpec(memory_space=pl.ANY),
                  row(ROUTER_PAD),
                  row(D_MODEL),
                  pl.BlockSpec((1, 6, D_MODEL), lambda i, dst: (_mod_row(i, bt, tp, ls), 0, 0)),
                  pl.BlockSpec((1, D_MODEL), lambda i, dst: (0, 0))],
        out_specs=out_specs,
        scratch_shapes=[pltpu.VMEM((2, TOP_K, bt, D_MODEL // 2), jnp.uint32), pltpu.SemaphoreType.DMA((2,))],
    )
    return pl.pallas_call(
        functools.partial(_finish_kernel, bt=bt, nbp=nbp),
        grid_spec=grid_spec,
        out_shape=out_shape,
        compiler_params=_cp(("arbitrary",), 48),
        name="moe_finish",
    )(dest, yu, rg, x1, mod, g)


def _moe(hu, rt, rg, cnt, x1, mod, g, w_gu, b_gu, w_dn, b_dn, l, tp, ls, split=False):
    t = hu.shape[0]
    n_blocks = -(-t * TOP_K // MOE_BM) + N_EXPERTS
    counts = cnt[0, :N_EXPERTS].astype(jnp.int32)
    pad_start, layout = _block_layout(counts, n_blocks)
    first = jnp.where(rt[:TOP_K, :, None] == jnp.arange(N_EXPERTS, dtype=jnp.int32), pad_start, 0)
    dest = (jnp.sum(first, axis=-1) + rt[TOP_K:2 * TOP_K]).reshape(-1)
    xs = _dispatch(dest, pad_start, counts, hu, n_blocks * MOE_BM)
    act = _gm1(layout, xs, w_gu, b_gu, l)
    yu = _gm2(layout, act, w_dn, b_dn, l)
    return _finish(dest, yu, rg, x1, mod, g, tp, ls, split)


def kernel(x_prompt, x_sample, cache_attn_k, cache_attn_v, cache_diff_k, cache_diff_v, c, c_ctx, w_mod, b_mod, norm_g, w_in, w_out, hy_conv_w, hy_conv_b, hy_ffn_w1, hy_ffn_b1, hy_ffn_w2, hy_ffn_b2, hy_ffn_w3, hy_sin_freq, hy_skip, sgu_norm_g, sgu_norm_b, sgu_w, sgu_b, attn_q_g, attn_k_g, diff_lambda, diff_norm_g, w_router, b_router, w_gate_up, b_gate_up, w_down, b_down):
    bp, lp, d = x_prompt.shape
    bs, ls, _ = x_sample.shape
    tp = bp * lp
    past = cache_attn_k.shape[2]
    x = jnp.concatenate([x_prompt.reshape(tp, d), x_sample.reshape(bs * ls, d)], axis=0)

    cond = jnp.concatenate([c_ctx[None, :], c, jnp.zeros((COND_ROWS - 1 - bs, d), F32)], axis=0)
    mod = _modulation(cond, w_mod, b_mod).reshape(DEPTH, COND_ROWS, 6, d)

    w_in_b = w_in.astype(BF16)
    w_out_b = w_out.astype(BF16)
    wr_f = jnp.pad(w_router, ((0, 0), (0, 0), (0, ROUTER_PAD - N_EXPERTS)))
    wr_hi = wr_f.astype(BF16)
    wr = jnp.concatenate([wr_hi, (wr_f - wr_hi.astype(F32)).astype(BF16)], axis=-1)
    br = jnp.pad(b_router, ((0, 0), (0, ROUTER_PAD - N_EXPERTS))).reshape(DEPTH, 1, ROUTER_PAD)

    pk = cache_attn_k.reshape(bs, DEPTH, past, -1)
    pv = cache_attn_v.reshape(bs, DEPTH, past, -1)
    pdk = cache_diff_k.reshape(bs, DEPTH, past, -1)
    pdv = cache_diff_v.reshape(bs, DEPTH, past, -1)

    groups = []
    for L in (lp, ls):
        cm, sm, alt8, fwd, inv = _dft_tables(L)
        z, window = _filter_consts(L)
        groups.append((L, cm, sm, alt8, fwd, inv, z, window))
    qw = ATTN_HEADS * ATTN_HEAD_DIM
    kw = ATTN_KV_HEADS * ATTN_HEAD_DIM
    attn_tables = _rope_tables(ls, ATTN_HEAD_DIM, qw) + _rope_tables(ls, ATTN_HEAD_DIM, kw)
    diff_tables = _rope_tables(ls, DIFF_QK_DIM, GROUP_WIDTH)

    new_k, new_v, new_dk, new_dv = [], [], [], []
    for l in range(DEPTH):
        ng = norm_g[l]
        proj = _in_proj(x, mod[l], ng[0:1], w_in_b[l], tp, ls)

        ys = []
        for gi, (B, row_off) in enumerate(((bp, 0), (bs, tp // ls))):
            L, cm, sm, alt8, fwd, inv, z, window = groups[gi]
            kre, kim = _hyena_filter(L, z, window, cm, sm, alt8, hy_ffn_w1[l], hy_ffn_b1[l], hy_ffn_w2[l],
                                     hy_ffn_b2[l], hy_ffn_w3[l], hy_sin_freq[l])
            ya = _hyena(proj, row_off, B, L, hy_conv_w[l], hy_conv_b[l], kre, kim, hy_skip[l], fwd, inv)
            yb = _sgu(proj, row_off, B, L, sgu_norm_g[l], sgu_norm_b[l], sgu_w[l], sgu_b[l])
            if gi == 0:
                yc, k_ctx, v_ctx = _gqa(proj, row_off, B, L, attn_q_g[l], attn_k_g[l], l)
                yd, dk_ctx, dv_ctx = _diff(proj, row_off, B, L, diff_lambda[l], diff_norm_g[l], l)
                new_k.append(k_ctx.reshape(bp, lp, ATTN_KV_HEADS, ATTN_HEAD_DIM))
                new_v.append(v_ctx.reshape(bp, lp, ATTN_KV_HEADS, ATTN_HEAD_DIM))
                new_dk.append(dk_ctx.reshape(bp, lp, DIFF_HEADS, 2, DIFF_QK_DIM))
                new_dv.append(dv_ctx.reshape(bp, lp, DIFF_HEADS, DIFF_V_DIM))
            else:
                yc = _gqa(proj, row_off, B, L, attn_q_g[l], attn_k_g[l], l, past=(pk, pv), tables=attn_tables)
                yd = _diff(proj, row_off, B, L, diff_lambda[l], diff_norm_g[l], l, past=(pdk, pdv),
                           tables=diff_tables)
            ys.append((ya, yb, yc, yd))
        x1, hu, rt, rg, cnt = _out_proj(ys[0], ys[1], w_out_b[l], x, mod[l], ng, wr[l], br[l], tp, ls)
        x = _moe(hu, rt, rg, cnt, x1, mod[l], ng[3:4], w_gate_up, b_gate_up, w_down, b_down, l, tp, ls,
                 split=(l == DEPTH - 1))

    y_prompt = x[0].reshape(bp, lp, d)
    y_sample = x[1].reshape(bs, ls, d)
    return (y_prompt, y_sample, jnp.stack(new_k, axis=1), jnp.stack(new_v, axis=1),
            jnp.stack(new_dk, axis=1), jnp.stack(new_dv, axis=1))
```
